```python
import jax, jax.numpy as jnp
from jax import lax
import numpy as np

D_MODEL = 2048
BATCH = 8
SEQ = 2048
DEPTH = 1

MIX_WIDTH = D_MODEL
HG_WIDTH = MIX_WIDTH // 2
HG_HEAD_DIM = 128
HG_HEADS = HG_WIDTH // HG_HEAD_DIM
HG_CHUNK = 64
NSA_WIDTH = MIX_WIDTH - HG_WIDTH
NSA_HEAD_DIM = 64
NSA_Q_HEADS = NSA_WIDTH // NSA_HEAD_DIM
NSA_KV_HEADS = 4
NSA_GROUP = NSA_Q_HEADS // NSA_KV_HEADS
CMP_BLOCK = 32
CMP_STRIDE = 16
CMP_HIDDEN = 256
SEL_BLOCK = 64
N_SELECT = 16
N_LOCAL = 2
WINDOW = 512
Q_BLOCK = 128
SEL_Q_BLOCK = 32
FORCE_SCORE = 1e9
NEG_INF = -1e30
PEER_HEADS = 8
N_KEYS = 128
N_EXPERTS = N_KEYS * N_KEYS
PEER_KEY_DIM = 256
PEER_TOPK = 16
PEER_TOKEN_BLOCK = 128
NORM_EPS = 1e-6

KV_W = NSA_KV_HEADS * NSA_HEAD_DIM
N_GATES = 3 * NSA_Q_HEADS
IN_SIZES = [HG_WIDTH, HG_WIDTH, HG_WIDTH, HG_WIDTH,
            NSA_WIDTH,
            KV_W, KV_W, KV_W, KV_W, KV_W, KV_W,
            N_GATES]
IN_COLS = sum(IN_SIZES)

kernel_name = "hymba_hgrn2_nsa_peer"


def rms_norm(x, w):
    xf = x.astype(jnp.float32)
    y = xf * lax.rsqrt(jnp.mean(xf * xf, axis=-1, keepdims=True) + NORM_EPS)
    return (y * w.astype(jnp.float32)).astype(x.dtype)


def masked_softmax(s, mask):
    p = jax.nn.softmax(jnp.where(mask, s, NEG_INF), axis=-1)
    return jnp.where(mask, p, 0.0)


def alibi_slopes(n):
    return jnp.asarray(2.0 ** (-8.0 * np.arange(1, n + 1) / n), dtype=jnp.float32)


def hgrn2_mixer(q, f_logit, i_in, g, lb, norm_w):
    B, T, _ = q.shape
    H, dk, C = HG_HEADS, HG_HEAD_DIM, HG_CHUNK
    nc = T // C
    f = lb + (1.0 - lb) * jax.nn.sigmoid(f_logit.astype(jnp.float32))

    def chunks(a):
        return a.astype(jnp.float32).reshape(B, nc, C, H, dk).transpose(0, 3, 1, 2, 4)

    qc, kc, vc, lfc = chunks(q), chunks(1.0 - f), chunks(i_in), chunks(jnp.log(f))
    b = jnp.cumsum(lfc, axis=3)
    b_end = b[:, :, :, -1:, :]
    q_dec = qc * jnp.exp(b)
    causal = np.tril(np.ones((C, C), dtype=bool))
    attn = jnp.where(causal, jnp.einsum('bhntc,bhnsc->bhnts', q_dec, kc * jnp.exp(-b)), 0.0)
    o_intra = jnp.einsum('bhnts,bhnsv->bhntv', attn, vc)
    upd = jnp.einsum('bhnsc,bhnsv->bhncv', kc * jnp.exp(b_end - b), vc)
    dec = jnp.exp(b_end[:, :, :, 0, :])

    def step(state, inp):
        d, u = inp
        return d[..., None] * state + u, state

    s0 = jnp.zeros((B, H, dk, dk), jnp.float32)
    _, s_prev = lax.scan(step, s0, (jnp.moveaxis(dec, 2, 0), jnp.moveaxis(upd, 2, 0)))
    o_inter = jnp.einsum('bhntc,nbhcv->bhntv', q_dec, s_prev)
    o = (o_intra + o_inter).transpose(0, 2, 3, 1, 4).reshape(B, T, H, dk)
    o = rms_norm(o, norm_w) * jax.nn.silu(g.astype(jnp.float32).reshape(B, T, H, dk))
    return o.reshape(B, T, H * dk).astype(q.dtype)


def nsa_mixer(q, kc, vc, ks, vs, kw, vw, gate_logits, q_norm_w, kc_norm_w, ks_norm_w, kw_norm_w,
              pos_k, pos_v, w_ck1, w_ck2, w_cv1, w_cv2):
    B, T, _ = q.shape
    Hkv, G, dh = NSA_KV_HEADS, NSA_GROUP, NSA_HEAD_DIM
    f32 = jnp.float32
    slopes = alibi_slopes(NSA_Q_HEADS).reshape(Hkv, G)
    qn = rms_norm(q.reshape(B, T, Hkv, G, dh), q_norm_w).transpose(0, 2, 3, 1, 4) * (dh ** -0.5)

    def kv_heads(a):
        return a.reshape(B, T, Hkv, dh).transpose(0, 2, 1, 3)

    t_pos = np.arange(T)

    n_cmp = (T - CMP_BLOCK) // CMP_STRIDE + 1
    cmp_start = np.arange(n_cmp) * CMP_STRIDE
    blk_idx = cmp_start[:, None] + np.arange(CMP_BLOCK)[None, :]
    cmp_end = cmp_start + CMP_BLOCK - 1

    def compress(a, pos, w1, w2):
        blocks = kv_heads(a)[:, :, blk_idx] + pos
        flat = blocks.reshape(B, Hkv, n_cmp, CMP_BLOCK * dh)
        return jax.nn.gelu(flat @ w1) @ w2

    k_cmp = rms_norm(compress(kc, pos_k, w_ck1, w_ck2), kc_norm_w)
    v_cmp = compress(vc, pos_v, w_cv1, w_cv2)
    dist_c = t_pos[:, None] - cmp_end[None, :]
    s_c = jnp.einsum('bgrtd,bgnd->bgrtn', qn, k_cmp).astype(f32) \
        - slopes[:, :, None, None] * dist_c.astype(np.float32)
    p_cmp = masked_softmax(s_c, dist_c >= 0)
    o_cmp = jnp.einsum('bgrtn,bgnd->bgrtd', p_cmp, v_cmp)

    n_sel = T // SEL_BLOCK
    sel_start = np.arange(n_sel) * SEL_BLOCK
    overlap = ((cmp_start[:, None] < sel_start[None, :] + SEL_BLOCK)
               & (cmp_start[:, None] + CMP_BLOCK > sel_start[None, :])).astype(np.float32)
    imp = jnp.einsum('bgrtn,nj->bgtj', p_cmp, overlap)
    cur = t_pos // SEL_BLOCK
    j = np.arange(n_sel)
    forced = (j[None, :] == 0) | ((j[None, :] <= cur[:, None]) & (j[None, :] > cur[:, None] - N_LOCAL))
    future = j[None, :] > cur[:, None]
    imp = jnp.where(forced, FORCE_SCORE, jnp.where(future, -FORCE_SCORE, imp))
    n_top = min(N_SELECT, n_sel)
    _, sel_idx = lax.top_k(imp, n_top)

    k_sel = kv_heads(rms_norm(ks.reshape(B, T, Hkv * 1, dh), ks_norm_w).reshape(B, T, KV_W)) \
        .reshape(B, Hkv, n_sel, SEL_BLOCK, dh)
    v_sel = kv_heads(vs).reshape(B, Hkv, n_sel, SEL_BLOCK, dh)
    bi = jnp.arange(B)[:, None, None, None]
    gi = jnp.arange(Hkv)[None, :, None, None]
    m_keys = n_top * SEL_BLOCK

    def sel_block(args):
        q_b, idx_b, start = args
        k_g = k_sel[bi, gi, idx_b].reshape(B, Hkv, SEL_Q_BLOCK, m_keys, dh)
        v_g = v_sel[bi, gi, idx_b].reshape(B, Hkv, SEL_Q_BLOCK, m_keys, dh)
        key_pos = (idx_b[..., None] * SEL_BLOCK + jnp.arange(SEL_BLOCK)).reshape(B, Hkv, SEL_Q_BLOCK, m_keys)
        dist = ((start + jnp.arange(SEL_Q_BLOCK))[:, None] - key_pos)[:, :, None]
        s = jnp.einsum('bgrqd,bgqmd->bgrqm', q_b, k_g).astype(f32) \
            - slopes[None, :, :, None, None] * dist.astype(f32)
        p = masked_softmax(s, dist >= 0)
        return jnp.einsum('bgrqm,bgqmd->bgrqd', p, v_g)

    n_qb = T // SEL_Q_BLOCK
    q_blocks = qn.reshape(B, Hkv, G, n_qb, SEL_Q_BLOCK, dh).transpose(3, 0, 1, 2, 4, 5)
    idx_blocks = sel_idx.reshape(B, Hkv, n_qb, SEL_Q_BLOCK, n_top).transpose(2, 0, 1, 3, 4)
    starts = jnp.arange(n_qb, dtype=jnp.int32) * SEL_Q_BLOCK
    o_sel = lax.map(sel_block, (q_blocks, idx_blocks, starts))
    o_sel = o_sel.transpose(1, 2, 3, 0, 4, 5).reshape(B, Hkv, G, T, dh)

    pad = ((0, 0), (0, 0), (WINDOW, 0), (0, 0))
    k_win = jnp.pad(kv_heads(rms_norm(kw.reshape(B, T, Hkv, dh), kw_norm_w).reshape(B, T, KV_W)), pad)
    v_win = jnp.pad(kv_heads(vw), pad)
    span = Q_BLOCK + WINDOW
    dist_w = np.arange(Q_BLOCK)[:, None] + WINDOW - np.arange(span)[None, :]

    def win_block(args):
        q_b, start = args
        k_b = lax.dynamic_slice_in_dim(k_win, start, span, axis=2)
        v_b = lax.dynamic_slice_in_dim(v_win, start, span, axis=2)
        key_pos = start - WINDOW + jnp.arange(span)
        mask = (dist_w >= 0) & (dist_w < WINDOW) & (key_pos >= 0)[None, :]
        s = jnp.einsum('bgrqd,bgkd->bgrqk', q_b, k_b).astype(f32) \
            - slopes[:, :, None, None] * dist_w.astype(np.float32)
        p = masked_softmax(s, mask)
        return jnp.einsum('bgrqk,bgkd->bgrqd', p, v_b)

    n_wb = T // Q_BLOCK
    q_wblocks = qn.reshape(B, Hkv, G, n_wb, Q_BLOCK, dh).transpose(3, 0, 1, 2, 4, 5)
    w_starts = jnp.arange(n_wb, dtype=jnp.int32) * Q_BLOCK
    o_win = lax.map(win_block, (q_wblocks, w_starts))
    o_win = o_win.transpose(1, 2, 3, 0, 4, 5).reshape(B, Hkv, G, T, dh)

    gates = jax.nn.sigmoid(gate_logits.astype(f32)).reshape(B, T, Hkv, G, 3).transpose(0, 2, 3, 1, 4)
    o = gates[..., 0:1] * o_cmp + gates[..., 1:2] * o_sel + gates[..., 2:3] * o_win
    return o.transpose(0, 3, 1, 2, 4).reshape(B, T, NSA_WIDTH).astype(q.dtype)


def peer_ffn(x, w_q, sub_keys, u_tab, v_tab):
    B, T, D = x.shape
    n = B * T
    H, K = PEER_HEADS, PEER_TOPK
    xt = x.reshape(n, D)
    q = (xt @ w_q).reshape(n, H, 2, PEER_KEY_DIM // 2)
    s = jnp.einsum('nhpd,hpkd->nhpk', q, sub_keys).astype(jnp.float32)
    s_top, i_top = lax.top_k(s, K)
    cand_s = (s_top[:, :, 0, :, None] + s_top[:, :, 1, None, :]).reshape(n, H, K * K)
    cand_i = (i_top[:, :, 0, :, None] * N_KEYS + i_top[:, :, 1, None, :]).reshape(n, H, K * K)
    best_s, best_pos = lax.top_k(cand_s, K)
    expert = jnp.take_along_axis(cand_i, best_pos, axis=-1)
    gate = jax.nn.softmax(best_s, axis=-1).astype(x.dtype)
    nb = n // PEER_TOKEN_BLOCK

    def block(args):
        xb, eb, gb = args
        act = jax.nn.gelu(jnp.einsum('td,thkd->thk', xb, u_tab[eb]))
        return jnp.einsum('thk,thkd->td', gb * act, v_tab[eb])

    out = lax.map(block, (xt.reshape(nb, PEER_TOKEN_BLOCK, D),
                          expert.reshape(nb, PEER_TOKEN_BLOCK, H, K),
                          gate.reshape(nb, PEER_TOKEN_BLOCK, H, K)))
    return out.reshape(B, T, D)


def setup_inputs(seed: int = 0) -> dict:
    key = jax.random.key(seed)
    ks = jax.random.split(key, 24)
    f32 = jnp.float32
    L, dh = DEPTH, NSA_HEAD_DIM

    def nrm(k, shape, scale):
        return jax.random.normal(k, shape, f32) * scale

    def gain(k, shape):
        return 1.0 + 0.01 * jax.random.normal(k, shape, f32)

    return {
        "x": nrm(ks[0], (BATCH, SEQ, D_MODEL), 1.0),
        "norm1_w": gain(ks[1], (L, D_MODEL)),
        "w_in": nrm(ks[2], (L, D_MODEL, IN_COLS), D_MODEL ** -0.5),
        "hg_lb_logits": nrm(ks[3], (L + 1, HG_WIDTH), 0.1),
        "hg_norm_w": gain(ks[4], (L, HG_HEAD_DIM)),
        "q_norm_w": gain(ks[5], (L, dh)),
        "kc_norm_w": gain(ks[6], (L, dh)),
        "ks_norm_w": gain(ks[7], (L, dh)),
        "kw_norm_w": gain(ks[8], (L, dh)),
        "cmp_pos_k": nrm(ks[9], (L, CMP_BLOCK, dh), 0.1),
        "cmp_pos_v": nrm(ks[10], (L, CMP_BLOCK, dh), 0.1),
        "w_ck1": nrm(ks[11], (L, CMP_BLOCK * dh, CMP_HIDDEN), (CMP_BLOCK * dh) ** -0.5),
        "w_ck2": nrm(ks[12], (L, CMP_HIDDEN, dh), CMP_HIDDEN ** -0.5),
        "w_cv1": nrm(ks[13], (L, CMP_BLOCK * dh, CMP_HIDDEN), (CMP_BLOCK * dh) ** -0.5),
        "w_cv2": nrm(ks[14], (L, CMP_HIDDEN, dh), CMP_HIDDEN ** -0.5),
        "w_out": nrm(ks[15], (L, MIX_WIDTH, D_MODEL), MIX_WIDTH ** -0.5),
        "norm2_w": gain(ks[16], (L, D_MODEL)),
        "peer_w_q": nrm(ks[17], (L, D_MODEL, PEER_HEADS * PEER_KEY_DIM), D_MODEL ** -0.5),
        "peer_sub_keys": nrm(ks[18], (L, PEER_HEADS, 2, N_KEYS, PEER_KEY_DIM // 2), (PEER_KEY_DIM // 2) ** -0.5),
        "peer_u": nrm(ks[19], (L, N_EXPERTS, D_MODEL), D_MODEL ** -0.5),
        "peer_v": nrm(ks[20], (L, N_EXPERTS, D_MODEL), PEER_HEADS ** -0.5),
    }


def reference(x, norm1_w, w_in, hg_lb_logits, hg_norm_w, q_norm_w, kc_norm_w, ks_norm_w, kw_norm_w,
              cmp_pos_k, cmp_pos_v, w_ck1, w_ck2, w_cv1, w_cv2, w_out, norm2_w,
              peer_w_q, peer_sub_keys, peer_u, peer_v):
    split_at = np.cumsum(IN_SIZES)[:-1].tolist()
    lower_bounds = jnp.cumsum(jax.nn.softmax(hg_lb_logits.astype(jnp.float32), axis=0), axis=0)
    h = x
    for layer in range(DEPTH):
        hn = rms_norm(h, norm1_w[layer])
        (hq, hf, hi, hg, nq, nkc, nvc, nks, nvs, nkw, nvw, ngate) = jnp.split(hn @ w_in[layer], split_at, axis=-1)
        hg_out = hgrn2_mixer(hq, hf, hi, hg, lower_bounds[layer], hg_norm_w[layer])
        nsa_out = nsa_mixer(nq, nkc, nvc, nks, nvs, nkw, nvw, ngate,
                            q_norm_w[layer], kc_norm_w[layer], ks_norm_w[layer], kw_norm_w[layer],
                            cmp_pos_k[layer], cmp_pos_v[layer], w_ck1[layer], w_ck2[layer],
                            w_cv1[layer], w_cv2[layer])
        h = h + jnp.concatenate([hg_out, nsa_out], axis=-1) @ w_out[layer]
        h = h + peer_ffn(rms_norm(h, norm2_w[layer]), peer_w_q[layer], peer_sub_keys[layer],
                         peer_u[layer], peer_v[layer])
    return h
```

```python
import functools

import jax
import jax.numpy as jnp
import numpy as np
from jax import lax
from jax.experimental import pallas as pl
from jax.experimental.pallas import tpu as pltpu

D_MODEL = 2048
HG_WIDTH = 1024
HG_HEAD_DIM = 128
HG_HEADS = 8
HG_CHUNK = 64
NSA_WIDTH = 1024
NSA_HEAD_DIM = 64
NSA_Q_HEADS = 16
NSA_KV_HEADS = 4
NSA_GROUP = 4
CMP_BLOCK = 32
CMP_STRIDE = 16
SEL_BLOCK = 64
N_SELECT = 16
N_LOCAL = 2
WINDOW = 512
Q_BLOCK = 128
SEL_Q_BLOCK = 32
FORCE_SCORE = 1e9
NEG_INF = -1e30
PEER_HEADS = 8
N_KEYS = 128
PEER_KEY_DIM = 256
PEER_TOPK = 16
PEER_TOKEN_BLOCK = 128
NORM_EPS = 1e-6
KV_W = NSA_KV_HEADS * NSA_HEAD_DIM
N_GATES = 3 * NSA_Q_HEADS
IN_SIZES = [HG_WIDTH] * 4 + [NSA_WIDTH] + [KV_W] * 6 + [N_GATES]
IN_COLS = sum(IN_SIZES)

VMEM_LIMIT_BYTES = 48 * 1024 * 1024


def _norm_matmul_body(x_ref, g_ref, w_ref, o_ref, xn_ref):
    @pl.when(pl.program_id(1) == 0)
    def _():
        x = x_ref[...]
        r = lax.rsqrt(jnp.mean(x * x, axis=-1, keepdims=True) + NORM_EPS)
        xn_ref[...] = (x * r * g_ref[...]).astype(jnp.bfloat16)

    o_ref[...] = jnp.dot(xn_ref[...], w_ref[...], preferred_element_type=jnp.float32).astype(o_ref.dtype)


def _norm_matmul(x, gain, w, tm, tn, out_dtype=jnp.float32):
    m, k = x.shape
    n = w.shape[1]
    return pl.pallas_call(
        _norm_matmul_body,
        grid=(m // tm, n // tn),
        in_specs=[
            pl.BlockSpec((tm, k), lambda i, j: (i, 0)),
            pl.BlockSpec((1, k), lambda i, j: (0, 0)),
            pl.BlockSpec((k, tn), lambda i, j: (0, j)),
        ],
        out_specs=pl.BlockSpec((tm, tn), lambda i, j: (i, j)),
        out_shape=jax.ShapeDtypeStruct((m, n), out_dtype),
        scratch_shapes=[pltpu.VMEM((tm, k), jnp.bfloat16)],
        compiler_params=pltpu.CompilerParams(
            dimension_semantics=("arbitrary", "arbitrary"), vmem_limit_bytes=VMEM_LIMIT_BYTES),
        name="norm_matmul",
    )(x, gain.reshape(1, k), w)


def _matmul_res_body(a_ref, w_ref, r_ref, o_ref):
    o_ref[...] = r_ref[...] + jnp.dot(a_ref[...].astype(jnp.bfloat16), w_ref[...],
                                      preferred_element_type=jnp.float32)


def _matmul_res(a, w, res, tm, tn):
    m, k = a.shape
    n = w.shape[1]
    return pl.pallas_call(
        _matmul_res_body,
        grid=(m // tm, n // tn),
        in_specs=[
            pl.BlockSpec((tm, k), lambda i, j: (i, 0)),
            pl.BlockSpec((k, tn), lambda i, j: (0, j)),
            pl.BlockSpec((tm, tn), lambda i, j: (i, j)),
        ],
        out_specs=pl.BlockSpec((tm, tn), lambda i, j: (i, j)),
        out_shape=jax.ShapeDtypeStruct((m, n), jnp.float32),
        compiler_params=pltpu.CompilerParams(
            dimension_semantics=("arbitrary", "arbitrary"), vmem_limit_bytes=VMEM_LIMIT_BYTES),
        name="matmul_res",
    )(a, w, res)


def _rms_norm(x, w):
    return x * lax.rsqrt(jnp.mean(x * x, axis=-1, keepdims=True) + NORM_EPS) * w


def _masked_softmax(s, mask):
    p = jax.nn.softmax(jnp.where(mask, s, NEG_INF), axis=-1)
    return jnp.where(mask, p, 0.0)


def _alibi_slopes(n):
    return jnp.asarray(2.0 ** (-8.0 * np.arange(1, n + 1) / n), dtype=jnp.float32)


def _hgrn2(q, f_logit, i_in, g, lb, norm_w):
    B, T, _ = q.shape
    H, dk, C = HG_HEADS, HG_HEAD_DIM, HG_CHUNK
    nc = T // C
    f = lb + (1.0 - lb) * jax.nn.sigmoid(f_logit)

    def chunks(a):
        return a.reshape(B, nc, C, H, dk).transpose(0, 3, 1, 2, 4)

    qc, kc, vc, lfc = chunks(q), chunks(1.0 - f), chunks(i_in), chunks(jnp.log(f))
    b = jnp.cumsum(lfc, axis=3)
    b_end = b[:, :, :, -1:, :]
    q_dec = qc * jnp.exp(b)
    causal = np.tril(np.ones((C, C), dtype=bool))
    attn = jnp.where(causal, jnp.einsum('bhntc,bhnsc->bhnts', q_dec, kc * jnp.exp(-b)), 0.0)
    o_intra = jnp.einsum('bhnts,bhnsv->bhntv', attn, vc)
    upd = jnp.einsum('bhnsc,bhnsv->bhncv', kc * jnp.exp(b_end - b), vc)
    dec = jnp.exp(b_end[:, :, :, 0, :])

    def step(state, inp):
        d, u = inp
        return d[..., None] * state + u, state

    s0 = jnp.zeros((B, H, dk, dk), jnp.float32)
    _, s_prev = lax.scan(step, s0, (jnp.moveaxis(dec, 2, 0), jnp.moveaxis(upd, 2, 0)))
    o_inter = jnp.einsum('bhntc,nbhcv->bhntv', q_dec, s_prev)
    o = (o_intra + o_inter).transpose(0, 2, 3, 1, 4).reshape(B, T, H, dk)
    o = _rms_norm(o, norm_w) * jax.nn.silu(g.reshape(B, T, H, dk))
    return o.reshape(B, T, H * dk)


def _nsa(q, kc, vc, ks, vs, kw, vw, gate_logits, q_norm_w, kc_norm_w, ks_norm_w, kw_norm_w,
         pos_k, pos_v, w_ck1, w_ck2, w_cv1, w_cv2):
    B, T, _ = q.shape
    Hkv, G, dh = NSA_KV_HEADS, NSA_GROUP, NSA_HEAD_DIM
    f32 = jnp.float32
    slopes = _alibi_slopes(NSA_Q_HEADS).reshape(Hkv, G)
    qn = _rms_norm(q.reshape(B, T, Hkv, G, dh), q_norm_w).transpose(0, 2, 3, 1, 4) * (dh ** -0.5)

    def kv_heads(a):
        return a.reshape(B, T, Hkv, dh).transpose(0, 2, 1, 3)

    t_pos = np.arange(T)
    n_cmp = (T - CMP_BLOCK) // CMP_STRIDE + 1
    cmp_start = np.arange(n_cmp) * CMP_STRIDE
    blk_idx = cmp_start[:, None] + np.arange(CMP_BLOCK)[None, :]
    cmp_end = cmp_start + CMP_BLOCK - 1

    def compress(a, pos, w1, w2):
        blocks = kv_heads(a)[:, :, blk_idx] + pos
        flat = blocks.reshape(B, Hkv, n_cmp, CMP_BLOCK * dh)
        return jax.nn.gelu(flat @ w1) @ w2

    k_cmp = _rms_norm(compress(kc, pos_k, w_ck1, w_ck2), kc_norm_w)
    v_cmp = compress(vc, pos_v, w_cv1, w_cv2)
    dist_c = t_pos[:, None] - cmp_end[None, :]
    s_c = jnp.einsum('bgrtd,bgnd->bgrtn', qn, k_cmp) - slopes[:, :, None, None] * dist_c.astype(np.float32)
    p_cmp = _masked_softmax(s_c, dist_c >= 0)
    o_cmp = jnp.einsum('bgrtn,bgnd->bgrtd', p_cmp, v_cmp)

    n_sel = T // SEL_BLOCK
    sel_start = np.arange(n_sel) * SEL_BLOCK
    overlap = ((cmp_start[:, None] < sel_start[None, :] + SEL_BLOCK)
               & (cmp_start[:, None] + CMP_BLOCK > sel_start[None, :])).astype(np.float32)
    imp = jnp.einsum('bgrtn,nj->bgtj', p_cmp, overlap)
    cur = t_pos // SEL_BLOCK
    j = np.arange(n_sel)
    forced = (j[None, :] == 0) | ((j[None, :] <= cur[:, None]) & (j[None, :] > cur[:, None] - N_LOCAL))
    future = j[None, :] > cur[:, None]
    imp = jnp.where(forced, FORCE_SCORE, jnp.where(future, -FORCE_SCORE, imp))
    n_top = min(N_SELECT, n_sel)
    _, sel_idx = lax.top_k(imp, n_top)

    k_sel = kv_heads(_rms_norm(ks.reshape(B, T, Hkv, dh), ks_norm_w).reshape(B, T, KV_W)) \
        .reshape(B, Hkv, n_sel, SEL_BLOCK, dh)
    v_sel = kv_heads(vs).reshape(B, Hkv, n_sel, SEL_BLOCK, dh)
    bi = jnp.arange(B)[:, None, None, None]
    gi = jnp.arange(Hkv)[None, :, None, None]
    m_keys = n_top * SEL_BLOCK

    def sel_block(args):
        q_b, idx_b, start = args
        k_g = k_sel[bi, gi, idx_b].reshape(B, Hkv, SEL_Q_BLOCK, m_keys, dh)
        v_g = v_sel[bi, gi, idx_b].reshape(B, Hkv, SEL_Q_BLOCK, m_keys, dh)
        key_pos = (idx_b[..., None] * SEL_BLOCK + jnp.arange(SEL_BLOCK)).reshape(B, Hkv, SEL_Q_BLOCK, m_keys)
        dist = ((start + jnp.arange(SEL_Q_BLOCK))[:, None] - key_pos)[:, :, None]
        s = jnp.einsum('bgrqd,bgqmd->bgrqm', q_b, k_g) - slopes[None, :, :, None, None] * dist.astype(f32)
        p = _masked_softmax(s, dist >= 0)
        return jnp.einsum('bgrqm,bgqmd->bgrqd', p, v_g)

    n_qb = T // SEL_Q_BLOCK
    q_blocks = qn.reshape(B, Hkv, G, n_qb, SEL_Q_BLOCK, dh).transpose(3, 0, 1, 2, 4, 5)
    idx_blocks = sel_idx.reshape(B, Hkv, n_qb, SEL_Q_BLOCK, n_top).transpose(2, 0, 1, 3, 4)
    starts = jnp.arange(n_qb, dtype=jnp.int32) * SEL_Q_BLOCK
    o_sel = lax.map(sel_block, (q_blocks, idx_blocks, starts))
    o_sel = o_sel.transpose(1, 2, 3, 0, 4, 5).reshape(B, Hkv, G, T, dh)

    pad = ((0, 0), (0, 0), (WINDOW, 0), (0, 0))
    k_win = jnp.pad(kv_heads(_rms_norm(kw.reshape(B, T, Hkv, dh), kw_norm_w).reshape(B, T, KV_W)), pad)
    v_win = jnp.pad(kv_heads(vw), pad)
    span = Q_BLOCK + WINDOW
    dist_w = np.arange(Q_BLOCK)[:, None] + WINDOW - np.arange(span)[None, :]

    def win_block(args):
        q_b, start = args
        k_b = lax.dynamic_slice_in_dim(k_win, start, span, axis=2)
        v_b = lax.dynamic_slice_in_dim(v_win, start, span, axis=2)
        key_pos = start - WINDOW + jnp.arange(span)
        mask = (dist_w >= 0) & (dist_w < WINDOW) & (key_pos >= 0)[None, :]
        s = jnp.einsum('bgrqd,bgkd->bgrqk', q_b, k_b) - slopes[:, :, None, None] * dist_w.astype(np.float32)
        p = _masked_softmax(s, mask)
        return jnp.einsum('bgrqk,bgkd->bgrqd', p, v_b)

    n_wb = T // Q_BLOCK
    q_wblocks = qn.reshape(B, Hkv, G, n_wb, Q_BLOCK, dh).transpose(3, 0, 1, 2, 4, 5)
    w_starts = jnp.arange(n_wb, dtype=jnp.int32) * Q_BLOCK
    o_win = lax.map(win_block, (q_wblocks, w_starts))
    o_win = o_win.transpose(1, 2, 3, 0, 4, 5).reshape(B, Hkv, G, T, dh)

    gates = jax.nn.sigmoid(gate_logits).reshape(B, T, Hkv, G, 3).transpose(0, 2, 3, 1, 4)
    o = gates[..., 0:1] * o_cmp + gates[..., 1:2] * o_sel + gates[..., 2:3] * o_win
    return o.transpose(0, 3, 1, 2, 4).reshape(B, T, NSA_WIDTH)


def _peer(xt, q, sub_keys, u_tab, v_tab):
    n, D = xt.shape
    H, K = PEER_HEADS, PEER_TOPK
    q = q.reshape(n, H, 2, PEER_KEY_DIM // 2)
    s = jnp.einsum('nhpd,hpkd->nhpk', q, sub_keys)
    s_top, i_top = lax.top_k(s, K)
    cand_s = (s_top[:, :, 0, :, None] + s_top[:, :, 1, None, :]).reshape(n, H, K * K)
    cand_i = (i_top[:, :, 0, :, None] * N_KEYS + i_top[:, :, 1, None, :]).reshape(n, H, K * K)
    best_s, best_pos = lax.top_k(cand_s, K)
    expert = jnp.take_along_axis(cand_i, best_pos, axis=-1)
    gate = jax.nn.softmax(best_s, axis=-1)
    nb = n // PEER_TOKEN_BLOCK

    def block(args):
        xb, eb, gb = args
        act = jax.nn.gelu(jnp.einsum('td,thkd->thk', xb, u_tab[eb]))
        return jnp.einsum('thk,thkd->td', gb * act, v_tab[eb])

    out = lax.map(block, (xt.reshape(nb, PEER_TOKEN_BLOCK, D),
                          expert.reshape(nb, PEER_TOKEN_BLOCK, H, K),
                          gate.reshape(nb, PEER_TOKEN_BLOCK, H, K)))
    return out.reshape(n, D)


def kernel(x, norm1_w, w_in, hg_lb_logits, hg_norm_w, q_norm_w, kc_norm_w, ks_norm_w, kw_norm_w,
           cmp_pos_k, cmp_pos_v, w_ck1, w_ck2, w_cv1, w_cv2, w_out, norm2_w,
           peer_w_q, peer_sub_keys, peer_u, peer_v):
    B, T, D = x.shape
    n = B * T
    layer = 0
    split_at = np.cumsum(IN_SIZES)[:-1].tolist()
    lower_bounds = jnp.cumsum(jax.nn.softmax(hg_lb_logits, axis=0), axis=0)
    xt = x.reshape(n, D)

    n_pad = 7168
    w_in_b = jnp.pad(w_in[layer].astype(jnp.bfloat16), ((0, 0), (0, n_pad - IN_COLS)))
    proj = _norm_matmul(xt, norm1_w[layer], w_in_b, 512, 512)[:, :IN_COLS].reshape(B, T, IN_COLS)
    (hq, hf, hi, hg, nq, nkc, nvc, nks, nvs, nkw, nvw, ngate) = jnp.split(proj, split_at, axis=-1)
    hg_out = _hgrn2(hq, hf, hi, hg, lower_bounds[layer], hg_norm_w[layer])
    nsa_out = _nsa(nq, nkc, nvc, nks, nvs, nkw, nvw, ngate,
                   q_norm_w[layer], kc_norm_w[layer], ks_norm_w[layer], kw_norm_w[layer],
                   cmp_pos_k[layer], cmp_pos_v[layer], w_ck1[layer], w_ck2[layer],
                   w_cv1[layer], w_cv2[layer])
    mix = jnp.concatenate([hg_out, nsa_out], axis=-1).reshape(n, D)
    h = _matmul_res(mix, w_out[layer].astype(jnp.bfloat16), xt, 512, 512)

    hn = _rms_norm(h, norm2_w[layer])
    pq = _norm_matmul(h, norm2_w[layer], peer_w_q[layer].astype(jnp.bfloat16), 512, 512)
    y = h + _peer(hn, pq, peer_sub_keys[layer], peer_u[layer], peer_v[layer])
    return y.reshape(B, T, D)
```

```python
import functools

import jax
import jax.numpy as jnp
import numpy as np
from jax import lax
from jax.experimental import pallas as pl
from jax.experimental.pallas import tpu as pltpu

D_MODEL = 2048
HG_WIDTH = 1024
HG_HEAD_DIM = 128
HG_HEADS = 8
HG_CHUNK = 64
NSA_WIDTH = 1024
NSA_HEAD_DIM = 64
NSA_Q_HEADS = 16
NSA_KV_HEADS = 4
NSA_GROUP = 4
CMP_BLOCK = 32
CMP_STRIDE = 16
SEL_BLOCK = 64
N_SELECT = 16
N_LOCAL = 2
WINDOW = 512
Q_BLOCK = 128
SEL_Q_BLOCK = 32
FORCE_SCORE = 1e9
NEG_INF = -1e30
PEER_HEADS = 8
N_KEYS = 128
PEER_KEY_DIM = 256
PEER_TOPK = 16
PEER_TOKEN_BLOCK = 128
NORM_EPS = 1e-6
KV_W = NSA_KV_HEADS * NSA_HEAD_DIM
N_GATES = 3 * NSA_Q_HEADS
IN_SIZES = [HG_WIDTH] * 4 + [NSA_WIDTH] + [KV_W] * 6 + [N_GATES]
IN_COLS = sum(IN_SIZES)

VMEM_LIMIT_BYTES = 48 * 1024 * 1024


def _norm_matmul_body(x_ref, g_ref, w_ref, o_ref, xn_ref):
    @pl.when(pl.program_id(1) == 0)
    def _():
        x = x_ref[...]
        r = lax.rsqrt(jnp.mean(x * x, axis=-1, keepdims=True) + NORM_EPS)
        xn_ref[...] = (x * r * g_ref[...]).astype(jnp.bfloat16)

    o_ref[...] = jnp.dot(xn_ref[...], w_ref[...], preferred_element_type=jnp.float32).astype(o_ref.dtype)


def _norm_matmul(x, gain, w, tm, tn, out_dtype=jnp.float32):
    m, k = x.shape
    n = w.shape[1]
    return pl.pallas_call(
        _norm_matmul_body,
        grid=(m // tm, n // tn),
        in_specs=[
            pl.BlockSpec((tm, k), lambda i, j: (i, 0)),
            pl.BlockSpec((1, k), lambda i, j: (0, 0)),
            pl.BlockSpec((k, tn), lambda i, j: (0, j)),
        ],
        out_specs=pl.BlockSpec((tm, tn), lambda i, j: (i, j)),
        out_shape=jax.ShapeDtypeStruct((m, n), out_dtype),
        scratch_shapes=[pltpu.VMEM((tm, k), jnp.bfloat16)],
        compiler_params=pltpu.CompilerParams(
            dimension_semantics=("arbitrary", "arbitrary"), vmem_limit_bytes=VMEM_LIMIT_BYTES),
        name="norm_matmul",
    )(x, gain.reshape(1, k), w)


def _matmul_res_body(a_ref, w_ref, r_ref, o_ref):
    o_ref[...] = r_ref[...] + jnp.dot(a_ref[...].astype(jnp.bfloat16), w_ref[...],
                                      preferred_element_type=jnp.float32)


def _matmul_res(a, w, res, tm, tn):
    m, k = a.shape
    n = w.shape[1]
    return pl.pallas_call(
        _matmul_res_body,
        grid=(m // tm, n // tn),
        in_specs=[
            pl.BlockSpec((tm, k), lambda i, j: (i, 0)),
            pl.BlockSpec((k, tn), lambda i, j: (0, j)),
            pl.BlockSpec((tm, tn), lambda i, j: (i, j)),
        ],
        out_specs=pl.BlockSpec((tm, tn), lambda i, j: (i, j)),
        out_shape=jax.ShapeDtypeStruct((m, n), jnp.float32),
        compiler_params=pltpu.CompilerParams(
            dimension_semantics=("arbitrary", "arbitrary"), vmem_limit_bytes=VMEM_LIMIT_BYTES),
        name="matmul_res",
    )(a, w, res)


def _sel_attn_body(q_ref, k_ref, v_ref, sel_ref, slope_ref, o_ref, m_ref, l_ref, acc_ref, *, tq, tk):
    i = pl.program_id(2)
    j = pl.program_id(3)
    G, dh = NSA_GROUP, NSA_HEAD_DIM

    @pl.when(j == 0)
    def _():
        m_ref[...] = jnp.full(m_ref.shape, NEG_INF, jnp.float32)
        l_ref[...] = jnp.zeros(l_ref.shape, jnp.float32)
        acc_ref[...] = jnp.zeros(acc_ref.shape, jnp.float32)

    @pl.when(j * tk <= i * tq + tq - 1)
    def _():
        q = q_ref[0, 0].reshape(G * tq, dh)
        s = lax.dot_general(q, k_ref[0, 0], (((1,), (1,)), ((), ())),
                            preferred_element_type=jnp.float32).reshape(G, tq, tk)
        dist = (i * tq - j * tk) + lax.broadcasted_iota(jnp.int32, (tq, tk), 0) \
            - lax.broadcasted_iota(jnp.int32, (tq, tk), 1)
        n_sel = sel_ref.shape[-1]
        blk = lax.broadcasted_iota(jnp.int32, (n_sel, tk), 0)
        col_blk = j * (tk // SEL_BLOCK) + lax.broadcasted_iota(jnp.int32, (n_sel, tk), 1) // SEL_BLOCK
        expand = (blk == col_blk).astype(jnp.bfloat16)
        picked = jnp.dot(sel_ref[0, 0], expand, preferred_element_type=jnp.float32)
        valid = ((dist >= 0) & (picked > 0.5))[None]
        s = s - slope_ref[0] * dist.astype(jnp.float32)[None]
        s = jnp.where(valid, s, NEG_INF)
        m_old = m_ref[...].reshape(G, tq, 1)
        m_new = jnp.maximum(m_old, jnp.max(s, axis=-1, keepdims=True))
        p = jnp.where(valid, jnp.exp(s - m_new), 0.0)
        alpha = jnp.exp(m_old - m_new)
        l_ref[...] = (alpha * l_ref[...].reshape(G, tq, 1) + jnp.sum(p, axis=-1, keepdims=True)).reshape(G * tq, 1)
        pv = jnp.dot(p.reshape(G * tq, tk).astype(jnp.bfloat16), v_ref[0, 0], preferred_element_type=jnp.float32)
        acc_ref[...] = alpha.reshape(G * tq, 1) * acc_ref[...] + pv
        m_ref[...] = m_new.reshape(G * tq, 1)

    @pl.when(j == pl.num_programs(3) - 1)
    def _():
        o_ref[0, 0] = (acc_ref[...] / l_ref[...]).reshape(G, tq, dh)


def _sel_attn(q, k, v, sel, slopes, tq=256, tk=512):
    B, Hkv, G, T, dh = q.shape
    n_sel = sel.shape[-1]

    def kv_map(b, g, i, j):
        return (b, g, jnp.minimum(j, (i * tq + tq - 1) // tk), 0)

    return pl.pallas_call(
        functools.partial(_sel_attn_body, tq=tq, tk=tk),
        grid=(B, Hkv, T // tq, T // tk),
        in_specs=[
            pl.BlockSpec((1, 1, G, tq, dh), lambda b, g, i, j: (b, g, 0, i, 0)),
            pl.BlockSpec((1, 1, tk, dh), kv_map),
            pl.BlockSpec((1, 1, tk, dh), kv_map),
            pl.BlockSpec((1, 1, tq, n_sel), lambda b, g, i, j: (b, g, i, 0)),
            pl.BlockSpec((1, G, 1, 1), lambda b, g, i, j: (g, 0, 0, 0)),
        ],
        out_specs=pl.BlockSpec((1, 1, G, tq, dh), lambda b, g, i, j: (b, g, 0, i, 0)),
        out_shape=jax.ShapeDtypeStruct((B, Hkv, G, T, dh), jnp.float32),
        scratch_shapes=[pltpu.VMEM((G * tq, 1), jnp.float32), pltpu.VMEM((G * tq, 1), jnp.float32),
                        pltpu.VMEM((G * tq, dh), jnp.float32)],
        compiler_params=pltpu.CompilerParams(
            dimension_semantics=("arbitrary",) * 4, vmem_limit_bytes=VMEM_LIMIT_BYTES),
        name="sel_attn",
    )(q, k, v, sel, slopes)


def _rms_norm(x, w):
    return x * lax.rsqrt(jnp.mean(x * x, axis=-1, keepdims=True) + NORM_EPS) * w


def _masked_softmax(s, mask):
    p = jax.nn.softmax(jnp.where(mask, s, NEG_INF), axis=-1)
    return jnp.where(mask, p, 0.0)


def _alibi_slopes(n):
    return jnp.asarray(2.0 ** (-8.0 * np.arange(1, n + 1) / n), dtype=jnp.float32)


def _hgrn2(q, f_logit, i_in, g, lb, norm_w):
    B, T, _ = q.shape
    H, dk, C = HG_HEADS, HG_HEAD_DIM, HG_CHUNK
    nc = T // C
    f = lb + (1.0 - lb) * jax.nn.sigmoid(f_logit)

    def chunks(a):
        return a.reshape(B, nc, C, H, dk).transpose(0, 3, 1, 2, 4)

    qc, kc, vc, lfc = chunks(q), chunks(1.0 - f), chunks(i_in), chunks(jnp.log(f))
    b = jnp.cumsum(lfc, axis=3)
    b_end = b[:, :, :, -1:, :]
    q_dec = qc * jnp.exp(b)
    causal = np.tril(np.ones((C, C), dtype=bool))
    attn = jnp.where(causal, jnp.einsum('bhntc,bhnsc->bhnts', q_dec, kc * jnp.exp(-b)), 0.0)
    o_intra = jnp.einsum('bhnts,bhnsv->bhntv', attn, vc)
    upd = jnp.einsum('bhnsc,bhnsv->bhncv', kc * jnp.exp(b_end - b), vc)
    dec = jnp.exp(b_end[:, :, :, 0, :])

    def step(state, inp):
        d, u = inp
        return d[..., None] * state + u, state

    s0 = jnp.zeros((B, H, dk, dk), jnp.float32)
    _, s_prev = lax.scan(step, s0, (jnp.moveaxis(dec, 2, 0), jnp.moveaxis(upd, 2, 0)))
    o_inter = jnp.einsum('bhntc,nbhcv->bhntv', q_dec, s_prev)
    o = (o_intra + o_inter).transpose(0, 2, 3, 1, 4).reshape(B, T, H, dk)
    o = _rms_norm(o, norm_w) * jax.nn.silu(g.reshape(B, T, H, dk))
    return o.reshape(B, T, H * dk)


def _nsa(q, kc, vc, ks, vs, kw, vw, gate_logits, q_norm_w, kc_norm_w, ks_norm_w, kw_norm_w,
         pos_k, pos_v, w_ck1, w_ck2, w_cv1, w_cv2):
    B, T, _ = q.shape
    Hkv, G, dh = NSA_KV_HEADS, NSA_GROUP, NSA_HEAD_DIM
    f32 = jnp.float32
    slopes = _alibi_slopes(NSA_Q_HEADS).reshape(Hkv, G)
    qn = _rms_norm(q.reshape(B, T, Hkv, G, dh), q_norm_w).transpose(0, 2, 3, 1, 4) * (dh ** -0.5)

    def kv_heads(a):
        return a.reshape(B, T, Hkv, dh).transpose(0, 2, 1, 3)

    t_pos = np.arange(T)
    n_cmp = (T - CMP_BLOCK) // CMP_STRIDE + 1
    cmp_start = np.arange(n_cmp) * CMP_STRIDE
    blk_idx = cmp_start[:, None] + np.arange(CMP_BLOCK)[None, :]
    cmp_end = cmp_start + CMP_BLOCK - 1

    def compress(a, pos, w1, w2):
        blocks = kv_heads(a)[:, :, blk_idx] + pos
        flat = blocks.reshape(B, Hkv, n_cmp, CMP_BLOCK * dh)
        return jax.nn.gelu(flat @ w1) @ w2

    k_cmp = _rms_norm(compress(kc, pos_k, w_ck1, w_ck2), kc_norm_w)
    v_cmp = compress(vc, pos_v, w_cv1, w_cv2)
    dist_c = t_pos[:, None] - cmp_end[None, :]
    s_c = jnp.einsum('bgrtd,bgnd->bgrtn', qn, k_cmp) - slopes[:, :, None, None] * dist_c.astype(np.float32)
    p_cmp = _masked_softmax(s_c, dist_c >= 0)
    o_cmp = jnp.einsum('bgrtn,bgnd->bgrtd', p_cmp, v_cmp)

    n_sel = T // SEL_BLOCK
    sel_start = np.arange(n_sel) * SEL_BLOCK
    overlap = ((cmp_start[:, None] < sel_start[None, :] + SEL_BLOCK)
               & (cmp_start[:, None] + CMP_BLOCK > sel_start[None, :])).astype(np.float32)
    imp = jnp.einsum('bgrtn,nj->bgtj', p_cmp, overlap)
    cur = t_pos // SEL_BLOCK
    j = np.arange(n_sel)
    forced = (j[None, :] == 0) | ((j[None, :] <= cur[:, None]) & (j[None, :] > cur[:, None] - N_LOCAL))
    future = j[None, :] > cur[:, None]
    imp = jnp.where(forced, FORCE_SCORE, jnp.where(future, -FORCE_SCORE, imp))
    n_top = min(N_SELECT, n_sel)
    _, sel_idx = lax.top_k(imp, n_top)

    k_sel = kv_heads(_rms_norm(ks.reshape(B, T, Hkv, dh), ks_norm_w).reshape(B, T, KV_W))
    v_sel = kv_heads(vs)
    sel_mask = (sel_idx[..., None] == jnp.arange(n_sel)).any(axis=-2).astype(jnp.bfloat16)
    o_sel = _sel_attn(qn.astype(jnp.bfloat16), k_sel.astype(jnp.bfloat16), v_sel.astype(jnp.bfloat16),
                      sel_mask, slopes.reshape(Hkv, G, 1, 1))

    pad = ((0, 0), (0, 0), (WINDOW, 0), (0, 0))
    k_win = jnp.pad(kv_heads(_rms_norm(kw.reshape(B, T, Hkv, dh), kw_norm_w).reshape(B, T, KV_W)), pad)
    v_win = jnp.pad(kv_heads(vw), pad)
    span = Q_BLOCK + WINDOW
    dist_w = np.arange(Q_BLOCK)[:, None] + WINDOW - np.arange(span)[None, :]

    def win_block(args):
        q_b, start = args
        k_b = lax.dynamic_slice_in_dim(k_win, start, span, axis=2)
        v_b = lax.dynamic_slice_in_dim(v_win, start, span, axis=2)
        key_pos = start - WINDOW + jnp.arange(span)
        mask = (dist_w >= 0) & (dist_w < WINDOW) & (key_pos >= 0)[None, :]
        s = jnp.einsum('bgrqd,bgkd->bgrqk', q_b, k_b) - slopes[:, :, None, None] * dist_w.astype(np.float32)
        p = _masked_softmax(s, mask)
        return jnp.einsum('bgrqk,bgkd->bgrqd', p, v_b)

    n_wb = T // Q_BLOCK
    q_wblocks = qn.reshape(B, Hkv, G, n_wb, Q_BLOCK, dh).transpose(3, 0, 1, 2, 4, 5)
    w_starts = jnp.arange(n_wb, dtype=jnp.int32) * Q_BLOCK
    o_win = lax.map(win_block, (q_wblocks, w_starts))
    o_win = o_win.transpose(1, 2, 3, 0, 4, 5).reshape(B, Hkv, G, T, dh)

    gates = jax.nn.sigmoid(gate_logits).reshape(B, T, Hkv, G, 3).transpose(0, 2, 3, 1, 4)
    o = gates[..., 0:1] * o_cmp + gates[..., 1:2] * o_sel + gates[..., 2:3] * o_win
    return o.transpose(0, 3, 1, 2, 4).reshape(B, T, NSA_WIDTH)


def _peer(xt, q, sub_keys, u_tab, v_tab):
    n, D = xt.shape
    H, K = PEER_HEADS, PEER_TOPK
    q = q.reshape(n, H, 2, PEER_KEY_DIM // 2)
    s = jnp.einsum('nhpd,hpkd->nhpk', q, sub_keys)
    s_top, i_top = lax.top_k(s, K)
    cand_s = (s_top[:, :, 0, :, None] + s_top[:, :, 1, None, :]).reshape(n, H, K * K)
    cand_i = (i_top[:, :, 0, :, None] * N_KEYS + i_top[:, :, 1, None, :]).reshape(n, H, K * K)
    best_s, best_pos = lax.top_k(cand_s, K)
    expert = jnp.take_along_axis(cand_i, best_pos, axis=-1)
    gate = jax.nn.softmax(best_s, axis=-1)
    nb = n // PEER_TOKEN_BLOCK

    def block(args):
        xb, eb, gb = args
        act = jax.nn.gelu(jnp.einsum('td,thkd->thk', xb, u_tab[eb]))
        return jnp.einsum('thk,thkd->td', gb * act, v_tab[eb])

    out = lax.map(block, (xt.reshape(nb, PEER_TOKEN_BLOCK, D),
                          expert.reshape(nb, PEER_TOKEN_BLOCK, H, K),
                          gate.reshape(nb, PEER_TOKEN_BLOCK, H, K)))
    return out.reshape(n, D)


def kernel(x, norm1_w, w_in, hg_lb_logits, hg_norm_w, q_norm_w, kc_norm_w, ks_norm_w, kw_norm_w,
           cmp_pos_k, cmp_pos_v, w_ck1, w_ck2, w_cv1, w_cv2, w_out, norm2_w,
           peer_w_q, peer_sub_keys, peer_u, peer_v):
    B, T, D = x.shape
    n = B * T
    layer = 0
    split_at = np.cumsum(IN_SIZES)[:-1].tolist()
    lower_bounds = jnp.cumsum(jax.nn.softmax(hg_lb_logits, axis=0), axis=0)
    xt = x.reshape(n, D)

    n_pad = 7168
    w_in_b = jnp.pad(w_in[layer].astype(jnp.bfloat16), ((0, 0), (0, n_pad - IN_COLS)))
    proj = _norm_matmul(xt, norm1_w[layer], w_in_b, 512, 512)[:, :IN_COLS].reshape(B, T, IN_COLS)
    (hq, hf, hi, hg, nq, nkc, nvc, nks, nvs, nkw, nvw, ngate) = jnp.split(proj, split_at, axis=-1)
    hg_out = _hgrn2(hq, hf, hi, hg, lower_bounds[layer], hg_norm_w[layer])
    nsa_out = _nsa(nq, nkc, nvc, nks, nvs, nkw, nvw, ngate,
                   q_norm_w[layer], kc_norm_w[layer], ks_norm_w[layer], kw_norm_w[layer],
                   cmp_pos_k[layer], cmp_pos_v[layer], w_ck1[layer], w_ck2[layer],
                   w_cv1[layer], w_cv2[layer])
    mix = jnp.concatenate([hg_out, nsa_out], axis=-1).reshape(n, D)
    h = _matmul_res(mix, w_out[layer].astype(jnp.bfloat16), xt, 512, 512)

    hn = _rms_norm(h, norm2_w[layer])
    pq = _norm_matmul(h, norm2_w[layer], peer_w_q[layer].astype(jnp.bfloat16), 512, 512)
    y = h + _peer(hn, pq, peer_sub_keys[layer], peer_u[layer], peer_v[layer])
    return y.reshape(B, T, D)
```

```python
import functools

import jax
import jax.numpy as jnp
import numpy as np
from jax import lax
from jax.experimental import pallas as pl
from jax.experimental.pallas import tpu as pltpu

D_MODEL = 2048
HG_WIDTH = 1024
HG_HEAD_DIM = 128
HG_HEADS = 8
HG_CHUNK = 64
NSA_WIDTH = 1024
NSA_HEAD_DIM = 64
NSA_Q_HEADS = 16
NSA_KV_HEADS = 4
NSA_GROUP = 4
CMP_BLOCK = 32
CMP_STRIDE = 16
SEL_BLOCK = 64
N_SELECT = 16
N_LOCAL = 2
WINDOW = 512
Q_BLOCK = 128
SEL_Q_BLOCK = 32
FORCE_SCORE = 1e9
NEG_INF = -1e30
PEER_HEADS = 8
N_KEYS = 128
PEER_KEY_DIM = 256
PEER_TOPK = 16
PEER_TOKEN_BLOCK = 128
NORM_EPS = 1e-6
KV_W = NSA_KV_HEADS * NSA_HEAD_DIM
N_GATES = 3 * NSA_Q_HEADS
IN_SIZES = [HG_WIDTH] * 4 + [NSA_WIDTH] + [KV_W] * 6 + [N_GATES]
IN_COLS = sum(IN_SIZES)

VMEM_LIMIT_BYTES = 48 * 1024 * 1024


def _norm_matmul_body(x_ref, g_ref, w_ref, o_ref, xn_ref):
    @pl.when(pl.program_id(1) == 0)
    def _():
        x = x_ref[...]
        r = lax.rsqrt(jnp.mean(x * x, axis=-1, keepdims=True) + NORM_EPS)
        xn_ref[...] = (x * r * g_ref[...]).astype(jnp.bfloat16)

    o_ref[...] = jnp.dot(xn_ref[...], w_ref[...], preferred_element_type=jnp.float32).astype(o_ref.dtype)


def _norm_matmul(x, gain, w, tm, tn, out_dtype=jnp.float32):
    m, k = x.shape
    n = w.shape[1]
    return pl.pallas_call(
        _norm_matmul_body,
        grid=(m // tm, n // tn),
        in_specs=[
            pl.BlockSpec((tm, k), lambda i, j: (i, 0)),
            pl.BlockSpec((1, k), lambda i, j: (0, 0)),
            pl.BlockSpec((k, tn), lambda i, j: (0, j)),
        ],
        out_specs=pl.BlockSpec((tm, tn), lambda i, j: (i, j)),
        out_shape=jax.ShapeDtypeStruct((m, n), out_dtype),
        scratch_shapes=[pltpu.VMEM((tm, k), jnp.bfloat16)],
        compiler_params=pltpu.CompilerParams(
            dimension_semantics=("arbitrary", "arbitrary"), vmem_limit_bytes=VMEM_LIMIT_BYTES),
        name="norm_matmul",
    )(x, gain.reshape(1, k), w)


def _matmul_res_body(a_ref, w_ref, r_ref, o_ref):
    o_ref[...] = r_ref[...] + jnp.dot(a_ref[...].astype(jnp.bfloat16), w_ref[...],
                                      preferred_element_type=jnp.float32)


def _matmul_res(a, w, res, tm, tn):
    m, k = a.shape
    n = w.shape[1]
    return pl.pallas_call(
        _matmul_res_body,
        grid=(m // tm, n // tn),
        in_specs=[
            pl.BlockSpec((tm, k), lambda i, j: (i, 0)),
            pl.BlockSpec((k, tn), lambda i, j: (0, j)),
            pl.BlockSpec((tm, tn), lambda i, j: (i, j)),
        ],
        out_specs=pl.BlockSpec((tm, tn), lambda i, j: (i, j)),
        out_shape=jax.ShapeDtypeStruct((m, n), jnp.float32),
        compiler_params=pltpu.CompilerParams(
            dimension_semantics=("arbitrary", "arbitrary"), vmem_limit_bytes=VMEM_LIMIT_BYTES),
        name="matmul_res",
    )(a, w, res)


def _sel_attn_body(q_ref, k_ref, v_ref, sel_ref, slope_ref, o_ref, m_ref, l_ref, acc_ref, *, tq, tk):
    i = pl.program_id(2)
    j = pl.program_id(3)
    G, dh = NSA_GROUP, NSA_HEAD_DIM

    @pl.when(j == 0)
    def _():
        m_ref[...] = jnp.full(m_ref.shape, NEG_INF, jnp.float32)
        l_ref[...] = jnp.zeros(l_ref.shape, jnp.float32)
        acc_ref[...] = jnp.zeros(acc_ref.shape, jnp.float32)

    @pl.when(j * tk <= i * tq + tq - 1)
    def _():
        q = q_ref[0, 0].reshape(G * tq, dh)
        s = lax.dot_general(q, k_ref[0, 0], (((1,), (1,)), ((), ())),
                            preferred_element_type=jnp.float32).reshape(G, tq, tk)
        dist = (i * tq - j * tk) + lax.broadcasted_iota(jnp.int32, (tq, tk), 0) \
            - lax.broadcasted_iota(jnp.int32, (tq, tk), 1)
        n_sel = sel_ref.shape[-1]
        blk = lax.broadcasted_iota(jnp.int32, (n_sel, tk), 0)
        col_blk = j * (tk // SEL_BLOCK) + lax.broadcasted_iota(jnp.int32, (n_sel, tk), 1) // SEL_BLOCK
        expand = (blk == col_blk).astype(jnp.bfloat16)
        picked = jnp.dot(sel_ref[0, 0], expand, preferred_element_type=jnp.float32)
        valid = ((dist >= 0) & (picked > 0.5))[None]
        s = s - slope_ref[0] * dist.astype(jnp.float32)[None]
        s = jnp.where(valid, s, NEG_INF)
        m_old = m_ref[...].reshape(G, tq, 1)
        m_new = jnp.maximum(m_old, jnp.max(s, axis=-1, keepdims=True))
        p = jnp.where(valid, jnp.exp(s - m_new), 0.0)
        alpha = jnp.exp(m_old - m_new)
        l_ref[...] = (alpha * l_ref[...].reshape(G, tq, 1) + jnp.sum(p, axis=-1, keepdims=True)).reshape(G * tq, 1)
        pv = jnp.dot(p.reshape(G * tq, tk).astype(jnp.bfloat16), v_ref[0, 0], preferred_element_type=jnp.float32)
        acc_ref[...] = alpha.reshape(G * tq, 1) * acc_ref[...] + pv
        m_ref[...] = m_new.reshape(G * tq, 1)

    @pl.when(j == pl.num_programs(3) - 1)
    def _():
        o_ref[0, 0] = (acc_ref[...] / l_ref[...]).reshape(G, tq, dh)


def _sel_attn(q, k, v, sel, slopes, tq=256, tk=512):
    B, Hkv, G, T, dh = q.shape
    n_sel = sel.shape[-1]

    def kv_map(b, g, i, j):
        return (b, g, jnp.minimum(j, (i * tq + tq - 1) // tk), 0)

    return pl.pallas_call(
        functools.partial(_sel_attn_body, tq=tq, tk=tk),
        grid=(B, Hkv, T // tq, T // tk),
        in_specs=[
            pl.BlockSpec((1, 1, G, tq, dh), lambda b, g, i, j: (b, g, 0, i, 0)),
            pl.BlockSpec((1, 1, tk, dh), kv_map),
            pl.BlockSpec((1, 1, tk, dh), kv_map),
            pl.BlockSpec((1, 1, tq, n_sel), lambda b, g, i, j: (b, g, i, 0)),
            pl.BlockSpec((1, G, 1, 1), lambda b, g, i, j: (g, 0, 0, 0)),
        ],
        out_specs=pl.BlockSpec((1, 1, G, tq, dh), lambda b, g, i, j: (b, g, 0, i, 0)),
        out_shape=jax.ShapeDtypeStruct((B, Hkv, G, T, dh), jnp.float32),
        scratch_shapes=[pltpu.VMEM((G * tq, 1), jnp.float32), pltpu.VMEM((G * tq, 1), jnp.float32),
                        pltpu.VMEM((G * tq, dh), jnp.float32)],
        compiler_params=pltpu.CompilerParams(
            dimension_semantics=("arbitrary",) * 4, vmem_limit_bytes=VMEM_LIMIT_BYTES),
        name="sel_attn",
    )(q, k, v, sel, slopes)


def _rms_norm(x, w):
    return x * lax.rsqrt(jnp.mean(x * x, axis=-1, keepdims=True) + NORM_EPS) * w


def _masked_softmax(s, mask):
    p = jax.nn.softmax(jnp.where(mask, s, NEG_INF), axis=-1)
    return jnp.where(mask, p, 0.0)


def _alibi_slopes(n):
    return jnp.asarray(2.0 ** (-8.0 * np.arange(1, n + 1) / n), dtype=jnp.float32)


def _hgrn2(q, f_logit, i_in, g, lb, norm_w):
    B, T, _ = q.shape
    H, dk, C = HG_HEADS, HG_HEAD_DIM, HG_CHUNK
    nc = T // C
    f = lb + (1.0 - lb) * jax.nn.sigmoid(f_logit)

    def chunks(a):
        return a.reshape(B, nc, C, H, dk).transpose(0, 3, 1, 2, 4)

    qc, kc, vc, lfc = chunks(q), chunks(1.0 - f), chunks(i_in), chunks(jnp.log(f))
    b = jnp.cumsum(lfc, axis=3)
    b_end = b[:, :, :, -1:, :]
    q_dec = qc * jnp.exp(b)
    causal = np.tril(np.ones((C, C), dtype=bool))
    attn = jnp.where(causal, jnp.einsum('bhntc,bhnsc->bhnts', q_dec, kc * jnp.exp(-b)), 0.0)
    o_intra = jnp.einsum('bhnts,bhnsv->bhntv', attn, vc)
    upd = jnp.einsum('bhnsc,bhnsv->bhncv', kc * jnp.exp(b_end - b), vc)
    dec = jnp.exp(b_end[:, :, :, 0, :])

    def step(state, inp):
        d, u = inp
        return d[..., None] * state + u, state

    s0 = jnp.zeros((B, H, dk, dk), jnp.float32)
    _, s_prev = lax.scan(step, s0, (jnp.moveaxis(dec, 2, 0), jnp.moveaxis(upd, 2, 0)))
    o_inter = jnp.einsum('bhntc,nbhcv->bhntv', q_dec, s_prev)
    o = (o_intra + o_inter).transpose(0, 2, 3, 1, 4).reshape(B, T, H, dk)
    o = _rms_norm(o, norm_w) * jax.nn.silu(g.reshape(B, T, H, dk))
    return o.reshape(B, T, H * dk)


def _nsa(q, kc, vc, ks, vs, kw, vw, gate_logits, q_norm_w, kc_norm_w, ks_norm_w, kw_norm_w,
         pos_k, pos_v, w_ck1, w_ck2, w_cv1, w_cv2):
    B, T, _ = q.shape
    Hkv, G, dh = NSA_KV_HEADS, NSA_GROUP, NSA_HEAD_DIM
    f32 = jnp.float32
    slopes = _alibi_slopes(NSA_Q_HEADS).reshape(Hkv, G)
    qn = _rms_norm(q.reshape(B, T, Hkv, G, dh), q_norm_w).transpose(0, 2, 3, 1, 4) * (dh ** -0.5)

    def kv_heads(a):
        return a.reshape(B, T, Hkv, dh).transpose(0, 2, 1, 3)

    t_pos = np.arange(T)
    n_cmp = (T - CMP_BLOCK) // CMP_STRIDE + 1
    cmp_start = np.arange(n_cmp) * CMP_STRIDE
    blk_idx = cmp_start[:, None] + np.arange(CMP_BLOCK)[None, :]
    cmp_end = cmp_start + CMP_BLOCK - 1

    def compress(a, pos, w1, w2):
        blocks = kv_heads(a)[:, :, blk_idx] + pos
        flat = blocks.reshape(B, Hkv, n_cmp, CMP_BLOCK * dh)
        return jax.nn.gelu(flat @ w1) @ w2

    k_cmp = _rms_norm(compress(kc, pos_k, w_ck1, w_ck2), kc_norm_w)
    v_cmp = compress(vc, pos_v, w_cv1, w_cv2)
    dist_c = t_pos[:, None] - cmp_end[None, :]
    s_c = jnp.einsum('bgrtd,bgnd->bgrtn', qn, k_cmp) - slopes[:, :, None, None] * dist_c.astype(np.float32)
    p_cmp = _masked_softmax(s_c, dist_c >= 0)
    o_cmp = jnp.einsum('bgrtn,bgnd->bgrtd', p_cmp, v_cmp)

    n_sel = T // SEL_BLOCK
    sel_start = np.arange(n_sel) * SEL_BLOCK
    overlap = ((cmp_start[:, None] < sel_start[None, :] + SEL_BLOCK)
               & (cmp_start[:, None] + CMP_BLOCK > sel_start[None, :])).astype(np.float32)
    imp = jnp.einsum('bgrtn,nj->bgtj', p_cmp, overlap)
    cur = t_pos // SEL_BLOCK
    j = np.arange(n_sel)
    forced = (j[None, :] == 0) | ((j[None, :] <= cur[:, None]) & (j[None, :] > cur[:, None] - N_LOCAL))
    future = j[None, :] > cur[:, None]
    imp = jnp.where(forced, FORCE_SCORE, jnp.where(future, -FORCE_SCORE, imp))
    n_top = min(N_SELECT, n_sel)
    _, sel_idx = lax.top_k(imp, n_top)

    k_sel = kv_heads(_rms_norm(ks.reshape(B, T, Hkv, dh), ks_norm_w).reshape(B, T, KV_W))
    v_sel = kv_heads(vs)
    sel_mask = (sel_idx[..., None] == jnp.arange(n_sel)).any(axis=-2).astype(jnp.bfloat16)
    o_sel = _sel_attn(qn.astype(jnp.bfloat16), k_sel.astype(jnp.bfloat16), v_sel.astype(jnp.bfloat16),
                      sel_mask, slopes.reshape(Hkv, G, 1, 1))

    pad = ((0, 0), (0, 0), (WINDOW, 0), (0, 0))
    k_win = jnp.pad(kv_heads(_rms_norm(kw.reshape(B, T, Hkv, dh), kw_norm_w).reshape(B, T, KV_W)), pad)
    v_win = jnp.pad(kv_heads(vw), pad)
    span = Q_BLOCK + WINDOW
    dist_w = np.arange(Q_BLOCK)[:, None] + WINDOW - np.arange(span)[None, :]

    def win_block(args):
        q_b, start = args
        k_b = lax.dynamic_slice_in_dim(k_win, start, span, axis=2)
        v_b = lax.dynamic_slice_in_dim(v_win, start, span, axis=2)
        key_pos = start - WINDOW + jnp.arange(span)
        mask = (dist_w >= 0) & (dist_w < WINDOW) & (key_pos >= 0)[None, :]
        s = jnp.einsum('bgrqd,bgkd->bgrqk', q_b, k_b) - slopes[:, :, None, None] * dist_w.astype(np.float32)
        p = _masked_softmax(s, mask)
        return jnp.einsum('bgrqk,bgkd->bgrqd', p, v_b)

    n_wb = T // Q_BLOCK
    q_wblocks = qn.reshape(B, Hkv, G, n_wb, Q_BLOCK, dh).transpose(3, 0, 1, 2, 4, 5)
    w_starts = jnp.arange(n_wb, dtype=jnp.int32) * Q_BLOCK
    o_win = lax.map(win_block, (q_wblocks, w_starts))
    o_win = o_win.transpose(1, 2, 3, 0, 4, 5).reshape(B, Hkv, G, T, dh)

    gates = jax.nn.sigmoid(gate_logits).reshape(B, T, Hkv, G, 3).transpose(0, 2, 3, 1, 4)
    o = gates[..., 0:1] * o_cmp + gates[..., 1:2] * o_sel + gates[..., 2:3] * o_win
    return o.transpose(0, 3, 1, 2, 4).reshape(B, T, NSA_WIDTH)


BIG_NEG = -3.0e38
PEER_CAND = [(a, b) for a in range(PEER_TOPK) for b in range(PEER_TOPK) if (a + 1) * (b + 1) <= PEER_TOPK]
PEER_CAND_ROWS = -(-len(PEER_CAND) // 8) * 8


def _top_rows(s, k):
    rows = s.shape[0]
    ridx = lax.broadcasted_iota(jnp.int32, s.shape, 0)
    out = []
    cur = s
    for _ in range(k):
        mk = jnp.max(cur, axis=0, keepdims=True)
        first = jnp.min(jnp.where(cur == mk, ridx, rows), axis=0, keepdims=True)
        out.append(mk)
        cur = jnp.where(ridx == first, BIG_NEG, cur)
    return out


def _peer_route_body(h_ref, g_ref, wq_ref, keys_ref, xt_ref, s2_ref, a2_ref, s1_ref, r_ref, tau_ref, qt_ref, cand_ref):
    x = h_ref[...]
    hn = x * lax.rsqrt(jnp.mean(x * x, axis=-1, keepdims=True) + NORM_EPS) * g_ref[...]
    hnt = hn.T.astype(jnp.bfloat16)
    xt_ref[...] = hnt
    qt_ref[...] = jnp.dot(wq_ref[...], hnt, preferred_element_type=jnp.float32).astype(jnp.bfloat16)
    cand_ref[...] = jnp.full(cand_ref.shape, BIG_NEG, jnp.float32)
    kd = PEER_KEY_DIM // 2

    def head(hd, carry):
        q1 = qt_ref[pl.ds(pl.multiple_of(hd * 2 * kd, kd), kd), :]
        q2 = qt_ref[pl.ds(pl.multiple_of(hd * 2 * kd + kd, kd), kd), :]
        s1 = jnp.dot(keys_ref[2 * hd], q1, preferred_element_type=jnp.float32)
        s2 = jnp.dot(keys_ref[2 * hd + 1], q2, preferred_element_type=jnp.float32)
        t1 = _top_rows(s1, PEER_TOPK)
        t2 = _top_rows(s2, PEER_TOPK)
        cmax = t1[0] + t2[0]
        cands = [t1[a] + t2[b] for a, b in PEER_CAND]
        for i, c in enumerate(cands):
            cand_ref[i:i + 1, :] = c
        call = cand_ref[...]
        n_gt = jnp.zeros(call.shape, jnp.float32)
        for c in cands:
            n_gt = n_gt + jnp.where(c > call, 1.0, 0.0)
        tau = jnp.min(jnp.where(n_gt <= PEER_TOPK - 1, call, -BIG_NEG), axis=0, keepdims=True)
        z = jnp.sum(jnp.where(call >= tau, jnp.exp(call - cmax), 0.0), axis=0, keepdims=True)
        s2_ref[hd] = s2
        a2_ref[hd] = jnp.exp(s2 - t2[0])
        s1_ref[hd] = s1
        tau_ref[hd] = tau
        r_ref[hd] = jnp.exp(s1 - t1[0]) / z
        return carry

    lax.fori_loop(0, PEER_HEADS, head, 0)


def _peer_route(h, gain, wq_t, keys, tt):
    n, d = h.shape
    hp, nk, kd = keys.shape
    stat = jax.ShapeDtypeStruct((PEER_HEADS, nk, n), jnp.float32)
    stat_spec = pl.BlockSpec((PEER_HEADS, nk, tt), lambda i: (0, 0, i))
    return pl.pallas_call(
        _peer_route_body,
        grid=(n // tt,),
        in_specs=[
            pl.BlockSpec((tt, d), lambda i: (i, 0)),
            pl.BlockSpec((1, d), lambda i: (0, 0)),
            pl.BlockSpec(wq_t.shape, lambda i: (0, 0)),
            pl.BlockSpec(keys.shape, lambda i: (0, 0, 0)),
        ],
        out_specs=[pl.BlockSpec((d, tt), lambda i: (0, i)), stat_spec, stat_spec, stat_spec, stat_spec,
                   pl.BlockSpec((PEER_HEADS, 1, tt), lambda i: (0, 0, i))],
        out_shape=[jax.ShapeDtypeStruct((d, n), jnp.bfloat16), stat, stat, stat, stat,
                   jax.ShapeDtypeStruct((PEER_HEADS, 1, n), jnp.float32)],
        scratch_shapes=[pltpu.VMEM((wq_t.shape[0], tt), jnp.bfloat16),
                        pltpu.VMEM((PEER_CAND_ROWS, tt), jnp.float32)],
        compiler_params=pltpu.CompilerParams(
            dimension_semantics=("arbitrary",), vmem_limit_bytes=VMEM_LIMIT_BYTES),
        name="peer_route",
    )(h, gain.reshape(1, d), wq_t, keys)


def _peer_experts_body(xt_ref, u_ref, vt_ref, s2_ref, a2_ref, s1_ref, r_ref, tau_ref, o_ref, *, n_i1):
    @pl.when(pl.program_id(1) == 0)
    def _():
        o_ref[...] = jnp.zeros(o_ref.shape, jnp.float32)

    act = jax.nn.gelu(jnp.dot(u_ref[...], xt_ref[...], preferred_element_type=jnp.float32))
    parts = []
    for i1 in range(n_i1):
        w = None
        for hd in range(PEER_HEADS):
            picked = s1_ref[hd, i1:i1 + 1, :] + s2_ref[hd] >= tau_ref[hd]
            term = jnp.where(picked, a2_ref[hd], 0.0) * r_ref[hd, i1:i1 + 1, :]
            w = term if w is None else w + term
        parts.append((w * act[i1 * N_KEYS:(i1 + 1) * N_KEYS]).astype(jnp.bfloat16))
    p = jnp.concatenate(parts, axis=0)
    o_ref[...] += jnp.dot(vt_ref[...], p, preferred_element_type=jnp.float32)


def _peer_experts(xt, u, vt, s2, a2, s1, r, tau, tt, te):
    d, n = xt.shape
    e = u.shape[0]
    n_i1 = te // N_KEYS
    full = pl.BlockSpec((PEER_HEADS, N_KEYS, tt), lambda i, j: (0, 0, i))
    part = pl.BlockSpec((PEER_HEADS, n_i1, tt), lambda i, j: (0, j, i))
    per_token = pl.BlockSpec((PEER_HEADS, 1, tt), lambda i, j: (0, 0, i))
    return pl.pallas_call(
        functools.partial(_peer_experts_body, n_i1=n_i1),
        grid=(n // tt, e // te),
        in_specs=[
            pl.BlockSpec((d, tt), lambda i, j: (0, i)),
            pl.BlockSpec((te, d), lambda i, j: (j, 0)),
            pl.BlockSpec((d, te), lambda i, j: (0, j)),
            full, full, part, part, per_token,
        ],
        out_specs=pl.BlockSpec((d, tt), lambda i, j: (0, i)),
        out_shape=jax.ShapeDtypeStruct((d, n), jnp.float32),
        compiler_params=pltpu.CompilerParams(
            dimension_semantics=("arbitrary", "arbitrary"), vmem_limit_bytes=VMEM_LIMIT_BYTES),
        name="peer_experts",
    )(xt, u, vt, s2, a2, s1, r, tau)


def _peer(h, gain, w_q, sub_keys, u_tab, v_tab):
    n, d = h.shape
    keys = sub_keys.reshape(PEER_HEADS * 2, N_KEYS, PEER_KEY_DIM // 2).astype(jnp.bfloat16)
    xt, s2, a2, s1, r, tau = _peer_route(h, gain, w_q.T.astype(jnp.bfloat16), keys, 256)
    out_t = _peer_experts(xt, u_tab.astype(jnp.bfloat16), v_tab.T.astype(jnp.bfloat16),
                          s2, a2, s1, r, tau, 512, 1024)
    return out_t.T


def kernel(x, norm1_w, w_in, hg_lb_logits, hg_norm_w, q_norm_w, kc_norm_w, ks_norm_w, kw_norm_w,
           cmp_pos_k, cmp_pos_v, w_ck1, w_ck2, w_cv1, w_cv2, w_out, norm2_w,
           peer_w_q, peer_sub_keys, peer_u, peer_v):
    B, T, D = x.shape
    n = B * T
    layer = 0
    split_at = np.cumsum(IN_SIZES)[:-1].tolist()
    lower_bounds = jnp.cumsum(jax.nn.softmax(hg_lb_logits, axis=0), axis=0)
    xt = x.reshape(n, D)

    n_pad = 7168
    w_in_b = jnp.pad(w_in[layer].astype(jnp.bfloat16), ((0, 0), (0, n_pad - IN_COLS)))
    proj = _norm_matmul(xt, norm1_w[layer], w_in_b, 512, 512)[:, :IN_COLS].reshape(B, T, IN_COLS)
    (hq, hf, hi, hg, nq, nkc, nvc, nks, nvs, nkw, nvw, ngate) = jnp.split(proj, split_at, axis=-1)
    hg_out = _hgrn2(hq, hf, hi, hg, lower_bounds[layer], hg_norm_w[layer])
    nsa_out = _nsa(nq, nkc, nvc, nks, nvs, nkw, nvw, ngate,
                   q_norm_w[layer], kc_norm_w[layer], ks_norm_w[layer], kw_norm_w[layer],
                   cmp_pos_k[layer], cmp_pos_v[layer], w_ck1[layer], w_ck2[layer],
                   w_cv1[layer], w_cv2[layer])
    mix = jnp.concatenate([hg_out, nsa_out], axis=-1).reshape(n, D)
    h = _matmul_res(mix, w_out[layer].astype(jnp.bfloat16), xt, 512, 512)

    y = h + _peer(h, norm2_w[layer], peer_w_q[layer], peer_sub_keys[layer], peer_u[layer], peer_v[layer])
    return y.reshape(B, T, D)
```

```python
import functools

import jax
import jax.numpy as jnp
import numpy as np
from jax import lax
from jax.experimental import pallas as pl
from jax.experimental.pallas import tpu as pltpu

D_MODEL = 2048
HG_WIDTH = 1024
HG_HEAD_DIM = 128
HG_HEADS = 8
HG_CHUNK = 64
NSA_WIDTH = 1024
NSA_HEAD_DIM = 64
NSA_Q_HEADS = 16
NSA_KV_HEADS = 4
NSA_GROUP = 4
CMP_BLOCK = 32
CMP_STRIDE = 16
CMP_HIDDEN = 256
SEL_BLOCK = 64
N_SELECT = 16
N_LOCAL = 2
WINDOW = 512
FORCE_SCORE = 1e9
NEG_INF = -1e30
PEER_HEADS = 8
N_KEYS = 128
PEER_KEY_DIM = 256
PEER_TOPK = 16
NORM_EPS = 1e-6
KV_W = NSA_KV_HEADS * NSA_HEAD_DIM
N_GATES = 3 * NSA_Q_HEADS
IN_SIZES = [HG_WIDTH] * 4 + [NSA_WIDTH] + [KV_W] * 6 + [N_GATES]
IN_COLS = sum(IN_SIZES)
IN_COLS_PADDED = 7168
Q_COL0 = 4 * HG_WIDTH
KV_COL0 = Q_COL0 + NSA_WIDTH
GATE_COL0 = KV_COL0 + 6 * KV_W

VMEM_LIMIT_BYTES = 48 * 1024 * 1024


def _norm_matmul_body(x_ref, g_ref, w_ref, o_ref, xn_ref):
    @pl.when(pl.program_id(1) == 0)
    def _():
        x = x_ref[...]
        r = lax.rsqrt(jnp.mean(x * x, axis=-1, keepdims=True) + NORM_EPS)
        xn_ref[...] = (x * r * g_ref[...]).astype(jnp.bfloat16)

    o_ref[...] = jnp.dot(xn_ref[...], w_ref[...], preferred_element_type=jnp.float32).astype(o_ref.dtype)


def _norm_matmul(x, gain, w, tm, tn, out_dtype=jnp.float32):
    m, k = x.shape
    n = w.shape[1]
    return pl.pallas_call(
        _norm_matmul_body,
        grid=(m // tm, n // tn),
        in_specs=[
            pl.BlockSpec((tm, k), lambda i, j: (i, 0)),
            pl.BlockSpec((1, k), lambda i, j: (0, 0)),
            pl.BlockSpec((k, tn), lambda i, j: (0, j)),
        ],
        out_specs=pl.BlockSpec((tm, tn), lambda i, j: (i, j)),
        out_shape=jax.ShapeDtypeStruct((m, n), out_dtype),
        scratch_shapes=[pltpu.VMEM((tm, k), jnp.bfloat16)],
        compiler_params=pltpu.CompilerParams(
            dimension_semantics=("arbitrary", "arbitrary"), vmem_limit_bytes=VMEM_LIMIT_BYTES),
        name="norm_matmul",
    )(x, gain.reshape(1, k), w)


def _out_proj_body(a1_ref, a2_ref, w1_ref, w2_ref, r_ref, o_ref):
    acc = jnp.dot(a1_ref[...].astype(jnp.bfloat16), w1_ref[...], preferred_element_type=jnp.float32)
    acc = acc + jnp.dot(a2_ref[...].astype(jnp.bfloat16), w2_ref[...], preferred_element_type=jnp.float32)
    o_ref[...] = r_ref[...] + acc


def _out_proj(a1, a2, w, res, tm, tn):
    m, k1 = a1.shape
    k2 = a2.shape[1]
    n = w.shape[1]
    return pl.pallas_call(
        _out_proj_body,
        grid=(m // tm, n // tn),
        in_specs=[
            pl.BlockSpec((tm, k1), lambda i, j: (i, 0)),
            pl.BlockSpec((tm, k2), lambda i, j: (i, 0)),
            pl.BlockSpec((k1, tn), lambda i, j: (0, j)),
            pl.BlockSpec((k2, tn), lambda i, j: (k1 // k2, j)),
            pl.BlockSpec((tm, tn), lambda i, j: (i, j)),
        ],
        out_specs=pl.BlockSpec((tm, tn), lambda i, j: (i, j)),
        out_shape=jax.ShapeDtypeStruct((m, n), jnp.float32),
        compiler_params=pltpu.CompilerParams(
            dimension_semantics=("arbitrary", "arbitrary"), vmem_limit_bytes=VMEM_LIMIT_BYTES),
        name="out_proj",
    )(a1, a2, w, w, res)


def _rms_norm(x, w):
    return x * lax.rsqrt(jnp.mean(x * x, axis=-1, keepdims=True) + NORM_EPS) * w


def _hgrn2(q, f_logit, i_in, g, lb, norm_w):
    B, T, _ = q.shape
    H, dk, C = HG_HEADS, HG_HEAD_DIM, HG_CHUNK
    nc = T // C
    f = lb + (1.0 - lb) * jax.nn.sigmoid(f_logit)

    def chunks(a):
        return a.reshape(B, nc, C, H, dk).transpose(0, 3, 1, 2, 4)

    qc, kc, vc, lfc = chunks(q), chunks(1.0 - f), chunks(i_in), chunks(jnp.log(f))
    b = jnp.cumsum(lfc, axis=3)
    b_end = b[:, :, :, -1:, :]
    q_dec = qc * jnp.exp(b)
    causal = np.tril(np.ones((C, C), dtype=bool))
    attn = jnp.where(causal, jnp.einsum('bhntc,bhnsc->bhnts', q_dec, kc * jnp.exp(-b)), 0.0)
    o_intra = jnp.einsum('bhnts,bhnsv->bhntv', attn, vc)
    upd = jnp.einsum('bhnsc,bhnsv->bhncv', kc * jnp.exp(b_end - b), vc)
    dec = jnp.exp(b_end[:, :, :, 0, :])

    def step(state, inp):
        d, u = inp
        return d[..., None] * state + u, state

    s0 = jnp.zeros((B, H, dk, dk), jnp.float32)
    _, s_prev = lax.scan(step, s0, (jnp.moveaxis(dec, 2, 0), jnp.moveaxis(upd, 2, 0)))
    o_inter = jnp.einsum('bhntc,nbhcv->bhntv', q_dec, s_prev)
    o = (o_intra + o_inter).transpose(0, 2, 3, 1, 4).reshape(B, T, H, dk)
    o = _rms_norm(o, norm_w) * jax.nn.silu(g.reshape(B, T, H, dk))
    return o.reshape(B, T, H * dk)


GELU_C0 = float(np.sqrt(2.0 / np.pi))
GELU_C1 = GELU_C0 * 0.044715


def _gelu(x):
    half = 0.5 * x
    return half + half * jnp.tanh(x * (GELU_C0 + GELU_C1 * (x * x)))


def _head_norm(x, w):
    return x * lax.rsqrt(jnp.mean(x * x, axis=-1, keepdims=True) + NORM_EPS) * w


def _kv_prep_body(c_ref, s_ref, w_ref, ksw_ref, kww_ref, kn_ref, vv_ref, cf_ref):
    dh = NSA_HEAD_DIM
    for h in range(NSA_KV_HEADS):
        k_cols = slice(h * dh, (h + 1) * dh)
        v_cols = slice(KV_W + h * dh, KV_W + (h + 1) * dh)
        cf_ref[0, h] = c_ref[:, k_cols]
        cf_ref[1, h] = c_ref[:, v_cols]
        kn_ref[0, h] = _head_norm(s_ref[:, k_cols], ksw_ref[...]).astype(jnp.bfloat16)
        vv_ref[0, h] = s_ref[:, v_cols].astype(jnp.bfloat16)
        kn_ref[1, h] = _head_norm(w_ref[:, k_cols], kww_ref[...]).astype(jnp.bfloat16)
        vv_ref[1, h] = w_ref[:, v_cols].astype(jnp.bfloat16)


def _kv_prep(proj, ks_w, kw_w, tm):
    n = proj.shape[0]
    dh = NSA_HEAD_DIM
    pair = 2 * KV_W
    col0 = KV_COL0 // pair
    head_block = pl.BlockSpec((2, NSA_KV_HEADS, tm, dh), lambda i: (0, 0, i, 0))
    w_spec = pl.BlockSpec((1, dh), lambda i: (0, 0))
    return pl.pallas_call(
        _kv_prep_body,
        grid=(n // tm,),
        in_specs=[pl.BlockSpec((tm, pair), lambda i: (i, col0)),
                  pl.BlockSpec((tm, pair), lambda i: (i, col0 + 1)),
                  pl.BlockSpec((tm, pair), lambda i: (i, col0 + 2)),
                  w_spec, w_spec],
        out_specs=[head_block, head_block, head_block],
        out_shape=[jax.ShapeDtypeStruct((2, NSA_KV_HEADS, n, dh), jnp.bfloat16),
                   jax.ShapeDtypeStruct((2, NSA_KV_HEADS, n, dh), jnp.bfloat16),
                   jax.ShapeDtypeStruct((2, NSA_KV_HEADS, n, dh), jnp.float32)],
        compiler_params=pltpu.CompilerParams(
            dimension_semantics=("arbitrary",), vmem_limit_bytes=VMEM_LIMIT_BYTES),
        name="kv_prep",
    )(proj, proj, proj, ks_w.reshape(1, dh), kw_w.reshape(1, dh))


def _cmp_kv_body(r_ref, pos_ref, w1_ref, w2_ref, nw_ref, o_ref):
    half = CMP_STRIDE * NSA_HEAD_DIM
    n_strips = r_ref.shape[2]
    for kind in range(2):
        strips = r_ref[kind, 0]
        top = (strips + pos_ref[kind, 0:1, :]).astype(jnp.bfloat16)
        bot = (strips + pos_ref[kind, 1:2, :]).astype(jnp.bfloat16)
        a = jnp.dot(top, w1_ref[kind, :half, :], preferred_element_type=jnp.float32)
        b = jnp.dot(bot, w1_ref[kind, half:, :], preferred_element_type=jnp.float32)
        hid = _gelu(a + pltpu.roll(b, n_strips - 1, 0))
        out = jnp.dot(hid.astype(jnp.bfloat16), w2_ref[kind], preferred_element_type=jnp.float32)
        if kind == 0:
            out = _head_norm(out, nw_ref[...])
        o_ref[kind, 0, 0] = out.astype(jnp.bfloat16)


def _cmp_kv(cf, pos, w1, w2, kc_w, batch):
    _, hkv, n, dh = cf.shape
    t = n // batch
    n_strips = t // CMP_STRIDE
    strips = cf.reshape(2, hkv, n // CMP_STRIDE, CMP_STRIDE * dh)
    return pl.pallas_call(
        _cmp_kv_body,
        grid=(batch, hkv),
        in_specs=[pl.BlockSpec((2, 1, n_strips, CMP_STRIDE * dh), lambda b, g: (0, g, b, 0)),
                  pl.BlockSpec(pos.shape, lambda b, g: (0, 0, 0)),
                  pl.BlockSpec(w1.shape, lambda b, g: (0, 0, 0)),
                  pl.BlockSpec(w2.shape, lambda b, g: (0, 0, 0)),
                  pl.BlockSpec((1, dh), lambda b, g: (0, 0))],
        out_specs=pl.BlockSpec((2, 1, 1, n_strips, dh), lambda b, g: (0, g, b, 0, 0)),
        out_shape=jax.ShapeDtypeStruct((2, hkv, batch, n_strips, dh), jnp.bfloat16),
        compiler_params=pltpu.CompilerParams(
            dimension_semantics=("arbitrary", "arbitrary"), vmem_limit_bytes=VMEM_LIMIT_BYTES),
        name="cmp_kv",
    )(strips, pos, w1, w2, kc_w.reshape(1, dh))


def _flash_step(q, k, v, valid, bias, m_ref, l_ref, acc_ref, g, tq):
    tk = k.shape[0]
    s = lax.dot_general(q, k, (((1,), (1,)), ((), ())), preferred_element_type=jnp.float32).reshape(g, tq, tk)
    s = jnp.where(valid, s - bias, NEG_INF)
    m_old = m_ref[...].reshape(g, tq, 1)
    m_new = jnp.maximum(m_old, jnp.max(s, axis=-1, keepdims=True))
    p = jnp.where(valid, jnp.exp(s - m_new), 0.0)
    alpha = jnp.exp(m_old - m_new)
    l_ref[...] = (alpha * l_ref[...].reshape(g, tq, 1) + jnp.sum(p, axis=-1, keepdims=True)).reshape(g * tq, 1)
    pv = jnp.dot(p.reshape(g * tq, tk).astype(jnp.bfloat16), v, preferred_element_type=jnp.float32)
    acc_ref[...] = alpha.reshape(g * tq, 1) * acc_ref[...] + pv
    m_ref[...] = m_new.reshape(g * tq, 1)


def _nsa_attn_body(q_ref, gate_ref, kc_ref, vc_ref, ks_ref, vs_ref, kw_ref, vw_ref, qw_ref, slope_ref, o_ref,
                   qs_ref, sel_ref, oc_ref, ms_ref, ls_ref, as_ref, mw_ref, lw_ref, aw_ref, *, tq, tk):
    i = pl.program_id(2)
    j = pl.program_id(3)
    G, dh = NSA_GROUP, NSA_HEAD_DIM
    last_j = (i * tq + tq - 1) // tk
    slope = slope_ref[0]
    n_sel = sel_ref.shape[-1]

    @pl.when(j == 0)
    def _():
        x = q_ref[...]
        for r in range(G):
            qn = _head_norm(x[:, r * dh:(r + 1) * dh], qw_ref[...]) * (dh ** -0.5)
            qs_ref[r * tq:(r + 1) * tq, :] = qn.astype(jnp.bfloat16)
        for ref in (ms_ref, mw_ref):
            ref[...] = jnp.full(ref.shape, NEG_INF, jnp.float32)
        for ref in (ls_ref, lw_ref, as_ref, aw_ref):
            ref[...] = jnp.zeros(ref.shape, jnp.float32)

        n_cmp_pad = kc_ref.shape[-2]
        s = lax.dot_general(qs_ref[...], kc_ref[0, 0, 0], (((1,), (1,)), ((), ())),
                            preferred_element_type=jnp.float32).reshape(G, tq, n_cmp_pad)
        t_pos = i * tq + lax.broadcasted_iota(jnp.int32, (tq, n_cmp_pad), 0)
        blk = lax.broadcasted_iota(jnp.int32, (tq, n_cmp_pad), 1)
        dist = t_pos - (blk * CMP_STRIDE + CMP_BLOCK - 1)
        valid = ((dist >= 0) & (blk < n_cmp_pad - 1))[None]
        s = jnp.where(valid, s - slope * dist.astype(jnp.float32)[None], NEG_INF)
        p = jnp.where(valid, jnp.exp(s - jnp.max(s, axis=-1, keepdims=True)), 0.0)
        denom = jnp.maximum(jnp.sum(p, axis=-1, keepdims=True), 1e-30)
        p = (p / denom).astype(jnp.bfloat16)
        oc_ref[...] = jnp.dot(p.reshape(G * tq, n_cmp_pad), vc_ref[0, 0, 0], preferred_element_type=jnp.float32)

        cb = lax.broadcasted_iota(jnp.int32, (n_cmp_pad, n_sel), 0) * CMP_STRIDE
        sb = lax.broadcasted_iota(jnp.int32, (n_cmp_pad, n_sel), 1) * SEL_BLOCK
        overlap = ((cb < sb + SEL_BLOCK) & (cb + CMP_BLOCK > sb)).astype(jnp.bfloat16)
        imp = jnp.dot(p[0], overlap, preferred_element_type=jnp.float32)
        for r in range(1, G):
            imp = imp + jnp.dot(p[r], overlap, preferred_element_type=jnp.float32)
        cur = (i * tq + lax.broadcasted_iota(jnp.int32, (tq, n_sel), 0)) // SEL_BLOCK
        jb = lax.broadcasted_iota(jnp.int32, (tq, n_sel), 1)
        forced = (jb == 0) | ((jb <= cur) & (jb > cur - N_LOCAL))
        imp = jnp.where(forced, FORCE_SCORE, jnp.where(jb > cur, -FORCE_SCORE, imp))
        rank = jnp.zeros((tq, n_sel), jnp.float32)
        for c in range(n_sel):
            col = imp[:, c:c + 1]
            ahead = (col > imp) | ((col == imp) & (jb > c))
            rank = rank + jnp.where(ahead, 1.0, 0.0)
        sel_ref[...] = jnp.where(rank < min(N_SELECT, n_sel), 1.0, 0.0).astype(jnp.bfloat16)

    @pl.when(j <= last_j)
    def _():
        dist = (i * tq - j * tk) + lax.broadcasted_iota(jnp.int32, (tq, tk), 0) \
            - lax.broadcasted_iota(jnp.int32, (tq, tk), 1)
        bias = slope * dist.astype(jnp.float32)[None]
        blk = lax.broadcasted_iota(jnp.int32, (n_sel, tk), 0)
        col_blk = j * (tk // SEL_BLOCK) + lax.broadcasted_iota(jnp.int32, (n_sel, tk), 1) // SEL_BLOCK
        expand = (blk == col_blk).astype(jnp.bfloat16)
        picked = jnp.dot(sel_ref[...], expand, preferred_element_type=jnp.float32)
        valid = ((dist >= 0) & (picked > 0.5))[None]
        _flash_step(qs_ref[...], ks_ref[0, 0], vs_ref[0, 0], valid, bias, ms_ref, ls_ref, as_ref, G, tq)

    @pl.when((j <= last_j) & (j * tk + tk - 1 >= i * tq - (WINDOW - 1)))
    def _():
        dist = (i * tq - j * tk) + lax.broadcasted_iota(jnp.int32, (tq, tk), 0) \
            - lax.broadcasted_iota(jnp.int32, (tq, tk), 1)
        bias = slope * dist.astype(jnp.float32)[None]
        valid = ((dist >= 0) & (dist < WINDOW))[None]
        _flash_step(qs_ref[...], kw_ref[0, 0], vw_ref[0, 0], valid, bias, mw_ref, lw_ref, aw_ref, G, tq)

    @pl.when(j == pl.num_programs(3) - 1)
    def _():
        gates = jax.nn.sigmoid(gate_ref[0])
        o_sel = as_ref[...] / ls_ref[...]
        o_win = aw_ref[...] / lw_ref[...]
        o_cmp = oc_ref[...]
        outs = []
        for r in range(G):
            rows = slice(r * tq, (r + 1) * tq)
            outs.append(gates[:, 3 * r:3 * r + 1] * o_cmp[rows] + gates[:, 3 * r + 1:3 * r + 2] * o_sel[rows]
                        + gates[:, 3 * r + 2:3 * r + 3] * o_win[rows])
        o_ref[...] = jnp.concatenate(outs, axis=-1)


def _nsa_attn(proj, gates, cmp_kv, kn, vv, q_w, slopes, batch, tq=256, tk=512):
    n = proj.shape[0]
    t = n // batch
    G, dh, hkv = NSA_GROUP, NSA_HEAD_DIM, NSA_KV_HEADS
    n_cmp_pad = cmp_kv.shape[-2]
    n_sel = t // SEL_BLOCK
    q_blk0 = Q_COL0 // (G * dh)

    def kv_map(kind):
        def index(b, g, i, j):
            return (kind, g, b * (t // tk) + jnp.minimum(j, (i * tq + tq - 1) // tk), 0)
        return index

    def cmp_map(kind):
        return lambda b, g, i, j: (kind, g, b, 0, 0)

    kv_block = (1, 1, tk, dh)
    cmp_block = (1, 1, 1, n_cmp_pad, dh)
    slab = pltpu.VMEM((G * tq, dh), jnp.float32)
    stat = pltpu.VMEM((G * tq, 1), jnp.float32)
    return pl.pallas_call(
        functools.partial(_nsa_attn_body, tq=tq, tk=tk),
        grid=(batch, hkv, t // tq, t // tk),
        in_specs=[
            pl.BlockSpec((tq, G * dh), lambda b, g, i, j: (b * (t // tq) + i, q_blk0 + g)),
            pl.BlockSpec((1, tq, 3 * G), lambda b, g, i, j: (g, b * (t // tq) + i, 0)),
            pl.BlockSpec(cmp_block, cmp_map(0)), pl.BlockSpec(cmp_block, cmp_map(1)),
            pl.BlockSpec(kv_block, kv_map(0)), pl.BlockSpec(kv_block, kv_map(0)),
            pl.BlockSpec(kv_block, kv_map(1)), pl.BlockSpec(kv_block, kv_map(1)),
            pl.BlockSpec((1, dh), lambda b, g, i, j: (0, 0)),
            pl.BlockSpec((1, G, 1, 1), lambda b, g, i, j: (g, 0, 0, 0)),
        ],
        out_specs=pl.BlockSpec((tq, G * dh), lambda b, g, i, j: (b * (t // tq) + i, g)),
        out_shape=jax.ShapeDtypeStruct((n, hkv * G * dh), jnp.float32),
        scratch_shapes=[pltpu.VMEM((G * tq, dh), jnp.bfloat16), pltpu.VMEM((tq, n_sel), jnp.bfloat16),
                        slab, stat, stat, slab, stat, stat, slab],
        compiler_params=pltpu.CompilerParams(
            dimension_semantics=("arbitrary",) * 4, vmem_limit_bytes=VMEM_LIMIT_BYTES),
        name="nsa_attn",
    )(proj, gates, cmp_kv, cmp_kv, kn, vv, kn, vv, q_w.reshape(1, dh), slopes)


def _nsa(proj, batch, q_norm_w, kc_norm_w, ks_norm_w, kw_norm_w, pos_k, pos_v, w_ck1, w_ck2, w_cv1, w_cv2):
    n = proj.shape[0]
    G, hkv = NSA_GROUP, NSA_KV_HEADS
    kn, vv, cf = _kv_prep(proj, ks_norm_w, kw_norm_w, 512)
    half = CMP_STRIDE * NSA_HEAD_DIM
    pos = jnp.stack([pos_k.reshape(2, half), pos_v.reshape(2, half)])
    w1 = jnp.stack([w_ck1, w_cv1]).astype(jnp.bfloat16)
    w2 = jnp.stack([w_ck2, w_cv2]).astype(jnp.bfloat16)
    cmp_kv = _cmp_kv(cf, pos, w1, w2, kc_norm_w, batch)
    gates = proj[:, GATE_COL0:GATE_COL0 + N_GATES].reshape(n, hkv, 3 * G).transpose(1, 0, 2)
    slopes = jnp.asarray(2.0 ** (-8.0 * np.arange(1, NSA_Q_HEADS + 1) / NSA_Q_HEADS), jnp.float32)
    return _nsa_attn(proj, gates, cmp_kv, kn, vv, q_norm_w, slopes.reshape(hkv, G, 1, 1), batch)


BIG_NEG = -3.0e38
PEER_CAND = [(a, b) for a in range(PEER_TOPK) for b in range(PEER_TOPK) if (a + 1) * (b + 1) <= PEER_TOPK]
PEER_CAND_ROWS = -(-len(PEER_CAND) // 8) * 8


def _top_rows(s, k):
    rows = s.shape[0]
    ridx = lax.broadcasted_iota(jnp.int32, s.shape, 0)
    out = []
    cur = s
    for _ in range(k):
        mk = jnp.max(cur, axis=0, keepdims=True)
        first = jnp.min(jnp.where(cur == mk, ridx, rows), axis=0, keepdims=True)
        out.append(mk)
        cur = jnp.where(ridx == first, BIG_NEG, cur)
    return out


def _peer_route_body(h_ref, g_ref, wq_ref, keys_ref, xt_ref, s2_ref, a2_ref, s1_ref, r_ref, tau_ref, qt_ref, cand_ref):
    x = h_ref[...]
    hn = x * lax.rsqrt(jnp.mean(x * x, axis=-1, keepdims=True) + NORM_EPS) * g_ref[...]
    hnt = hn.T.astype(jnp.bfloat16)
    xt_ref[...] = hnt
    qt_ref[...] = jnp.dot(wq_ref[...], hnt, preferred_element_type=jnp.float32).astype(jnp.bfloat16)
    cand_ref[...] = jnp.full(cand_ref.shape, BIG_NEG, jnp.float32)
    kd = PEER_KEY_DIM // 2

    def head(hd, carry):
        q1 = qt_ref[pl.ds(pl.multiple_of(hd * 2 * kd, kd), kd), :]
        q2 = qt_ref[pl.ds(pl.multiple_of(hd * 2 * kd + kd, kd), kd), :]
        s1 = jnp.dot(keys_ref[2 * hd], q1, preferred_element_type=jnp.float32)
        s2 = jnp.dot(keys_ref[2 * hd + 1], q2, preferred_element_type=jnp.float32)
        t1 = _top_rows(s1, PEER_TOPK)
        t2 = _top_rows(s2, PEER_TOPK)
        cmax = t1[0] + t2[0]
        cands = [t1[a] + t2[b] for a, b in PEER_CAND]
        for i, c in enumerate(cands):
            cand_ref[i:i + 1, :] = c
        call = cand_ref[...]
        n_gt = jnp.zeros(call.shape, jnp.float32)
        for c in cands:
            n_gt = n_gt + jnp.where(c > call, 1.0, 0.0)
        tau = jnp.min(jnp.where(n_gt <= PEER_TOPK - 1, call, -BIG_NEG), axis=0, keepdims=True)
        z = jnp.sum(jnp.where(call >= tau, jnp.exp(call - cmax), 0.0), axis=0, keepdims=True)
        s2_ref[hd] = s2
        a2_ref[hd] = jnp.exp(s2 - t2[0])
        s1_ref[hd] = s1
        tau_ref[hd] = tau
        r_ref[hd] = jnp.exp(s1 - t1[0]) / z
        return carry

    lax.fori_loop(0, PEER_HEADS, head, 0)


def _peer_route(h, gain, wq_t, keys, tt):
    n, d = h.shape
    hp, nk, kd = keys.shape
    stat = jax.ShapeDtypeStruct((PEER_HEADS, nk, n), jnp.float32)
    stat_spec = pl.BlockSpec((PEER_HEADS, nk, tt), lambda i: (0, 0, i))
    return pl.pallas_call(
        _peer_route_body,
        grid=(n // tt,),
        in_specs=[
            pl.BlockSpec((tt, d), lambda i: (i, 0)),
            pl.BlockSpec((1, d), lambda i: (0, 0)),
            pl.BlockSpec(wq_t.shape, lambda i: (0, 0)),
            pl.BlockSpec(keys.shape, lambda i: (0, 0, 0)),
        ],
        out_specs=[pl.BlockSpec((d, tt), lambda i: (0, i)), stat_spec, stat_spec, stat_spec, stat_spec,
                   pl.BlockSpec((PEER_HEADS, 1, tt), lambda i: (0, 0, i))],
        out_shape=[jax.ShapeDtypeStruct((d, n), jnp.bfloat16), stat, stat, stat, stat,
                   jax.ShapeDtypeStruct((PEER_HEADS, 1, n), jnp.float32)],
        scratch_shapes=[pltpu.VMEM((wq_t.shape[0], tt), jnp.bfloat16),
                        pltpu.VMEM((PEER_CAND_ROWS, tt), jnp.float32)],
        compiler_params=pltpu.CompilerParams(
            dimension_semantics=("arbitrary",), vmem_limit_bytes=VMEM_LIMIT_BYTES),
        name="peer_route",
    )(h, gain.reshape(1, d), wq_t, keys)


def _peer_experts_body(xt_ref, u_ref, vt_ref, s2_ref, a2_ref, s1_ref, r_ref, tau_ref, o_ref, *, n_i1):
    @pl.when(pl.program_id(1) == 0)
    def _():
        o_ref[...] = jnp.zeros(o_ref.shape, jnp.float32)

    act = _gelu(jnp.dot(u_ref[...], xt_ref[...], preferred_element_type=jnp.float32))
    parts = []
    for i1 in range(n_i1):
        w = None
        for hd in range(PEER_HEADS):
            picked = s1_ref[hd, i1:i1 + 1, :] + s2_ref[hd] >= tau_ref[hd]
            term = jnp.where(picked, a2_ref[hd], 0.0) * r_ref[hd, i1:i1 + 1, :]
            w = term if w is None else w + term
        parts.append((w * act[i1 * N_KEYS:(i1 + 1) * N_KEYS]).astype(jnp.bfloat16))
    p = jnp.concatenate(parts, axis=0)
    o_ref[...] += jnp.dot(vt_ref[...], p, preferred_element_type=jnp.float32)


def _peer_experts(xt, u, vt, s2, a2, s1, r, tau, tt, te):
    d, n = xt.shape
    e = u.shape[0]
    n_i1 = te // N_KEYS
    full = pl.BlockSpec((PEER_HEADS, N_KEYS, tt), lambda i, j: (0, 0, i))
    part = pl.BlockSpec((PEER_HEADS, n_i1, tt), lambda i, j: (0, j, i))
    per_token = pl.BlockSpec((PEER_HEADS, 1, tt), lambda i, j: (0, 0, i))
    return pl.pallas_call(
        functools.partial(_peer_experts_body, n_i1=n_i1),
        grid=(n // tt, e // te),
        in_specs=[
            pl.BlockSpec((d, tt), lambda i, j: (0, i)),
            pl.BlockSpec((te, d), lambda i, j: (j, 0)),
            pl.BlockSpec((d, te), lambda i, j: (0, j)),
            full, full, part, part, per_token,
        ],
        out_specs=pl.BlockSpec((d, tt), lambda i, j: (0, i)),
        out_shape=jax.ShapeDtypeStruct((d, n), jnp.float32),
        compiler_params=pltpu.CompilerParams(
            dimension_semantics=("arbitrary", "arbitrary"), vmem_limit_bytes=VMEM_LIMIT_BYTES),
        name="peer_experts",
    )(xt, u, vt, s2, a2, s1, r, tau)


def _peer(h, gain, w_q, sub_keys, u_tab, v_tab):
    keys = sub_keys.reshape(PEER_HEADS * 2, N_KEYS, PEER_KEY_DIM // 2).astype(jnp.bfloat16)
    xt, s2, a2, s1, r, tau = _peer_route(h, gain, w_q.T.astype(jnp.bfloat16), keys, 256)
    out_t = _peer_experts(xt, u_tab.astype(jnp.bfloat16), v_tab.T.astype(jnp.bfloat16),
                          s2, a2, s1, r, tau, 512, 1024)
    return out_t.T


def kernel(x, norm1_w, w_in, hg_lb_logits, hg_norm_w, q_norm_w, kc_norm_w, ks_norm_w, kw_norm_w,
           cmp_pos_k, cmp_pos_v, w_ck1, w_ck2, w_cv1, w_cv2, w_out, norm2_w,
           peer_w_q, peer_sub_keys, peer_u, peer_v):
    B, T, D = x.shape
    n = B * T
    layer = 0
    lower_bounds = jnp.cumsum(jax.nn.softmax(hg_lb_logits, axis=0), axis=0)
    xt = x.reshape(n, D)

    w_in_b = jnp.pad(w_in[layer].astype(jnp.bfloat16), ((0, 0), (0, IN_COLS_PADDED - IN_COLS)))
    proj = _norm_matmul(xt, norm1_w[layer], w_in_b, 512, 512)
    hq, hf, hi, hg = (proj[:, c * HG_WIDTH:(c + 1) * HG_WIDTH].reshape(B, T, HG_WIDTH) for c in range(4))
    hg_out = _hgrn2(hq, hf, hi, hg, lower_bounds[layer], hg_norm_w[layer]).reshape(n, HG_WIDTH)
    nsa_out = _nsa(proj, B, q_norm_w[layer], kc_norm_w[layer], ks_norm_w[layer], kw_norm_w[layer],
                   cmp_pos_k[layer], cmp_pos_v[layer], w_ck1[layer], w_ck2[layer], w_cv1[layer], w_cv2[layer])
    h = _out_proj(hg_out, nsa_out, w_out[layer].astype(jnp.bfloat16), xt, 512, 512)

    y = h + _peer(h, norm2_w[layer], peer_w_q[layer], peer_sub_keys[layer], peer_u[layer], peer_v[layer])
    return y.reshape(B, T, D)
```

```python
import functools

import jax
import jax.numpy as jnp
import numpy as np
from jax import lax
from jax.experimental import pallas as pl
from jax.experimental.pallas import tpu as pltpu

D_MODEL = 2048
HG_WIDTH = 1024
HG_HEAD_DIM = 128
HG_HEADS = 8
HG_CHUNK = 64
NSA_WIDTH = 1024
NSA_HEAD_DIM = 64
NSA_Q_HEADS = 16
NSA_KV_HEADS = 4
NSA_GROUP = 4
CMP_BLOCK = 32
CMP_STRIDE = 16
CMP_HIDDEN = 256
SEL_BLOCK = 64
N_SELECT = 16
N_LOCAL = 2
WINDOW = 512
FORCE_SCORE = 1e9
NEG_INF = -1e30
PEER_HEADS = 8
N_KEYS = 128
PEER_KEY_DIM = 256
PEER_TOPK = 16
NORM_EPS = 1e-6
KV_W = NSA_KV_HEADS * NSA_HEAD_DIM
N_GATES = 3 * NSA_Q_HEADS
IN_SIZES = [HG_WIDTH] * 4 + [NSA_WIDTH] + [KV_W] * 6 + [N_GATES]
IN_COLS = sum(IN_SIZES)
IN_COLS_PADDED = 7168
Q_COL0 = 4 * HG_WIDTH
KV_COL0 = Q_COL0 + NSA_WIDTH
GATE_COL0 = KV_COL0 + 6 * KV_W

VMEM_LIMIT_BYTES = 48 * 1024 * 1024
LANES = 128


def _norm_matmul_body(x_ref, g_ref, w_ref, o_ref, xn_ref):
    @pl.when(pl.program_id(1) == 0)
    def _():
        x = x_ref[...]
        r = lax.rsqrt(jnp.mean(x * x, axis=-1, keepdims=True) + NORM_EPS)
        xn_ref[...] = (x * r * g_ref[...]).astype(jnp.bfloat16)

    o_ref[...] = jnp.dot(xn_ref[...], w_ref[...], preferred_element_type=jnp.float32).astype(o_ref.dtype)


def _norm_matmul(x, gain, w, tm, tn, out_dtype=jnp.float32):
    m, k = x.shape
    n = w.shape[1]
    return pl.pallas_call(
        _norm_matmul_body,
        grid=(m // tm, n // tn),
        in_specs=[
            pl.BlockSpec((tm, k), lambda i, j: (i, 0)),
            pl.BlockSpec((1, k), lambda i, j: (0, 0)),
            pl.BlockSpec((k, tn), lambda i, j: (0, j)),
        ],
        out_specs=pl.BlockSpec((tm, tn), lambda i, j: (i, j)),
        out_shape=jax.ShapeDtypeStruct((m, n), out_dtype),
        scratch_shapes=[pltpu.VMEM((tm, k), jnp.bfloat16)],
        compiler_params=pltpu.CompilerParams(
            dimension_semantics=("arbitrary", "arbitrary"), vmem_limit_bytes=VMEM_LIMIT_BYTES),
        name="norm_matmul",
    )(x, gain.reshape(1, k), w)


def _out_proj_body(a1_ref, a2_ref, w1_ref, w2_ref, r_ref, o_ref):
    acc = jnp.dot(a1_ref[...].astype(jnp.bfloat16), w1_ref[...], preferred_element_type=jnp.float32)
    acc = acc + jnp.dot(a2_ref[...].astype(jnp.bfloat16), w2_ref[...], preferred_element_type=jnp.float32)
    o_ref[...] = r_ref[...] + acc


def _out_proj(a1, a2, w, res, tm, tn):
    m, k1 = a1.shape
    k2 = a2.shape[1]
    n = w.shape[1]
    return pl.pallas_call(
        _out_proj_body,
        grid=(m // tm, n // tn),
        in_specs=[
            pl.BlockSpec((tm, k1), lambda i, j: (i, 0)),
            pl.BlockSpec((tm, k2), lambda i, j: (i, 0)),
            pl.BlockSpec((k1, tn), lambda i, j: (0, j)),
            pl.BlockSpec((k2, tn), lambda i, j: (k1 // k2, j)),
            pl.BlockSpec((tm, tn), lambda i, j: (i, j)),
        ],
        out_specs=pl.BlockSpec((tm, tn), lambda i, j: (i, j)),
        out_shape=jax.ShapeDtypeStruct((m, n), jnp.float32),
        compiler_params=pltpu.CompilerParams(
            dimension_semantics=("arbitrary", "arbitrary"), vmem_limit_bytes=VMEM_LIMIT_BYTES),
        name="out_proj",
    )(a1, a2, w, w, res)


def _hgrn2_body(q_ref, f_ref, v_ref, g_ref, lb_ref, nw_ref, o_ref, qd_ref, kd_ref, ku_ref, vb_ref, dec_ref, sp_ref):
    t, dk = q_ref.shape
    C = HG_CHUNK
    nc = t // C
    lb = lb_ref[...]
    f = lb + (1.0 - lb) * jax.nn.sigmoid(f_ref[...])
    kf = 1.0 - f
    b = jnp.log(f)
    row = lax.broadcasted_iota(jnp.int32, (t, dk), 0) % C
    shift = 1
    while shift < C:
        b = b + jnp.where(row >= shift, pltpu.roll(b, shift, 0), 0.0)
        shift *= 2
    b3 = b.reshape(nc, C, dk)
    b_end = b3[:, C - 1:C, :]
    qd_ref[...] = (q_ref[...] * jnp.exp(b)).astype(jnp.bfloat16)
    kd_ref[...] = (kf * jnp.exp(-b)).astype(jnp.bfloat16)
    ku_ref[...] = (kf.reshape(nc, C, dk) * jnp.exp(b_end - b3)).reshape(t, dk).astype(jnp.bfloat16)
    vb_ref[...] = v_ref[...].astype(jnp.bfloat16)
    dec_ref[...] = jnp.exp(b_end)

    st = jnp.zeros((dk, dk), jnp.float32)
    for n in range(nc):
        rows = slice(n * C, (n + 1) * C)
        sp_ref[n] = st.astype(jnp.bfloat16)
        upd_t = lax.dot_general(vb_ref[rows, :], ku_ref[rows, :], (((0,), (0,)), ((), ())),
                                preferred_element_type=jnp.float32)
        st = dec_ref[n] * st + upd_t

    causal = lax.broadcasted_iota(jnp.int32, (C, C), 0) >= lax.broadcasted_iota(jnp.int32, (C, C), 1)
    for n in range(nc):
        rows = slice(n * C, (n + 1) * C)
        qd = qd_ref[rows, :]
        attn = lax.dot_general(qd, kd_ref[rows, :], (((1,), (1,)), ((), ())), preferred_element_type=jnp.float32)
        attn = jnp.where(causal, attn, 0.0).astype(jnp.bfloat16)
        o = jnp.dot(attn, vb_ref[rows, :], preferred_element_type=jnp.float32)
        o = o + lax.dot_general(qd, sp_ref[n], (((1,), (1,)), ((), ())), preferred_element_type=jnp.float32)
        o = o * lax.rsqrt(jnp.mean(o * o, axis=-1, keepdims=True) + NORM_EPS) * nw_ref[...]
        o_ref[rows, :] = o * jax.nn.silu(g_ref[rows, :])


def _hgrn2(proj, lb, norm_w, batch):
    n = proj.shape[0]
    t = n // batch
    dk, H = HG_HEAD_DIM, HG_HEADS
    nc = t // HG_CHUNK

    def part(p):
        return pl.BlockSpec((t, dk), lambda b, h: (b, p * H + h))

    slab = pltpu.VMEM((t, dk), jnp.bfloat16)
    return pl.pallas_call(
        _hgrn2_body,
        grid=(batch, H),
        in_specs=[part(0), part(1), part(2), part(3),
                  pl.BlockSpec((1, dk), lambda b, h: (0, h)),
                  pl.BlockSpec((1, dk), lambda b, h: (0, 0))],
        out_specs=pl.BlockSpec((t, dk), lambda b, h: (b, h)),
        out_shape=jax.ShapeDtypeStruct((n, H * dk), jnp.float32),
        scratch_shapes=[slab, slab, slab, slab, pltpu.VMEM((nc, 1, dk), jnp.float32),
                        pltpu.VMEM((nc, dk, dk), jnp.bfloat16)],
        compiler_params=pltpu.CompilerParams(
            dimension_semantics=("arbitrary", "arbitrary"), vmem_limit_bytes=VMEM_LIMIT_BYTES),
        name="hgrn2",
    )(proj, proj, proj, proj, lb.reshape(1, H * dk), norm_w.reshape(1, dk))


GELU_C0 = float(np.sqrt(2.0 / np.pi))
GELU_C1 = GELU_C0 * 0.044715


def _gelu(x):
    half = 0.5 * x
    return half + half * jnp.tanh(x * (GELU_C0 + GELU_C1 * (x * x)))


def _head_norm(x, w):
    return x * lax.rsqrt(jnp.mean(x * x, axis=-1, keepdims=True) + NORM_EPS) * w


def _kv_prep_body(c_ref, s_ref, w_ref, ksw_ref, kww_ref, kn_ref, vv_ref, cf_ref):
    dh = NSA_HEAD_DIM
    for h in range(NSA_KV_HEADS):
        k_cols = slice(h * dh, (h + 1) * dh)
        v_cols = slice(KV_W + h * dh, KV_W + (h + 1) * dh)
        cf_ref[0, h] = c_ref[:, k_cols]
        cf_ref[1, h] = c_ref[:, v_cols]
        kn_ref[0, h] = _head_norm(s_ref[:, k_cols], ksw_ref[...]).astype(jnp.bfloat16)
        vv_ref[0, h] = s_ref[:, v_cols].astype(jnp.bfloat16)
        kn_ref[1, h] = _head_norm(w_ref[:, k_cols], kww_ref[...]).astype(jnp.bfloat16)
        vv_ref[1, h] = w_ref[:, v_cols].astype(jnp.bfloat16)


def _kv_prep(proj, ks_w, kw_w, tm):
    n = proj.shape[0]
    dh = NSA_HEAD_DIM
    pair = 2 * KV_W
    col0 = KV_COL0 // pair
    head_block = pl.BlockSpec((2, NSA_KV_HEADS, tm, dh), lambda i: (0, 0, i, 0))
    w_spec = pl.BlockSpec((1, dh), lambda i: (0, 0))
    return pl.pallas_call(
        _kv_prep_body,
        grid=(n // tm,),
        in_specs=[pl.BlockSpec((tm, pair), lambda i: (i, col0)),
                  pl.BlockSpec((tm, pair), lambda i: (i, col0 + 1)),
                  pl.BlockSpec((tm, pair), lambda i: (i, col0 + 2)),
                  w_spec, w_spec],
        out_specs=[head_block, head_block, head_block],
        out_shape=[jax.ShapeDtypeStruct((2, NSA_KV_HEADS, n, dh), jnp.bfloat16),
                   jax.ShapeDtypeStruct((2, NSA_KV_HEADS, n, dh), jnp.bfloat16),
                   jax.ShapeDtypeStruct((2, NSA_KV_HEADS, n, dh), jnp.float32)],
        compiler_params=pltpu.CompilerParams(
            dimension_semantics=("arbitrary",), vmem_limit_bytes=VMEM_LIMIT_BYTES),
        name="kv_prep",
    )(proj, proj, proj, ks_w.reshape(1, dh), kw_w.reshape(1, dh))


def _cmp_kv_body(r_ref, pos_ref, w1_ref, w2_ref, nw_ref, o_ref):
    half = CMP_STRIDE * NSA_HEAD_DIM
    n_strips = r_ref.shape[2]
    for kind in range(2):
        strips = r_ref[kind, 0]
        top = (strips + pos_ref[kind, 0:1, :]).astype(jnp.bfloat16)
        bot = (strips + pos_ref[kind, 1:2, :]).astype(jnp.bfloat16)
        a = jnp.dot(top, w1_ref[kind, :half, :], preferred_element_type=jnp.float32)
        b = jnp.dot(bot, w1_ref[kind, half:, :], preferred_element_type=jnp.float32)
        hid = _gelu(a + pltpu.roll(b, n_strips - 1, 0))
        out = jnp.dot(hid.astype(jnp.bfloat16), w2_ref[kind], preferred_element_type=jnp.float32)
        if kind == 0:
            out = _head_norm(out, nw_ref[...])
        o_ref[kind, 0, 0] = out.astype(jnp.bfloat16)


def _cmp_kv(cf, pos, w1, w2, kc_w, batch):
    _, hkv, n, dh = cf.shape
    t = n // batch
    n_strips = t // CMP_STRIDE
    strips = cf.reshape(2, hkv, n // CMP_STRIDE, CMP_STRIDE * dh)
    return pl.pallas_call(
        _cmp_kv_body,
        grid=(batch, hkv),
        in_specs=[pl.BlockSpec((2, 1, n_strips, CMP_STRIDE * dh), lambda b, g: (0, g, b, 0)),
                  pl.BlockSpec(pos.shape, lambda b, g: (0, 0, 0)),
                  pl.BlockSpec(w1.shape, lambda b, g: (0, 0, 0)),
                  pl.BlockSpec(w2.shape, lambda b, g: (0, 0, 0)),
                  pl.BlockSpec((1, dh), lambda b, g: (0, 0))],
        out_specs=pl.BlockSpec((2, 1, 1, n_strips, dh), lambda b, g: (0, g, b, 0, 0)),
        out_shape=jax.ShapeDtypeStruct((2, hkv, batch, n_strips, dh), jnp.bfloat16),
        compiler_params=pltpu.CompilerParams(
            dimension_semantics=("arbitrary", "arbitrary"), vmem_limit_bytes=VMEM_LIMIT_BYTES),
        name="cmp_kv",
    )(strips, pos, w1, w2, kc_w.reshape(1, dh))


def _flash_step(qs_ref, k, v, slope_ref, m_ref, l_ref, acc_ref, base, mask_fn, tq):
    tk = k.shape[0]
    g = NSA_GROUP
    dist = base + lax.broadcasted_iota(jnp.int32, (tq, tk), 0) - lax.broadcasted_iota(jnp.int32, (tq, tk), 1)
    valid = mask_fn(dist)[None]
    s = lax.dot_general(qs_ref[...], k, (((1,), (1,)), ((), ())),
                        preferred_element_type=jnp.float32).reshape(g, tq, tk)
    s = jnp.where(valid, s - slope_ref[0] * dist.astype(jnp.float32)[None], NEG_INF)
    m_old = m_ref[...].reshape(g, tq, 1)
    m_new = jnp.maximum(m_old, jnp.max(s, axis=-1, keepdims=True))
    p = jnp.exp(s - m_new)
    alpha = jnp.exp(m_old - m_new)
    l_ref[...] = (alpha * l_ref[...].reshape(g, tq, 1) + jnp.sum(p, axis=-1, keepdims=True)).reshape(g * tq, 1)
    pv = jnp.dot(p.reshape(g * tq, tk).astype(jnp.bfloat16), v, preferred_element_type=jnp.float32)
    acc_ref[...] = alpha.reshape(g * tq, 1) * acc_ref[...] + pv
    m_ref[...] = m_new.reshape(g * tq, 1)


def _nsa_attn_body(q_ref, gate_ref, kc_ref, vc_ref, ks_ref, vs_ref, kw_ref, vw_ref, qw_ref, slope_ref, o_ref,
                   qs_ref, sel_ref, oc_ref, ms_ref, ls_ref, as_ref, mw_ref, lw_ref, aw_ref, *, tq, tk, n_sel):
    i = pl.program_id(2)
    j = pl.program_id(3)
    G, dh = NSA_GROUP, NSA_HEAD_DIM
    last_j = (i * tq + tq - 1) // tk

    @pl.when(j == 0)
    def _():
        x = q_ref[...]
        for r in range(G):
            qn = _head_norm(x[:, r * dh:(r + 1) * dh], qw_ref[...]) * (dh ** -0.5)
            qs_ref[r * tq:(r + 1) * tq, :] = qn.astype(jnp.bfloat16)
        for ref in (ms_ref, mw_ref):
            ref[...] = jnp.full(ref.shape, NEG_INF, jnp.float32)
        for ref in (ls_ref, lw_ref, as_ref, aw_ref):
            ref[...] = jnp.zeros(ref.shape, jnp.float32)

        n_cmp_pad = kc_ref.shape[-2]
        t_pos = i * tq + lax.broadcasted_iota(jnp.int32, (tq, n_cmp_pad), 0)
        blk = lax.broadcasted_iota(jnp.int32, (tq, n_cmp_pad), 1)
        dist = t_pos - (blk * CMP_STRIDE + CMP_BLOCK - 1)
        valid = (dist >= 0) & (blk < n_cmp_pad - 1)
        distf = dist.astype(jnp.float32)
        cb = lax.broadcasted_iota(jnp.int32, (n_sel, n_cmp_pad), 1) * CMP_STRIDE
        sb = lax.broadcasted_iota(jnp.int32, (n_sel, n_cmp_pad), 0) * SEL_BLOCK
        overlap_t = ((cb < sb + SEL_BLOCK) & (cb + CMP_BLOCK > sb)).astype(jnp.bfloat16)
        imp = None
        for r in range(G):
            rows = slice(r * tq, (r + 1) * tq)
            s = lax.dot_general(qs_ref[rows, :], kc_ref[0, 0, 0], (((1,), (1,)), ((), ())),
                                preferred_element_type=jnp.float32)
            s = jnp.where(valid, s - slope_ref[0, r] * distf, NEG_INF)
            p = jnp.where(valid, jnp.exp(s - jnp.max(s, axis=-1, keepdims=True)), 0.0)
            denom = jnp.maximum(jnp.sum(p, axis=-1, keepdims=True), 1e-30)
            p = (p / denom).astype(jnp.bfloat16)
            oc_ref[rows, :] = jnp.dot(p, vc_ref[0, 0, 0], preferred_element_type=jnp.float32)
            part = lax.dot_general(overlap_t, p, (((1,), (1,)), ((), ())), preferred_element_type=jnp.float32)
            imp = part if imp is None else imp + part

        cur = (i * tq + lax.broadcasted_iota(jnp.int32, (n_sel, tq), 1)) // SEL_BLOCK
        jb = lax.broadcasted_iota(jnp.int32, (n_sel, tq), 0)
        forced = (jb == 0) | ((jb <= cur) & (jb > cur - N_LOCAL))
        imp = jnp.where(forced, FORCE_SCORE, jnp.where(jb > cur, -FORCE_SCORE, imp))
        rank = jnp.zeros((n_sel, tq), jnp.float32)
        for c in range(n_sel):
            row = imp[c:c + 1, :]
            ahead = (row > imp) | ((row == imp) & (jb > c))
            rank = rank + jnp.where(ahead, 1.0, 0.0)
        sel_t = jnp.where(rank < min(N_SELECT, n_sel), 1.0, 0.0)
        sel_t = jnp.concatenate([sel_t, jnp.zeros((sel_ref.shape[1] - n_sel, tq), jnp.float32)], axis=0)
        sel_ref[...] = sel_t.T.astype(jnp.bfloat16)

    @pl.when(j <= last_j)
    def _():
        n_pad = sel_ref.shape[1]
        blk = lax.broadcasted_iota(jnp.int32, (n_pad, tk), 0)
        col_blk = j * (tk // SEL_BLOCK) + lax.broadcasted_iota(jnp.int32, (n_pad, tk), 1) // SEL_BLOCK
        expand = (blk == col_blk).astype(jnp.bfloat16)

        def mask(dist):
            picked = jnp.dot(sel_ref[...], expand, preferred_element_type=jnp.float32)
            return (dist >= 0) & (picked > 0.5)

        _flash_step(qs_ref, ks_ref[0, 0], vs_ref[0, 0], slope_ref, ms_ref, ls_ref, as_ref,
                    i * tq - j * tk, mask, tq)

    @pl.when((j <= last_j) & (j * tk + tk - 1 >= i * tq - (WINDOW - 1)))
    def _():
        _flash_step(qs_ref, kw_ref[0, 0], vw_ref[0, 0], slope_ref, mw_ref, lw_ref, aw_ref,
                    i * tq - j * tk, lambda dist: (dist >= 0) & (dist < WINDOW), tq)

    @pl.when(j == pl.num_programs(3) - 1)
    def _():
        gates = jax.nn.sigmoid(gate_ref[0])
        o_sel = as_ref[...] / ls_ref[...]
        o_win = aw_ref[...] / lw_ref[...]
        o_cmp = oc_ref[...]
        outs = []
        for r in range(G):
            rows = slice(r * tq, (r + 1) * tq)
            outs.append(gates[:, 3 * r:3 * r + 1] * o_cmp[rows] + gates[:, 3 * r + 1:3 * r + 2] * o_sel[rows]
                        + gates[:, 3 * r + 2:3 * r + 3] * o_win[rows])
        o_ref[...] = jnp.concatenate(outs, axis=-1)


def _nsa_attn(proj, gates, cmp_kv, kn, vv, q_w, slopes, batch, tq=256, tk=512):
    n = proj.shape[0]
    t = n // batch
    G, dh, hkv = NSA_GROUP, NSA_HEAD_DIM, NSA_KV_HEADS
    n_cmp_pad = cmp_kv.shape[-2]
    n_sel = t // SEL_BLOCK
    q_blk0 = Q_COL0 // (G * dh)

    def kv_map(kind):
        def index(b, g, i, j):
            return (kind, g, b * (t // tk) + jnp.minimum(j, (i * tq + tq - 1) // tk), 0)
        return index

    def cmp_map(kind):
        return lambda b, g, i, j: (kind, g, b, 0, 0)

    kv_block = (1, 1, tk, dh)
    cmp_block = (1, 1, 1, n_cmp_pad, dh)
    slab = pltpu.VMEM((G * tq, dh), jnp.float32)
    stat = pltpu.VMEM((G * tq, 1), jnp.float32)
    return pl.pallas_call(
        functools.partial(_nsa_attn_body, tq=tq, tk=tk, n_sel=n_sel),
        grid=(batch, hkv, t // tq, t // tk),
        in_specs=[
            pl.BlockSpec((tq, G * dh), lambda b, g, i, j: (b * (t // tq) + i, q_blk0 + g)),
            pl.BlockSpec((1, tq, 3 * G), lambda b, g, i, j: (g, b * (t // tq) + i, 0)),
            pl.BlockSpec(cmp_block, cmp_map(0)), pl.BlockSpec(cmp_block, cmp_map(1)),
            pl.BlockSpec(kv_block, kv_map(0)), pl.BlockSpec(kv_block, kv_map(0)),
            pl.BlockSpec(kv_block, kv_map(1)), pl.BlockSpec(kv_block, kv_map(1)),
            pl.BlockSpec((1, dh), lambda b, g, i, j: (0, 0)),
            pl.BlockSpec((1, G, 1, 1), lambda b, g, i, j: (g, 0, 0, 0)),
        ],
        out_specs=pl.BlockSpec((tq, G * dh), lambda b, g, i, j: (b * (t // tq) + i, g)),
        out_shape=jax.ShapeDtypeStruct((n, hkv * G * dh), jnp.float32),
        scratch_shapes=[pltpu.VMEM((G * tq, dh), jnp.bfloat16), pltpu.VMEM((tq, LANES), jnp.bfloat16),
                        slab, stat, stat, slab, stat, stat, slab],
        compiler_params=pltpu.CompilerParams(
            dimension_semantics=("arbitrary",) * 4, vmem_limit_bytes=VMEM_LIMIT_BYTES),
        name="nsa_attn",
    )(proj, gates, cmp_kv, cmp_kv, kn, vv, kn, vv, q_w.reshape(1, dh), slopes)


def _nsa(proj, batch, q_norm_w, kc_norm_w, ks_norm_w, kw_norm_w, pos_k, pos_v, w_ck1, w_ck2, w_cv1, w_cv2):
    n = proj.shape[0]
    G, hkv = NSA_GROUP, NSA_KV_HEADS
    kn, vv, cf = _kv_prep(proj, ks_norm_w, kw_norm_w, 512)
    half = CMP_STRIDE * NSA_HEAD_DIM
    pos = jnp.stack([pos_k.reshape(2, half), pos_v.reshape(2, half)])
    w1 = jnp.stack([w_ck1, w_cv1]).astype(jnp.bfloat16)
    w2 = jnp.stack([w_ck2, w_cv2]).astype(jnp.bfloat16)
    cmp_kv = _cmp_kv(cf, pos, w1, w2, kc_norm_w, batch)
    gates = proj[:, GATE_COL0:GATE_COL0 + N_GATES].reshape(n, hkv, 3 * G).transpose(1, 0, 2)
    slopes = jnp.asarray(2.0 ** (-8.0 * np.arange(1, NSA_Q_HEADS + 1) / NSA_Q_HEADS), jnp.float32)
    return _nsa_attn(proj, gates, cmp_kv, kn, vv, q_norm_w, slopes.reshape(hkv, G, 1, 1), batch)


BIG_NEG = -3.0e38
PEER_CAND = [(a, b) for a in range(PEER_TOPK) for b in range(PEER_TOPK) if (a + 1) * (b + 1) <= PEER_TOPK]
PEER_CAND_ROWS = -(-len(PEER_CAND) // 8) * 8


def _top_rows(s, k):
    rows = s.shape[0]
    ridx = lax.broadcasted_iota(jnp.int32, s.shape, 0)
    out = []
    cur = s
    for _ in range(k):
        mk = jnp.max(cur, axis=0, keepdims=True)
        first = jnp.min(jnp.where(cur == mk, ridx, rows), axis=0, keepdims=True)
        out.append(mk)
        cur = jnp.where(ridx == first, BIG_NEG, cur)
    return out


def _peer_route_body(h_ref, g_ref, wq_ref, keys_ref, xt_ref, s2_ref, a2_ref, s1_ref, r_ref, tau_ref, qt_ref, cand_ref):
    x = h_ref[...]
    hn = x * lax.rsqrt(jnp.mean(x * x, axis=-1, keepdims=True) + NORM_EPS) * g_ref[...]
    hnt = hn.T.astype(jnp.bfloat16)
    xt_ref[...] = hnt
    qt_ref[...] = jnp.dot(wq_ref[...], hnt, preferred_element_type=jnp.float32).astype(jnp.bfloat16)
    cand_ref[...] = jnp.full(cand_ref.shape, BIG_NEG, jnp.float32)
    kd = PEER_KEY_DIM // 2

    def head(hd, carry):
        q1 = qt_ref[pl.ds(pl.multiple_of(hd * 2 * kd, kd), kd), :]
        q2 = qt_ref[pl.ds(pl.multiple_of(hd * 2 * kd + kd, kd), kd), :]
        s1 = jnp.dot(keys_ref[2 * hd], q1, preferred_element_type=jnp.float32)
        s2 = jnp.dot(keys_ref[2 * hd + 1], q2, preferred_element_type=jnp.float32)
        t1 = _top_rows(s1, PEER_TOPK)
        t2 = _top_rows(s2, PEER_TOPK)
        cmax = t1[0] + t2[0]
        cands = [t1[a] + t2[b] for a, b in PEER_CAND]
        for i, c in enumerate(cands):
            cand_ref[i:i + 1, :] = c
        call = cand_ref[...]
        n_gt = jnp.zeros(call.shape, jnp.float32)
        for c in cands:
            n_gt = n_gt + jnp.where(c > call, 1.0, 0.0)
        tau = jnp.min(jnp.where(n_gt <= PEER_TOPK - 1, call, -BIG_NEG), axis=0, keepdims=True)
        z = jnp.sum(jnp.where(call >= tau, jnp.exp(call - cmax), 0.0), axis=0, keepdims=True)
        s2_ref[hd] = s2
        a2_ref[hd] = jnp.exp(s2 - t2[0])
        s1_ref[hd] = s1
        tau_ref[hd] = tau
        r_ref[hd] = jnp.exp(s1 - t1[0]) / z
        return carry

    lax.fori_loop(0, PEER_HEADS, head, 0)


def _peer_route(h, gain, wq_t, keys, tt):
    n, d = h.shape
    hp, nk, kd = keys.shape
    stat = jax.ShapeDtypeStruct((PEER_HEADS, nk, n), jnp.float32)
    stat_spec = pl.BlockSpec((PEER_HEADS, nk, tt), lambda i: (0, 0, i))
    return pl.pallas_call(
        _peer_route_body,
        grid=(n // tt,),
        in_specs=[
            pl.BlockSpec((tt, d), lambda i: (i, 0)),
            pl.BlockSpec((1, d), lambda i: (0, 0)),
            pl.BlockSpec(wq_t.shape, lambda i: (0, 0)),
            pl.BlockSpec(keys.shape, lambda i: (0, 0, 0)),
        ],
        out_specs=[pl.BlockSpec((d, tt), lambda i: (0, i)), stat_spec, stat_spec, stat_spec, stat_spec,
                   pl.BlockSpec((PEER_HEADS, 1, tt), lambda i: (0, 0, i))],
        out_shape=[jax.ShapeDtypeStruct((d, n), jnp.bfloat16), stat, stat, stat, stat,
                   jax.ShapeDtypeStruct((PEER_HEADS, 1, n), jnp.float32)],
        scratch_shapes=[pltpu.VMEM((wq_t.shape[0], tt), jnp.bfloat16),
                        pltpu.VMEM((PEER_CAND_ROWS, tt), jnp.float32)],
        compiler_params=pltpu.CompilerParams(
            dimension_semantics=("arbitrary",), vmem_limit_bytes=VMEM_LIMIT_BYTES),
        name="peer_route",
    )(h, gain.reshape(1, d), wq_t, keys)


def _peer_experts_body(xt_ref, u_ref, vt_ref, s2_ref, a2_ref, s1_ref, r_ref, tau_ref, o_ref, *, n_i1):
    @pl.when(pl.program_id(1) == 0)
    def _():
        o_ref[...] = jnp.zeros(o_ref.shape, jnp.float32)

    act = _gelu(jnp.dot(u_ref[...], xt_ref[...], preferred_element_type=jnp.float32))
    parts = []
    for i1 in range(n_i1):
        w = None
        for hd in range(PEER_HEADS):
            picked = s1_ref[hd, i1:i1 + 1, :] + s2_ref[hd] >= tau_ref[hd]
            term = jnp.where(picked, a2_ref[hd], 0.0) * r_ref[hd, i1:i1 + 1, :]
            w = term if w is None else w + term
        parts.append((w * act[i1 * N_KEYS:(i1 + 1) * N_KEYS]).astype(jnp.bfloat16))
    p = jnp.concatenate(parts, axis=0)
    o_ref[...] += jnp.dot(vt_ref[...], p, preferred_element_type=jnp.float32)


def _peer_experts(xt, u, vt, s2, a2, s1, r, tau, tt, te):
    d, n = xt.shape
    e = u.shape[0]
    n_i1 = te // N_KEYS
    full = pl.BlockSpec((PEER_HEADS, N_KEYS, tt), lambda i, j: (0, 0, i))
    part = pl.BlockSpec((PEER_HEADS, n_i1, tt), lambda i, j: (0, j, i))
    per_token = pl.BlockSpec((PEER_HEADS, 1, tt), lambda i, j: (0, 0, i))
    return pl.pallas_call(
        functools.partial(_peer_experts_body, n_i1=n_i1),
        grid=(n // tt, e // te),
        in_specs=[
            pl.BlockSpec((d, tt), lambda i, j: (0, i)),
            pl.BlockSpec((te, d), lambda i, j: (j, 0)),
            pl.BlockSpec((d, te), lambda i, j: (0, j)),
            full, full, part, part, per_token,
        ],
        out_specs=pl.BlockSpec((d, tt), lambda i, j: (0, i)),
        out_shape=jax.ShapeDtypeStruct((d, n), jnp.float32),
        compiler_params=pltpu.CompilerParams(
            dimension_semantics=("arbitrary", "arbitrary"), vmem_limit_bytes=VMEM_LIMIT_BYTES),
        name="peer_experts",
    )(xt, u, vt, s2, a2, s1, r, tau)


def _peer(h, gain, w_q, sub_keys, u_tab, v_tab):
    keys = sub_keys.reshape(PEER_HEADS * 2, N_KEYS, PEER_KEY_DIM // 2).astype(jnp.bfloat16)
    xt, s2, a2, s1, r, tau = _peer_route(h, gain, w_q.T.astype(jnp.bfloat16), keys, 256)
    out_t = _peer_experts(xt, u_tab.astype(jnp.bfloat16), v_tab.T.astype(jnp.bfloat16),
                          s2, a2, s1, r, tau, 512, 1024)
    return out_t.T


def kernel(x, norm1_w, w_in, hg_lb_logits, hg_norm_w, q_norm_w, kc_norm_w, ks_norm_w, kw_norm_w,
           cmp_pos_k, cmp_pos_v, w_ck1, w_ck2, w_cv1, w_cv2, w_out, norm2_w,
           peer_w_q, peer_sub_keys, peer_u, peer_v):
    B, T, D = x.shape
    n = B * T
    layer = 0
    lower_bounds = jnp.cumsum(jax.nn.softmax(hg_lb_logits, axis=0), axis=0)
    xt = x.reshape(n, D)

    w_in_b = jnp.pad(w_in[layer].astype(jnp.bfloat16), ((0, 0), (0, IN_COLS_PADDED - IN_COLS)))
    proj = _norm_matmul(xt, norm1_w[layer], w_in_b, 512, 512)
    hg_out = _hgrn2(proj, lower_bounds[layer], hg_norm_w[layer], B)
    nsa_out = _nsa(proj, B, q_norm_w[layer], kc_norm_w[layer], ks_norm_w[layer], kw_norm_w[layer],
                   cmp_pos_k[layer], cmp_pos_v[layer], w_ck1[layer], w_ck2[layer], w_cv1[layer], w_cv2[layer])
    h = _out_proj(hg_out, nsa_out, w_out[layer].astype(jnp.bfloat16), xt, 512, 512)

    y = h + _peer(h, norm2_w[layer], peer_w_q[layer], peer_sub_keys[layer], peer_u[layer], peer_v[layer])
    return y.reshape(B, T, D)
```

```python
import functools

import jax
import jax.numpy as jnp
import numpy as np
from jax import lax
from jax.experimental import pallas as pl
from jax.experimental.pallas import tpu as pltpu

D_MODEL = 2048
HG_WIDTH = 1024
HG_HEAD_DIM = 128
HG_HEADS = 8
HG_CHUNK = 64
NSA_WIDTH = 1024
NSA_HEAD_DIM = 64
NSA_Q_HEADS = 16
NSA_KV_HEADS = 4
NSA_GROUP = 4
CMP_BLOCK = 32
CMP_STRIDE = 16
CMP_HIDDEN = 256
SEL_BLOCK = 64
N_SELECT = 16
N_LOCAL = 2
WINDOW = 512
FORCE_SCORE = 1e9
NEG_INF = -1e30
PEER_HEADS = 8
N_KEYS = 128
PEER_KEY_DIM = 256
PEER_TOPK = 16
NORM_EPS = 1e-6
KV_W = NSA_KV_HEADS * NSA_HEAD_DIM
N_GATES = 3 * NSA_Q_HEADS
IN_SIZES = [HG_WIDTH] * 4 + [NSA_WIDTH] + [KV_W] * 6 + [N_GATES]
IN_COLS = sum(IN_SIZES)
IN_COLS_PADDED = 7168
Q_COL0 = 4 * HG_WIDTH
KV_COL0 = Q_COL0 + NSA_WIDTH
GATE_COL0 = KV_COL0 + 6 * KV_W

VMEM_LIMIT_BYTES = 48 * 1024 * 1024
LANES = 128


def _norm_matmul_body(x_ref, g_ref, w_ref, o_ref, xn_ref):
    @pl.when(pl.program_id(1) == 0)
    def _():
        x = x_ref[...]
        r = lax.rsqrt(jnp.mean(x * x, axis=-1, keepdims=True) + NORM_EPS)
        xn_ref[...] = (x * r * g_ref[...]).astype(jnp.bfloat16)

    o_ref[...] = jnp.dot(xn_ref[...], w_ref[...], preferred_element_type=jnp.float32).astype(o_ref.dtype)


def _norm_matmul(x, gain, w, tm, tn, out_dtype=jnp.float32):
    m, k = x.shape
    n = w.shape[1]
    return pl.pallas_call(
        _norm_matmul_body,
        grid=(m // tm, n // tn),
        in_specs=[
            pl.BlockSpec((tm, k), lambda i, j: (i, 0)),
            pl.BlockSpec((1, k), lambda i, j: (0, 0)),
            pl.BlockSpec((k, tn), lambda i, j: (0, j)),
        ],
        out_specs=pl.BlockSpec((tm, tn), lambda i, j: (i, j)),
        out_shape=jax.ShapeDtypeStruct((m, n), out_dtype),
        scratch_shapes=[pltpu.VMEM((tm, k), jnp.bfloat16)],
        compiler_params=pltpu.CompilerParams(
            dimension_semantics=("arbitrary", "arbitrary"), vmem_limit_bytes=VMEM_LIMIT_BYTES),
        name="norm_matmul",
    )(x, gain.reshape(1, k), w)


def _out_proj_body(a1_ref, a2_ref, w1_ref, w2_ref, r_ref, o_ref):
    acc = jnp.dot(a1_ref[...].astype(jnp.bfloat16), w1_ref[...], preferred_element_type=jnp.float32)
    acc = acc + jnp.dot(a2_ref[...].astype(jnp.bfloat16), w2_ref[...], preferred_element_type=jnp.float32)
    o_ref[...] = r_ref[...] + acc


def _out_proj(a1, a2, w, res, tm, tn):
    m, k1 = a1.shape
    k2 = a2.shape[1]
    n = w.shape[1]
    return pl.pallas_call(
        _out_proj_body,
        grid=(m // tm, n // tn),
        in_specs=[
            pl.BlockSpec((tm, k1), lambda i, j: (i, 0)),
            pl.BlockSpec((tm, k2), lambda i, j: (i, 0)),
            pl.BlockSpec((k1, tn), lambda i, j: (0, j)),
            pl.BlockSpec((k2, tn), lambda i, j: (k1 // k2, j)),
            pl.BlockSpec((tm, tn), lambda i, j: (i, j)),
        ],
        out_specs=pl.BlockSpec((tm, tn), lambda i, j: (i, j)),
        out_shape=jax.ShapeDtypeStruct((m, n), jnp.float32),
        compiler_params=pltpu.CompilerParams(
            dimension_semantics=("arbitrary", "arbitrary"), vmem_limit_bytes=VMEM_LIMIT_BYTES),
        name="out_proj",
    )(a1, a2, w, w, res)


def _hgrn2_body(q_ref, f_ref, v_ref, g_ref, lb_ref, nw_ref, o_ref, qd_ref, kd_ref, ku_ref, vb_ref, dec_ref, sp_ref):
    t, dk = q_ref.shape
    C = HG_CHUNK
    nc = t // C
    lb = lb_ref[...]
    f = lb + (1.0 - lb) * jax.nn.sigmoid(f_ref[...])
    kf = 1.0 - f
    b = jnp.log(f)
    row = lax.broadcasted_iota(jnp.int32, (t, dk), 0) % C
    shift = 1
    while shift < C:
        b = b + jnp.where(row >= shift, pltpu.roll(b, shift, 0), 0.0)
        shift *= 2
    b3 = b.reshape(nc, C, dk)
    b_end = b3[:, C - 1:C, :]
    qd_ref[...] = (q_ref[...] * jnp.exp(b)).astype(jnp.bfloat16)
    kd_ref[...] = (kf * jnp.exp(-b)).astype(jnp.bfloat16)
    ku_ref[...] = (kf.reshape(nc, C, dk) * jnp.exp(b_end - b3)).reshape(t, dk).astype(jnp.bfloat16)
    vb_ref[...] = v_ref[...].astype(jnp.bfloat16)
    dec_ref[...] = jnp.exp(b_end)

    st = jnp.zeros((dk, dk), jnp.float32)
    for n in range(nc):
        rows = slice(n * C, (n + 1) * C)
        sp_ref[n] = st.astype(jnp.bfloat16)
        upd_t = lax.dot_general(vb_ref[rows, :], ku_ref[rows, :], (((0,), (0,)), ((), ())),
                                preferred_element_type=jnp.float32)
        st = dec_ref[n] * st + upd_t

    causal = lax.broadcasted_iota(jnp.int32, (C, C), 0) >= lax.broadcasted_iota(jnp.int32, (C, C), 1)
    for n in range(nc):
        rows = slice(n * C, (n + 1) * C)
        qd = qd_ref[rows, :]
        attn = lax.dot_general(qd, kd_ref[rows, :], (((1,), (1,)), ((), ())), preferred_element_type=jnp.float32)
        attn = jnp.where(causal, attn, 0.0).astype(jnp.bfloat16)
        o = jnp.dot(attn, vb_ref[rows, :], preferred_element_type=jnp.float32)
        o = o + lax.dot_general(qd, sp_ref[n], (((1,), (1,)), ((), ())), preferred_element_type=jnp.float32)
        o = o * lax.rsqrt(jnp.mean(o * o, axis=-1, keepdims=True) + NORM_EPS) * nw_ref[...]
        o_ref[rows, :] = o * jax.nn.silu(g_ref[rows, :])


def _hgrn2(proj, lb, norm_w, batch):
    n = proj.shape[0]
    t = n // batch
    dk, H = HG_HEAD_DIM, HG_HEADS
    nc = t // HG_CHUNK

    def part(p):
        return pl.BlockSpec((t, dk), lambda b, h: (b, p * H + h))

    slab = pltpu.VMEM((t, dk), jnp.bfloat16)
    return pl.pallas_call(
        _hgrn2_body,
        grid=(batch, H),
        in_specs=[part(0), part(1), part(2), part(3),
                  pl.BlockSpec((1, dk), lambda b, h: (0, h)),
                  pl.BlockSpec((1, dk), lambda b, h: (0, 0))],
        out_specs=pl.BlockSpec((t, dk), lambda b, h: (b, h)),
        out_shape=jax.ShapeDtypeStruct((n, H * dk), jnp.float32),
        scratch_shapes=[slab, slab, slab, slab, pltpu.VMEM((nc, 1, dk), jnp.float32),
                        pltpu.VMEM((nc, dk, dk), jnp.bfloat16)],
        compiler_params=pltpu.CompilerParams(
            dimension_semantics=("arbitrary", "arbitrary"), vmem_limit_bytes=VMEM_LIMIT_BYTES),
        name="hgrn2",
    )(proj, proj, proj, proj, lb.reshape(1, H * dk), norm_w.reshape(1, dk))


GELU_C0 = float(np.sqrt(2.0 / np.pi))
GELU_C1 = GELU_C0 * 0.044715


def _gelu(x):
    half = 0.5 * x
    return half + half * jnp.tanh(x * (GELU_C0 + GELU_C1 * (x * x)))


def _head_norm(x, w):
    return x * lax.rsqrt(jnp.mean(x * x, axis=-1, keepdims=True) + NORM_EPS) * w


def _kv_prep_body(c_ref, s_ref, w_ref, ksw_ref, kww_ref, kn_ref, vv_ref, cf_ref):
    dh = NSA_HEAD_DIM
    for h in range(NSA_KV_HEADS):
        k_cols = slice(h * dh, (h + 1) * dh)
        v_cols = slice(KV_W + h * dh, KV_W + (h + 1) * dh)
        cf_ref[0, h] = c_ref[:, k_cols]
        cf_ref[1, h] = c_ref[:, v_cols]
        kn_ref[0, h] = _head_norm(s_ref[:, k_cols], ksw_ref[...]).astype(jnp.bfloat16)
        vv_ref[0, h] = s_ref[:, v_cols].astype(jnp.bfloat16)
        kn_ref[1, h] = _head_norm(w_ref[:, k_cols], kww_ref[...]).astype(jnp.bfloat16)
        vv_ref[1, h] = w_ref[:, v_cols].astype(jnp.bfloat16)


def _kv_prep(proj, ks_w, kw_w, tm):
    n = proj.shape[0]
    dh = NSA_HEAD_DIM
    pair = 2 * KV_W
    col0 = KV_COL0 // pair
    head_block = pl.BlockSpec((2, NSA_KV_HEADS, tm, dh), lambda i: (0, 0, i, 0))
    w_spec = pl.BlockSpec((1, dh), lambda i: (0, 0))
    return pl.pallas_call(
        _kv_prep_body,
        grid=(n // tm,),
        in_specs=[pl.BlockSpec((tm, pair), lambda i: (i, col0)),
                  pl.BlockSpec((tm, pair), lambda i: (i, col0 + 1)),
                  pl.BlockSpec((tm, pair), lambda i: (i, col0 + 2)),
                  w_spec, w_spec],
        out_specs=[head_block, head_block, head_block],
        out_shape=[jax.ShapeDtypeStruct((2, NSA_KV_HEADS, n, dh), jnp.bfloat16),
                   jax.ShapeDtypeStruct((2, NSA_KV_HEADS, n, dh), jnp.bfloat16),
                   jax.ShapeDtypeStruct((2, NSA_KV_HEADS, n, dh), jnp.float32)],
        compiler_params=pltpu.CompilerParams(
            dimension_semantics=("arbitrary",), vmem_limit_bytes=VMEM_LIMIT_BYTES),
        name="kv_prep",
    )(proj, proj, proj, ks_w.reshape(1, dh), kw_w.reshape(1, dh))


def _cmp_kv_body(r_ref, pos_ref, w1_ref, w2_ref, nw_ref, o_ref):
    half = CMP_STRIDE * NSA_HEAD_DIM
    n_strips = r_ref.shape[2]
    for kind in range(2):
        strips = r_ref[kind, 0]
        top = (strips + pos_ref[kind, 0:1, :]).astype(jnp.bfloat16)
        bot = (strips + pos_ref[kind, 1:2, :]).astype(jnp.bfloat16)
        a = jnp.dot(top, w1_ref[kind, :half, :], preferred_element_type=jnp.float32)
        b = jnp.dot(bot, w1_ref[kind, half:, :], preferred_element_type=jnp.float32)
        hid = _gelu(a + pltpu.roll(b, n_strips - 1, 0))
        out = jnp.dot(hid.astype(jnp.bfloat16), w2_ref[kind], preferred_element_type=jnp.float32)
        if kind == 0:
            out = _head_norm(out, nw_ref[...])
        o_ref[kind, 0, 0] = out.astype(jnp.bfloat16)


def _cmp_kv(cf, pos, w1, w2, kc_w, batch):
    _, hkv, n, dh = cf.shape
    t = n // batch
    n_strips = t // CMP_STRIDE
    strips = cf.reshape(2, hkv, n // CMP_STRIDE, CMP_STRIDE * dh)
    return pl.pallas_call(
        _cmp_kv_body,
        grid=(batch, hkv),
        in_specs=[pl.BlockSpec((2, 1, n_strips, CMP_STRIDE * dh), lambda b, g: (0, g, b, 0)),
                  pl.BlockSpec(pos.shape, lambda b, g: (0, 0, 0)),
                  pl.BlockSpec(w1.shape, lambda b, g: (0, 0, 0)),
                  pl.BlockSpec(w2.shape, lambda b, g: (0, 0, 0)),
                  pl.BlockSpec((1, dh), lambda b, g: (0, 0))],
        out_specs=pl.BlockSpec((2, 1, 1, n_strips, dh), lambda b, g: (0, g, b, 0, 0)),
        out_shape=jax.ShapeDtypeStruct((2, hkv, batch, n_strips, dh), jnp.bfloat16),
        compiler_params=pltpu.CompilerParams(
            dimension_semantics=("arbitrary", "arbitrary"), vmem_limit_bytes=VMEM_LIMIT_BYTES),
        name="cmp_kv",
    )(strips, pos, w1, w2, kc_w.reshape(1, dh))


def _flash_step(qs_ref, k, v, slope_ref, m_ref, l_ref, acc_ref, base, mask_fn, tq):
    tk = k.shape[0]
    g = NSA_GROUP
    dist = base + lax.broadcasted_iota(jnp.int32, (tq, tk), 0) - lax.broadcasted_iota(jnp.int32, (tq, tk), 1)
    valid = mask_fn(dist)[None]
    s = lax.dot_general(qs_ref[...], k, (((1,), (1,)), ((), ())),
                        preferred_element_type=jnp.float32).reshape(g, tq, tk)
    s = jnp.where(valid, s - slope_ref[0] * dist.astype(jnp.float32)[None], NEG_INF)
    m_old = m_ref[...].reshape(g, tq, 1)
    m_new = jnp.maximum(m_old, jnp.max(s, axis=-1, keepdims=True))
    p = jnp.where(valid, jnp.exp(s - m_new), 0.0)
    alpha = jnp.exp(m_old - m_new)
    l_ref[...] = (alpha * l_ref[...].reshape(g, tq, 1) + jnp.sum(p, axis=-1, keepdims=True)).reshape(g * tq, 1)
    pv = jnp.dot(p.reshape(g * tq, tk).astype(jnp.bfloat16), v, preferred_element_type=jnp.float32)
    acc_ref[...] = alpha.reshape(g * tq, 1) * acc_ref[...] + pv
    m_ref[...] = m_new.reshape(g * tq, 1)


def _nsa_attn_body(q_ref, gate_ref, kc_ref, vc_ref, ks_ref, vs_ref, kw_ref, vw_ref, qw_ref, slope_ref, o_ref,
                   qs_ref, sel_ref, oc_ref, ms_ref, ls_ref, as_ref, mw_ref, lw_ref, aw_ref, *, tq, tk, n_sel):
    i = pl.program_id(2)
    j = pl.program_id(3)
    G, dh = NSA_GROUP, NSA_HEAD_DIM
    last_j = (i * tq + tq - 1) // tk

    @pl.when(j == 0)
    def _():
        x = q_ref[...]
        for r in range(G):
            qn = _head_norm(x[:, r * dh:(r + 1) * dh], qw_ref[...]) * (dh ** -0.5)
            qs_ref[r * tq:(r + 1) * tq, :] = qn.astype(jnp.bfloat16)
        for ref in (ms_ref, mw_ref):
            ref[...] = jnp.full(ref.shape, NEG_INF, jnp.float32)
        for ref in (ls_ref, lw_ref, as_ref, aw_ref):
            ref[...] = jnp.zeros(ref.shape, jnp.float32)

        n_cmp_pad = kc_ref.shape[-2]
        t_pos = i * tq + lax.broadcasted_iota(jnp.int32, (tq, n_cmp_pad), 0)
        blk = lax.broadcasted_iota(jnp.int32, (tq, n_cmp_pad), 1)
        dist = t_pos - (blk * CMP_STRIDE + CMP_BLOCK - 1)
        valid = (dist >= 0) & (blk < n_cmp_pad - 1)
        distf = dist.astype(jnp.float32)
        cb = lax.broadcasted_iota(jnp.int32, (n_sel, n_cmp_pad), 1) * CMP_STRIDE
        sb = lax.broadcasted_iota(jnp.int32, (n_sel, n_cmp_pad), 0) * SEL_BLOCK
        overlap_t = ((cb < sb + SEL_BLOCK) & (cb + CMP_BLOCK > sb)).astype(jnp.bfloat16)
        imp = None
        for r in range(G):
            rows = slice(r * tq, (r + 1) * tq)
            s = lax.dot_general(qs_ref[rows, :], kc_ref[0, 0, 0], (((1,), (1,)), ((), ())),
                                preferred_element_type=jnp.float32)
            s = jnp.where(valid, s - slope_ref[0, r] * distf, NEG_INF)
            p = jnp.where(valid, jnp.exp(s - jnp.max(s, axis=-1, keepdims=True)), 0.0)
            denom = jnp.maximum(jnp.sum(p, axis=-1, keepdims=True), 1e-30)
            p = (p / denom).astype(jnp.bfloat16)
            oc_ref[rows, :] = jnp.dot(p, vc_ref[0, 0, 0], preferred_element_type=jnp.float32)
            part = lax.dot_general(overlap_t, p, (((1,), (1,)), ((), ())), preferred_element_type=jnp.float32)
            imp = part if imp is None else imp + part

        cur = (i * tq + lax.broadcasted_iota(jnp.int32, (n_sel, tq), 1)) // SEL_BLOCK
        jb = lax.broadcasted_iota(jnp.int32, (n_sel, tq), 0)
        forced = (jb == 0) | ((jb <= cur) & (jb > cur - N_LOCAL))
        imp = jnp.where(forced, FORCE_SCORE, jnp.where(jb > cur, -FORCE_SCORE, imp))
        rank = jnp.zeros((n_sel, tq), jnp.float32)
        for c in range(n_sel):
            row = imp[c:c + 1, :]
            ahead = (row > imp) | ((row == imp) & (jb > c))
            rank = rank + jnp.where(ahead, 1.0, 0.0)
        sel_t = jnp.where(rank < min(N_SELECT, n_sel), 1.0, 0.0)
        sel_t = jnp.concatenate([sel_t, jnp.zeros((sel_ref.shape[1] - n_sel, tq), jnp.float32)], axis=0)
        sel_ref[...] = sel_t.T.astype(jnp.bfloat16)

    @pl.when(j <= last_j)
    def _():
        n_pad = sel_ref.shape[1]
        blk = lax.broadcasted_iota(jnp.int32, (n_pad, tk), 0)
        col_blk = j * (tk // SEL_BLOCK) + lax.broadcasted_iota(jnp.int32, (n_pad, tk), 1) // SEL_BLOCK
        expand = (blk == col_blk).astype(jnp.bfloat16)

        def mask(dist):
            picked = jnp.dot(sel_ref[...], expand, preferred_element_type=jnp.float32)
            return (dist >= 0) & (picked > 0.5)

        _flash_step(qs_ref, ks_ref[0, 0], vs_ref[0, 0], slope_ref, ms_ref, ls_ref, as_ref,
                    i * tq - j * tk, mask, tq)

    @pl.when((j <= last_j) & (j * tk + tk - 1 >= i * tq - (WINDOW - 1)))
    def _():
        _flash_step(qs_ref, kw_ref[0, 0], vw_ref[0, 0], slope_ref, mw_ref, lw_ref, aw_ref,
                    i * tq - j * tk, lambda dist: (dist >= 0) & (dist < WINDOW), tq)

    @pl.when(j == pl.num_programs(3) - 1)
    def _():
        gates = jax.nn.sigmoid(gate_ref[0])
        o_sel = as_ref[...] / ls_ref[...]
        o_win = aw_ref[...] / lw_ref[...]
        o_cmp = oc_ref[...]
        outs = []
        for r in range(G):
            rows = slice(r * tq, (r + 1) * tq)
            outs.append(gates[:, 3 * r:3 * r + 1] * o_cmp[rows] + gates[:, 3 * r + 1:3 * r + 2] * o_sel[rows]
                        + gates[:, 3 * r + 2:3 * r + 3] * o_win[rows])
        o_ref[...] = jnp.concatenate(outs, axis=-1)


def _nsa_attn(proj, gates, cmp_kv, kn, vv, q_w, slopes, batch, tq=512, tk=512):
    n = proj.shape[0]
    t = n // batch
    G, dh, hkv = NSA_GROUP, NSA_HEAD_DIM, NSA_KV_HEADS
    n_cmp_pad = cmp_kv.shape[-2]
    n_sel = t // SEL_BLOCK
    q_blk0 = Q_COL0 // (G * dh)

    def kv_map(kind):
        def index(b, g, i, j):
            return (kind, g, b * (t // tk) + jnp.minimum(j, (i * tq + tq - 1) // tk), 0)
        return index

    def cmp_map(kind):
        return lambda b, g, i, j: (kind, g, b, 0, 0)

    kv_block = (1, 1, tk, dh)
    cmp_block = (1, 1, 1, n_cmp_pad, dh)
    slab = pltpu.VMEM((G * tq, dh), jnp.float32)
    stat = pltpu.VMEM((G * tq, 1), jnp.float32)
    return pl.pallas_call(
        functools.partial(_nsa_attn_body, tq=tq, tk=tk, n_sel=n_sel),
        grid=(batch, hkv, t // tq, t // tk),
        in_specs=[
            pl.BlockSpec((tq, G * dh), lambda b, g, i, j: (b * (t // tq) + i, q_blk0 + g)),
            pl.BlockSpec((1, tq, 3 * G), lambda b, g, i, j: (g, b * (t // tq) + i, 0)),
            pl.BlockSpec(cmp_block, cmp_map(0)), pl.BlockSpec(cmp_block, cmp_map(1)),
            pl.BlockSpec(kv_block, kv_map(0)), pl.BlockSpec(kv_block, kv_map(0)),
            pl.BlockSpec(kv_block, kv_map(1)), pl.BlockSpec(kv_block, kv_map(1)),
            pl.BlockSpec((1, dh), lambda b, g, i, j: (0, 0)),
            pl.BlockSpec((1, G, 1, 1), lambda b, g, i, j: (g, 0, 0, 0)),
        ],
        out_specs=pl.BlockSpec((tq, G * dh), lambda b, g, i, j: (b * (t // tq) + i, g)),
        out_shape=jax.ShapeDtypeStruct((n, hkv * G * dh), jnp.float32),
        scratch_shapes=[pltpu.VMEM((G * tq, dh), jnp.bfloat16), pltpu.VMEM((tq, LANES), jnp.bfloat16),
                        slab, stat, stat, slab, stat, stat, slab],
        compiler_params=pltpu.CompilerParams(
            dimension_semantics=("arbitrary",) * 4, vmem_limit_bytes=VMEM_LIMIT_BYTES),
        name="nsa_attn",
    )(proj, gates, cmp_kv, cmp_kv, kn, vv, kn, vv, q_w.reshape(1, dh), slopes)


def _nsa(proj, batch, q_norm_w, kc_norm_w, ks_norm_w, kw_norm_w, pos_k, pos_v, w_ck1, w_ck2, w_cv1, w_cv2):
    n = proj.shape[0]
    G, hkv = NSA_GROUP, NSA_KV_HEADS
    kn, vv, cf = _kv_prep(proj, ks_norm_w, kw_norm_w, 512)
    half = CMP_STRIDE * NSA_HEAD_DIM
    pos = jnp.stack([pos_k.reshape(2, half), pos_v.reshape(2, half)])
    w1 = jnp.stack([w_ck1, w_cv1]).astype(jnp.bfloat16)
    w2 = jnp.stack([w_ck2, w_cv2]).astype(jnp.bfloat16)
    cmp_kv = _cmp_kv(cf, pos, w1, w2, kc_norm_w, batch)
    gates = proj[:, GATE_COL0:GATE_COL0 + N_GATES].reshape(n, hkv, 3 * G).transpose(1, 0, 2)
    slopes = jnp.asarray(2.0 ** (-8.0 * np.arange(1, NSA_Q_HEADS + 1) / NSA_Q_HEADS), jnp.float32)
    return _nsa_attn(proj, gates, cmp_kv, kn, vv, q_norm_w, slopes.reshape(hkv, G, 1, 1), batch)


BIG_NEG = -3.0e38
PEER_CAND = [(a, b) for a in range(PEER_TOPK) for b in range(PEER_TOPK) if (a + 1) * (b + 1) <= PEER_TOPK]
PEER_CAND_ROWS = -(-len(PEER_CAND) // 8) * 8


def _peer_route_body(h_ref, g_ref, wq_ref, keys_ref, xt_ref, s2_ref, a2_ref, thr_ref, r_ref,
                     qt_ref, sc_ref, cur_ref, top_ref, cand_ref):
    x = h_ref[...]
    hn = x * lax.rsqrt(jnp.mean(x * x, axis=-1, keepdims=True) + NORM_EPS) * g_ref[...]
    hnt = hn.T.astype(jnp.bfloat16)
    xt_ref[...] = hnt
    qt_ref[...] = jnp.dot(wq_ref[...], hnt, preferred_element_type=jnp.float32).astype(jnp.bfloat16)
    cand_ref[...] = jnp.full(cand_ref.shape, BIG_NEG, jnp.float32)
    kd = PEER_KEY_DIM // 2
    n_parts = 2 * PEER_HEADS
    for hp in range(n_parts):
        s = jnp.dot(keys_ref[hp], qt_ref[hp * kd:(hp + 1) * kd, :], preferred_element_type=jnp.float32)
        sc_ref[hp] = s
        cur_ref[hp] = s

    ridx = lax.broadcasted_iota(jnp.int32, cur_ref.shape, 1)

    def extract(k, carry):
        cur = cur_ref[...]
        mk = jnp.max(cur, axis=1, keepdims=True)
        first = jnp.min(jnp.where(cur == mk, ridx, N_KEYS), axis=1, keepdims=True)
        top_ref[k] = mk
        cur_ref[...] = jnp.where(ridx == first, BIG_NEG, cur)
        return carry

    lax.fori_loop(0, PEER_TOPK, extract, 0)

    def head(hd, carry):
        t1 = [top_ref[a, 2 * hd] for a in range(PEER_TOPK)]
        t2 = [top_ref[b, 2 * hd + 1] for b in range(PEER_TOPK)]
        cmax = t1[0] + t2[0]
        cands = [t1[a] + t2[b] for a, b in PEER_CAND]
        for i, c in enumerate(cands):
            cand_ref[i:i + 1, :] = c
        call = cand_ref[...]
        n_gt = jnp.zeros(call.shape, jnp.float32)
        for c in cands:
            n_gt = n_gt + jnp.where(c > call, 1.0, 0.0)
        tau = jnp.min(jnp.where(n_gt <= PEER_TOPK - 1, call, -BIG_NEG), axis=0, keepdims=True)
        z = jnp.sum(jnp.where(call >= tau, jnp.exp(call - cmax), 0.0), axis=0, keepdims=True)
        low = [None] * PEER_TOPK
        for (a, b), c in zip(PEER_CAND, cands):
            cut = jnp.where(c >= tau, t2[b], -BIG_NEG)
            low[a] = cut if low[a] is None else jnp.minimum(low[a], cut)
        s1 = sc_ref[2 * hd]
        s2 = sc_ref[2 * hd + 1]
        thr = jnp.full(s1.shape, -BIG_NEG, jnp.float32)
        for a in range(PEER_TOPK):
            thr = jnp.where(s1 == t1[a], low[a], thr)
        s2_ref[hd] = s2
        a2_ref[hd] = jnp.exp(s2 - t2[0])
        thr_ref[hd] = thr
        r_ref[hd] = jnp.exp(s1 - t1[0]) / z
        return carry

    lax.fori_loop(0, PEER_HEADS, head, 0)


def _peer_route(h, gain, wq_t, keys, tt):
    n, d = h.shape
    hp, nk, kd = keys.shape
    stat = jax.ShapeDtypeStruct((PEER_HEADS, nk, n), jnp.float32)
    stat_spec = pl.BlockSpec((PEER_HEADS, nk, tt), lambda i: (0, 0, i))
    return pl.pallas_call(
        _peer_route_body,
        grid=(n // tt,),
        in_specs=[
            pl.BlockSpec((tt, d), lambda i: (i, 0)),
            pl.BlockSpec((1, d), lambda i: (0, 0)),
            pl.BlockSpec(wq_t.shape, lambda i: (0, 0)),
            pl.BlockSpec(keys.shape, lambda i: (0, 0, 0)),
        ],
        out_specs=[pl.BlockSpec((d, tt), lambda i: (0, i)), stat_spec, stat_spec, stat_spec, stat_spec],
        out_shape=[jax.ShapeDtypeStruct((d, n), jnp.bfloat16), stat, stat, stat, stat],
        scratch_shapes=[pltpu.VMEM((wq_t.shape[0], tt), jnp.bfloat16),
                        pltpu.VMEM((hp, nk, tt), jnp.float32), pltpu.VMEM((hp, nk, tt), jnp.float32),
                        pltpu.VMEM((PEER_TOPK, hp, 1, tt), jnp.float32),
                        pltpu.VMEM((PEER_CAND_ROWS, tt), jnp.float32)],
        compiler_params=pltpu.CompilerParams(
            dimension_semantics=("arbitrary",), vmem_limit_bytes=VMEM_LIMIT_BYTES),
        name="peer_route",
    )(h, gain.reshape(1, d), wq_t, keys)


def _peer_experts_body(xt_ref, u_ref, vt_ref, s2_ref, a2_ref, thr_ref, r_ref, o_ref, *, n_i1):
    @pl.when(pl.program_id(1) == 0)
    def _():
        o_ref[...] = jnp.zeros(o_ref.shape, jnp.float32)

    act = _gelu(jnp.dot(u_ref[...], xt_ref[...], preferred_element_type=jnp.float32))
    parts = []
    for i1 in range(n_i1):
        w = None
        for hd in range(PEER_HEADS):
            picked = s2_ref[hd] >= thr_ref[hd, i1:i1 + 1, :]
            term = jnp.where(picked, a2_ref[hd], 0.0) * r_ref[hd, i1:i1 + 1, :]
            w = term if w is None else w + term
        parts.append((w * act[i1 * N_KEYS:(i1 + 1) * N_KEYS]).astype(jnp.bfloat16))
    p = jnp.concatenate(parts, axis=0)
    o_ref[...] += jnp.dot(vt_ref[...], p, preferred_element_type=jnp.float32)


def _peer_experts(xt, u, vt, s2, a2, thr, r, tt, te):
    d, n = xt.shape
    e = u.shape[0]
    n_i1 = te // N_KEYS
    full = pl.BlockSpec((PEER_HEADS, N_KEYS, tt), lambda i, j: (0, 0, i))
    part = pl.BlockSpec((PEER_HEADS, n_i1, tt), lambda i, j: (0, j, i))
    return pl.pallas_call(
        functools.partial(_peer_experts_body, n_i1=n_i1),
        grid=(n // tt, e // te),
        in_specs=[
            pl.BlockSpec((d, tt), lambda i, j: (0, i)),
            pl.BlockSpec((te, d), lambda i, j: (j, 0)),
            pl.BlockSpec((d, te), lambda i, j: (0, j)),
            full, full, part, part,
        ],
        out_specs=pl.BlockSpec((d, tt), lambda i, j: (0, i)),
        out_shape=jax.ShapeDtypeStruct((d, n), jnp.float32),
        compiler_params=pltpu.CompilerParams(
            dimension_semantics=("arbitrary", "arbitrary"), vmem_limit_bytes=VMEM_LIMIT_BYTES),
        name="peer_experts",
    )(xt, u, vt, s2, a2, thr, r)


def _peer(h, gain, w_q, sub_keys, u_tab, v_tab):
    keys = sub_keys.reshape(PEER_HEADS * 2, N_KEYS, PEER_KEY_DIM // 2).astype(jnp.bfloat16)
    xt, s2, a2, thr, r = _peer_route(h, gain, w_q.T.astype(jnp.bfloat16), keys, 256)
    out_t = _peer_experts(xt, u_tab.astype(jnp.bfloat16), v_tab.T.astype(jnp.bfloat16),
                          s2, a2, thr, r, 512, 1024)
    return out_t.T


def kernel(x, norm1_w, w_in, hg_lb_logits, hg_norm_w, q_norm_w, kc_norm_w, ks_norm_w, kw_norm_w,
           cmp_pos_k, cmp_pos_v, w_ck1, w_ck2, w_cv1, w_cv2, w_out, norm2_w,
           peer_w_q, peer_sub_keys, peer_u, peer_v):
    B, T, D = x.shape
    n = B * T
    layer = 0
    lower_bounds = jnp.cumsum(jax.nn.softmax(hg_lb_logits, axis=0), axis=0)
    xt = x.reshape(n, D)

    w_in_b = jnp.pad(w_in[layer].astype(jnp.bfloat16), ((0, 0), (0, IN_COLS_PADDED - IN_COLS)))
    proj = _norm_matmul(xt, norm1_w[layer], w_in_b, 512, 1024)
    hg_out = _hgrn2(proj, lower_bounds[layer], hg_norm_w[layer], B)
    nsa_out = _nsa(proj, B, q_norm_w[layer], kc_norm_w[layer], ks_norm_w[layer], kw_norm_w[layer],
                   cmp_pos_k[layer], cmp_pos_v[layer], w_ck1[layer], w_ck2[layer], w_cv1[layer], w_cv2[layer])
    h = _out_proj(hg_out, nsa_out, w_out[layer].astype(jnp.bfloat16), xt, 512, 1024)

    y = h + _peer(h, norm2_w[layer], peer_w_q[layer], peer_sub_keys[layer], peer_u[layer], peer_v[layer])
    return y.reshape(B, T, D)
```

```python
import functools

import jax
import jax.numpy as jnp
import numpy as np
from jax import lax
from jax.experimental import pallas as pl
from jax.experimental.pallas import tpu as pltpu

D_MODEL = 2048
HG_WIDTH = 1024
HG_HEAD_DIM = 128
HG_HEADS = 8
HG_CHUNK = 64
NSA_WIDTH = 1024
NSA_HEAD_DIM = 64
NSA_Q_HEADS = 16
NSA_KV_HEADS = 4
NSA_GROUP = 4
CMP_BLOCK = 32
CMP_STRIDE = 16
CMP_HIDDEN = 256
SEL_BLOCK = 64
N_SELECT = 16
N_LOCAL = 2
WINDOW = 512
FORCE_SCORE = 1e9
NEG_INF = -1e30
PEER_HEADS = 8
N_KEYS = 128
PEER_KEY_DIM = 256
PEER_TOPK = 16
NORM_EPS = 1e-6
KV_W = NSA_KV_HEADS * NSA_HEAD_DIM
N_GATES = 3 * NSA_Q_HEADS
IN_SIZES = [HG_WIDTH] * 4 + [NSA_WIDTH] + [KV_W] * 6 + [N_GATES]
IN_COLS = sum(IN_SIZES)
IN_COLS_PADDED = 7168
Q_COL0 = 4 * HG_WIDTH
KV_COL0 = Q_COL0 + NSA_WIDTH
GATE_COL0 = KV_COL0 + 6 * KV_W

VMEM_LIMIT_BYTES = 48 * 1024 * 1024
LANES = 128


def _norm_matmul_body(x_ref, g_ref, w_ref, o_ref, xn_ref):
    @pl.when(pl.program_id(1) == 0)
    def _():
        x = x_ref[...]
        r = lax.rsqrt(jnp.mean(x * x, axis=-1, keepdims=True) + NORM_EPS)
        xn_ref[...] = (x * r * g_ref[...]).astype(jnp.bfloat16)

    o_ref[...] = jnp.dot(xn_ref[...], w_ref[...], preferred_element_type=jnp.float32).astype(o_ref.dtype)


def _norm_matmul(x, gain, w, tm, tn, out_dtype=jnp.float32):
    m, k = x.shape
    n = w.shape[1]
    return pl.pallas_call(
        _norm_matmul_body,
        grid=(m // tm, n // tn),
        in_specs=[
            pl.BlockSpec((tm, k), lambda i, j: (i, 0)),
            pl.BlockSpec((1, k), lambda i, j: (0, 0)),
            pl.BlockSpec((k, tn), lambda i, j: (0, j)),
        ],
        out_specs=pl.BlockSpec((tm, tn), lambda i, j: (i, j)),
        out_shape=jax.ShapeDtypeStruct((m, n), out_dtype),
        scratch_shapes=[pltpu.VMEM((tm, k), jnp.bfloat16)],
        compiler_params=pltpu.CompilerParams(
            dimension_semantics=("arbitrary", "arbitrary"), vmem_limit_bytes=VMEM_LIMIT_BYTES),
        name="norm_matmul",
    )(x, gain.reshape(1, k), w)


def _out_proj_body(a1_ref, a2_ref, w1_ref, w2_ref, r_ref, o_ref):
    acc = jnp.dot(a1_ref[...].astype(jnp.bfloat16), w1_ref[...], preferred_element_type=jnp.float32)
    acc = acc + jnp.dot(a2_ref[...].astype(jnp.bfloat16), w2_ref[...], preferred_element_type=jnp.float32)
    o_ref[...] = r_ref[...] + acc


def _out_proj(a1, a2, w, res, tm, tn):
    m, k1 = a1.shape
    k2 = a2.shape[1]
    n = w.shape[1]
    return pl.pallas_call(
        _out_proj_body,
        grid=(m // tm, n // tn),
        in_specs=[
            pl.BlockSpec((tm, k1), lambda i, j: (i, 0)),
            pl.BlockSpec((tm, k2), lambda i, j: (i, 0)),
            pl.BlockSpec((k1, tn), lambda i, j: (0, j)),
            pl.BlockSpec((k2, tn), lambda i, j: (k1 // k2, j)),
            pl.BlockSpec((tm, tn), lambda i, j: (i, j)),
        ],
        out_specs=pl.BlockSpec((tm, tn), lambda i, j: (i, j)),
        out_shape=jax.ShapeDtypeStruct((m, n), jnp.float32),
        compiler_params=pltpu.CompilerParams(
            dimension_semantics=("arbitrary", "arbitrary"), vmem_limit_bytes=VMEM_LIMIT_BYTES),
        name="out_proj",
    )(a1, a2, w, w, res)


def _hgrn2_body(q_ref, f_ref, v_ref, g_ref, lb_ref, nw_ref, o_ref, qd_ref, kd_ref, ku_ref, vb_ref, dec_ref, sp_ref):
    t, dk = q_ref.shape
    C = HG_CHUNK
    nc = t // C
    lb = lb_ref[...]
    f = lb + (1.0 - lb) * jax.nn.sigmoid(f_ref[...])
    kf = 1.0 - f
    b = jnp.log(f)
    row = lax.broadcasted_iota(jnp.int32, (t, dk), 0) % C
    shift = 1
    while shift < C:
        b = b + jnp.where(row >= shift, pltpu.roll(b, shift, 0), 0.0)
        shift *= 2
    b3 = b.reshape(nc, C, dk)
    b_end = b3[:, C - 1:C, :]
    qd_ref[...] = (q_ref[...] * jnp.exp(b)).astype(jnp.bfloat16)
    kd_ref[...] = (kf * jnp.exp(-b)).astype(jnp.bfloat16)
    ku_ref[...] = (kf.reshape(nc, C, dk) * jnp.exp(b_end - b3)).reshape(t, dk).astype(jnp.bfloat16)
    vb_ref[...] = v_ref[...].astype(jnp.bfloat16)
    dec_ref[...] = jnp.exp(b_end)

    st = jnp.zeros((dk, dk), jnp.float32)
    for n in range(nc):
        rows = slice(n * C, (n + 1) * C)
        sp_ref[n] = st.astype(jnp.bfloat16)
        upd_t = lax.dot_general(vb_ref[rows, :], ku_ref[rows, :], (((0,), (0,)), ((), ())),
                                preferred_element_type=jnp.float32)
        st = dec_ref[n] * st + upd_t

    causal = lax.broadcasted_iota(jnp.int32, (C, C), 0) >= lax.broadcasted_iota(jnp.int32, (C, C), 1)
    for n in range(nc):
        rows = slice(n * C, (n + 1) * C)
        qd = qd_ref[rows, :]
        attn = lax.dot_general(qd, kd_ref[rows, :], (((1,), (1,)), ((), ())), preferred_element_type=jnp.float32)
        attn = jnp.where(causal, attn, 0.0).astype(jnp.bfloat16)
        o = jnp.dot(attn, vb_ref[rows, :], preferred_element_type=jnp.float32)
        o = o + lax.dot_general(qd, sp_ref[n], (((1,), (1,)), ((), ())), preferred_element_type=jnp.float32)
        o = o * lax.rsqrt(jnp.mean(o * o, axis=-1, keepdims=True) + NORM_EPS) * nw_ref[...]
        o_ref[rows, :] = o * jax.nn.silu(g_ref[rows, :])


def _hgrn2(proj, lb, norm_w, batch):
    n = proj.shape[0]
    t = n // batch
    dk, H = HG_HEAD_DIM, HG_HEADS
    nc = t // HG_CHUNK

    def part(p):
        return pl.BlockSpec((t, dk), lambda b, h: (b, p * H + h))

    slab = pltpu.VMEM((t, dk), jnp.bfloat16)
    return pl.pallas_call(
        _hgrn2_body,
        grid=(batch, H),
        in_specs=[part(0), part(1), part(2), part(3),
                  pl.BlockSpec((1, dk), lambda b, h: (0, h)),
                  pl.BlockSpec((1, dk), lambda b, h: (0, 0))],
        out_specs=pl.BlockSpec((t, dk), lambda b, h: (b, h)),
        out_shape=jax.ShapeDtypeStruct((n, H * dk), jnp.float32),
        scratch_shapes=[slab, slab, slab, slab, pltpu.VMEM((nc, 1, dk), jnp.float32),
                        pltpu.VMEM((nc, dk, dk), jnp.bfloat16)],
        compiler_params=pltpu.CompilerParams(
            dimension_semantics=("arbitrary", "arbitrary"), vmem_limit_bytes=VMEM_LIMIT_BYTES),
        name="hgrn2",
    )(proj, proj, proj, proj, lb.reshape(1, H * dk), norm_w.reshape(1, dk))


GELU_C0 = float(np.sqrt(2.0 / np.pi))
GELU_C1 = GELU_C0 * 0.044715


def _gelu(x):
    half = 0.5 * x
    return half + half * jnp.tanh(x * (GELU_C0 + GELU_C1 * (x * x)))


def _head_norm(x, w):
    return x * lax.rsqrt(jnp.mean(x * x, axis=-1, keepdims=True) + NORM_EPS) * w


def _kv_prep_body(c_ref, s_ref, w_ref, ksw_ref, kww_ref, kn_ref, vv_ref, cf_ref):
    dh = NSA_HEAD_DIM
    for h in range(NSA_KV_HEADS):
        k_cols = slice(h * dh, (h + 1) * dh)
        v_cols = slice(KV_W + h * dh, KV_W + (h + 1) * dh)
        cf_ref[0, h] = c_ref[:, k_cols]
        cf_ref[1, h] = c_ref[:, v_cols]
        kn_ref[0, h] = _head_norm(s_ref[:, k_cols], ksw_ref[...]).astype(jnp.bfloat16)
        vv_ref[0, h] = s_ref[:, v_cols].astype(jnp.bfloat16)
        kn_ref[1, h] = _head_norm(w_ref[:, k_cols], kww_ref[...]).astype(jnp.bfloat16)
        vv_ref[1, h] = w_ref[:, v_cols].astype(jnp.bfloat16)


def _kv_prep(proj, ks_w, kw_w, tm):
    n = proj.shape[0]
    dh = NSA_HEAD_DIM
    pair = 2 * KV_W
    col0 = KV_COL0 // pair
    head_block = pl.BlockSpec((2, NSA_KV_HEADS, tm, dh), lambda i: (0, 0, i, 0))
    w_spec = pl.BlockSpec((1, dh), lambda i: (0, 0))
    return pl.pallas_call(
        _kv_prep_body,
        grid=(n // tm,),
        in_specs=[pl.BlockSpec((tm, pair), lambda i: (i, col0)),
                  pl.BlockSpec((tm, pair), lambda i: (i, col0 + 1)),
                  pl.BlockSpec((tm, pair), lambda i: (i, col0 + 2)),
                  w_spec, w_spec],
        out_specs=[head_block, head_block, head_block],
        out_shape=[jax.ShapeDtypeStruct((2, NSA_KV_HEADS, n, dh), jnp.bfloat16),
                   jax.ShapeDtypeStruct((2, NSA_KV_HEADS, n, dh), jnp.bfloat16),
                   jax.ShapeDtypeStruct((2, NSA_KV_HEADS, n, dh), jnp.float32)],
        compiler_params=pltpu.CompilerParams(
            dimension_semantics=("arbitrary",), vmem_limit_bytes=VMEM_LIMIT_BYTES),
        name="kv_prep",
    )(proj, proj, proj, ks_w.reshape(1, dh), kw_w.reshape(1, dh))


def _cmp_kv_body(r_ref, pos_ref, w1_ref, w2_ref, nw_ref, o_ref):
    half = CMP_STRIDE * NSA_HEAD_DIM
    n_strips = r_ref.shape[2]
    for kind in range(2):
        strips = r_ref[kind, 0]
        top = (strips + pos_ref[kind, 0:1, :]).astype(jnp.bfloat16)
        bot = (strips + pos_ref[kind, 1:2, :]).astype(jnp.bfloat16)
        a = jnp.dot(top, w1_ref[kind, :half, :], preferred_element_type=jnp.float32)
        b = jnp.dot(bot, w1_ref[kind, half:, :], preferred_element_type=jnp.float32)
        hid = _gelu(a + pltpu.roll(b, n_strips - 1, 0))
        out = jnp.dot(hid.astype(jnp.bfloat16), w2_ref[kind], preferred_element_type=jnp.float32)
        if kind == 0:
            out = _head_norm(out, nw_ref[...])
        o_ref[kind, 0, 0] = out.astype(jnp.bfloat16)


def _cmp_kv(cf, pos, w1, w2, kc_w, batch):
    _, hkv, n, dh = cf.shape
    t = n // batch
    n_strips = t // CMP_STRIDE
    strips = cf.reshape(2, hkv, n // CMP_STRIDE, CMP_STRIDE * dh)
    return pl.pallas_call(
        _cmp_kv_body,
        grid=(batch, hkv),
        in_specs=[pl.BlockSpec((2, 1, n_strips, CMP_STRIDE * dh), lambda b, g: (0, g, b, 0)),
                  pl.BlockSpec(pos.shape, lambda b, g: (0, 0, 0)),
                  pl.BlockSpec(w1.shape, lambda b, g: (0, 0, 0)),
                  pl.BlockSpec(w2.shape, lambda b, g: (0, 0, 0)),
                  pl.BlockSpec((1, dh), lambda b, g: (0, 0))],
        out_specs=pl.BlockSpec((2, 1, 1, n_strips, dh), lambda b, g: (0, g, b, 0, 0)),
        out_shape=jax.ShapeDtypeStruct((2, hkv, batch, n_strips, dh), jnp.bfloat16),
        compiler_params=pltpu.CompilerParams(
            dimension_semantics=("arbitrary", "arbitrary"), vmem_limit_bytes=VMEM_LIMIT_BYTES),
        name="cmp_kv",
    )(strips, pos, w1, w2, kc_w.reshape(1, dh))


def _flash_step(qs_ref, k, v, slope_ref, m_ref, l_ref, acc_ref, base, mask_fn, tq):
    tk = k.shape[0]
    g = NSA_GROUP
    dist = base + lax.broadcasted_iota(jnp.int32, (tq, tk), 0) - lax.broadcasted_iota(jnp.int32, (tq, tk), 1)
    valid = mask_fn(dist)[None]
    s = lax.dot_general(qs_ref[...], k, (((1,), (1,)), ((), ())),
                        preferred_element_type=jnp.float32).reshape(g, tq, tk)
    s = jnp.where(valid, s - slope_ref[0] * dist.astype(jnp.float32)[None], NEG_INF)
    m_old = m_ref[...].reshape(g, tq, 1)
    m_new = jnp.maximum(m_old, jnp.max(s, axis=-1, keepdims=True))
    p = jnp.where(valid, jnp.exp(s - m_new), 0.0)
    alpha = jnp.exp(m_old - m_new)
    l_ref[...] = (alpha * l_ref[...].reshape(g, tq, 1) + jnp.sum(p, axis=-1, keepdims=True)).reshape(g * tq, 1)
    pv = jnp.dot(p.reshape(g * tq, tk).astype(jnp.bfloat16), v, preferred_element_type=jnp.float32)
    acc_ref[...] = alpha.reshape(g * tq, 1) * acc_ref[...] + pv
    m_ref[...] = m_new.reshape(g * tq, 1)


def _nsa_attn_body(q_ref, gate_ref, kc_ref, vc_ref, ks_ref, vs_ref, kw_ref, vw_ref, qw_ref, slope_ref, o_ref,
                   qs_ref, sel_ref, oc_ref, ms_ref, ls_ref, as_ref, mw_ref, lw_ref, aw_ref, *, tq, tk, n_sel):
    i = pl.program_id(2)
    j = pl.program_id(3)
    G, dh = NSA_GROUP, NSA_HEAD_DIM
    last_j = (i * tq + tq - 1) // tk

    @pl.when(j == 0)
    def _():
        x = q_ref[...]
        for r in range(G):
            qn = _head_norm(x[:, r * dh:(r + 1) * dh], qw_ref[...]) * (dh ** -0.5)
            qs_ref[r * tq:(r + 1) * tq, :] = qn.astype(jnp.bfloat16)
        for ref in (ms_ref, mw_ref):
            ref[...] = jnp.full(ref.shape, NEG_INF, jnp.float32)
        for ref in (ls_ref, lw_ref, as_ref, aw_ref):
            ref[...] = jnp.zeros(ref.shape, jnp.float32)

        n_cmp_pad = kc_ref.shape[-2]
        t_pos = i * tq + lax.broadcasted_iota(jnp.int32, (tq, n_cmp_pad), 0)
        blk = lax.broadcasted_iota(jnp.int32, (tq, n_cmp_pad), 1)
        dist = t_pos - (blk * CMP_STRIDE + CMP_BLOCK - 1)
        valid = (dist >= 0) & (blk < n_cmp_pad - 1)
        distf = dist.astype(jnp.float32)
        cb = lax.broadcasted_iota(jnp.int32, (n_sel, n_cmp_pad), 1) * CMP_STRIDE
        sb = lax.broadcasted_iota(jnp.int32, (n_sel, n_cmp_pad), 0) * SEL_BLOCK
        overlap_t = ((cb < sb + SEL_BLOCK) & (cb + CMP_BLOCK > sb)).astype(jnp.bfloat16)
        imp = None
        for r in range(G):
            rows = slice(r * tq, (r + 1) * tq)
            s = lax.dot_general(qs_ref[rows, :], kc_ref[0, 0, 0], (((1,), (1,)), ((), ())),
                                preferred_element_type=jnp.float32)
            s = jnp.where(valid, s - slope_ref[0, r] * distf, NEG_INF)
            p = jnp.where(valid, jnp.exp(s - jnp.max(s, axis=-1, keepdims=True)), 0.0)
            denom = jnp.maximum(jnp.sum(p, axis=-1, keepdims=True), 1e-30)
            p = (p / denom).astype(jnp.bfloat16)
            oc_ref[rows, :] = jnp.dot(p, vc_ref[0, 0, 0], preferred_element_type=jnp.float32)
            part = lax.dot_general(overlap_t, p, (((1,), (1,)), ((), ())), preferred_element_type=jnp.float32)
            imp = part if imp is None else imp + part

        cur = (i * tq + lax.broadcasted_iota(jnp.int32, (n_sel, tq), 1)) // SEL_BLOCK
        jb = lax.broadcasted_iota(jnp.int32, (n_sel, tq), 0)
        forced = (jb == 0) | ((jb <= cur) & (jb > cur - N_LOCAL))
        imp = jnp.where(forced, FORCE_SCORE, jnp.where(jb > cur, -FORCE_SCORE, imp))
        rank = jnp.zeros((n_sel, tq), jnp.float32)
        for c in range(n_sel):
            row = imp[c:c + 1, :]
            ahead = (row > imp) | ((row == imp) & (jb > c))
            rank = rank + jnp.where(ahead, 1.0, 0.0)
        sel_t = jnp.where(rank < min(N_SELECT, n_sel), 1.0, 0.0)
        sel_t = jnp.concatenate([sel_t, jnp.zeros((sel_ref.shape[1] - n_sel, tq), jnp.float32)], axis=0)
        sel_ref[...] = sel_t.T.astype(jnp.bfloat16)

    @pl.when(j <= last_j)
    def _():
        n_pad = sel_ref.shape[1]
        blk = lax.broadcasted_iota(jnp.int32, (n_pad, tk), 0)
        col_blk = j * (tk // SEL_BLOCK) + lax.broadcasted_iota(jnp.int32, (n_pad, tk), 1) // SEL_BLOCK
        expand = (blk == col_blk).astype(jnp.bfloat16)

        def mask(dist):
            picked = jnp.dot(sel_ref[...], expand, preferred_element_type=jnp.float32)
            return (dist >= 0) & (picked > 0.5)

        _flash_step(qs_ref, ks_ref[0, 0], vs_ref[0, 0], slope_ref, ms_ref, ls_ref, as_ref,
                    i * tq - j * tk, mask, tq)

    @pl.when((j <= last_j) & (j * tk + tk - 1 >= i * tq - (WINDOW - 1)))
    def _():
        _flash_step(qs_ref, kw_ref[0, 0], vw_ref[0, 0], slope_ref, mw_ref, lw_ref, aw_ref,
                    i * tq - j * tk, lambda dist: (dist >= 0) & (dist < WINDOW), tq)

    @pl.when(j == pl.num_programs(3) - 1)
    def _():
        gates = jax.nn.sigmoid(gate_ref[0])
        o_sel = as_ref[...] / ls_ref[...]
        o_win = aw_ref[...] / lw_ref[...]
        o_cmp = oc_ref[...]
        outs = []
        for r in range(G):
            rows = slice(r * tq, (r + 1) * tq)
            outs.append(gates[:, 3 * r:3 * r + 1] * o_cmp[rows] + gates[:, 3 * r + 1:3 * r + 2] * o_sel[rows]
                        + gates[:, 3 * r + 2:3 * r + 3] * o_win[rows])
        o_ref[...] = jnp.concatenate(outs, axis=-1)


def _nsa_attn(proj, gates, cmp_kv, kn, vv, q_w, slopes, batch, tq=512, tk=512):
    n = proj.shape[0]
    t = n // batch
    G, dh, hkv = NSA_GROUP, NSA_HEAD_DIM, NSA_KV_HEADS
    n_cmp_pad = cmp_kv.shape[-2]
    n_sel = t // SEL_BLOCK
    q_blk0 = Q_COL0 // (G * dh)

    def kv_map(kind):
        def index(b, g, i, j):
            return (kind, g, b * (t // tk) + jnp.minimum(j, (i * tq + tq - 1) // tk), 0)
        return index

    def cmp_map(kind):
        return lambda b, g, i, j: (kind, g, b, 0, 0)

    kv_block = (1, 1, tk, dh)
    cmp_block = (1, 1, 1, n_cmp_pad, dh)
    slab = pltpu.VMEM((G * tq, dh), jnp.float32)
    stat = pltpu.VMEM((G * tq, 1), jnp.float32)
    return pl.pallas_call(
        functools.partial(_nsa_attn_body, tq=tq, tk=tk, n_sel=n_sel),
        grid=(batch, hkv, t // tq, t // tk),
        in_specs=[
            pl.BlockSpec((tq, G * dh), lambda b, g, i, j: (b * (t // tq) + i, q_blk0 + g)),
            pl.BlockSpec((1, tq, 3 * G), lambda b, g, i, j: (g, b * (t // tq) + i, 0)),
            pl.BlockSpec(cmp_block, cmp_map(0)), pl.BlockSpec(cmp_block, cmp_map(1)),
            pl.BlockSpec(kv_block, kv_map(0)), pl.BlockSpec(kv_block, kv_map(0)),
            pl.BlockSpec(kv_block, kv_map(1)), pl.BlockSpec(kv_block, kv_map(1)),
            pl.BlockSpec((1, dh), lambda b, g, i, j: (0, 0)),
            pl.BlockSpec((1, G, 1, 1), lambda b, g, i, j: (g, 0, 0, 0)),
        ],
        out_specs=pl.BlockSpec((tq, G * dh), lambda b, g, i, j: (b * (t // tq) + i, g)),
        out_shape=jax.ShapeDtypeStruct((n, hkv * G * dh), jnp.float32),
        scratch_shapes=[pltpu.VMEM((G * tq, dh), jnp.bfloat16), pltpu.VMEM((tq, LANES), jnp.bfloat16),
                        slab, stat, stat, slab, stat, stat, slab],
        compiler_params=pltpu.CompilerParams(
            dimension_semantics=("arbitrary",) * 4, vmem_limit_bytes=VMEM_LIMIT_BYTES),
        name="nsa_attn",
    )(proj, gates, cmp_kv, cmp_kv, kn, vv, kn, vv, q_w.reshape(1, dh), slopes)


def _nsa(proj, batch, q_norm_w, kc_norm_w, ks_norm_w, kw_norm_w, pos_k, pos_v, w_ck1, w_ck2, w_cv1, w_cv2):
    n = proj.shape[0]
    G, hkv = NSA_GROUP, NSA_KV_HEADS
    kn, vv, cf = _kv_prep(proj, ks_norm_w, kw_norm_w, 512)
    half = CMP_STRIDE * NSA_HEAD_DIM
    pos = jnp.stack([pos_k.reshape(2, half), pos_v.reshape(2, half)])
    w1 = jnp.stack([w_ck1, w_cv1]).astype(jnp.bfloat16)
    w2 = jnp.stack([w_ck2, w_cv2]).astype(jnp.bfloat16)
    cmp_kv = _cmp_kv(cf, pos, w1, w2, kc_norm_w, batch)
    gates = proj[:, GATE_COL0:GATE_COL0 + N_GATES].reshape(n, hkv, 3 * G).transpose(1, 0, 2)
    slopes = jnp.asarray(2.0 ** (-8.0 * np.arange(1, NSA_Q_HEADS + 1) / NSA_Q_HEADS), jnp.float32)
    return _nsa_attn(proj, gates, cmp_kv, kn, vv, q_norm_w, slopes.reshape(hkv, G, 1, 1), batch)


BIG_NEG = -3.0e38
PEER_CAND = [(a, b) for a in range(PEER_TOPK) for b in range(PEER_TOPK) if (a + 1) * (b + 1) <= PEER_TOPK]
PEER_CAND_ROWS = -(-len(PEER_CAND) // 8) * 8


def _peer_route_body(h_ref, g_ref, wq_ref, keys_ref, xt_ref, rk2_ref, a2_ref, cnt_ref, r_ref,
                     qt_ref, sc_ref, cur_ref, top_ref, cand_ref):
    x = h_ref[...]
    hn = x * lax.rsqrt(jnp.mean(x * x, axis=-1, keepdims=True) + NORM_EPS) * g_ref[...]
    hnt = hn.T.astype(jnp.bfloat16)
    xt_ref[...] = hnt
    qt_ref[...] = jnp.dot(wq_ref[...], hnt, preferred_element_type=jnp.float32).astype(jnp.bfloat16)
    cand_ref[...] = jnp.full(cand_ref.shape, BIG_NEG, jnp.float32)
    kd = PEER_KEY_DIM // 2
    n_parts = 2 * PEER_HEADS
    for hp in range(n_parts):
        s = jnp.dot(keys_ref[hp], qt_ref[hp * kd:(hp + 1) * kd, :], preferred_element_type=jnp.float32)
        sc_ref[hp] = s
        cur_ref[hp] = s

    ridx = lax.broadcasted_iota(jnp.int32, cur_ref.shape, 1)

    def extract(k, carry):
        cur = cur_ref[...]
        mk = jnp.max(cur, axis=1, keepdims=True)
        first = jnp.min(jnp.where(cur == mk, ridx, N_KEYS), axis=1, keepdims=True)
        top_ref[k] = mk
        cur_ref[...] = jnp.where(ridx == first, BIG_NEG, cur)
        return carry

    lax.fori_loop(0, PEER_TOPK, extract, 0)

    def head(hd, carry):
        t1 = [top_ref[a, 2 * hd] for a in range(PEER_TOPK)]
        t2 = [top_ref[b, 2 * hd + 1] for b in range(PEER_TOPK)]
        cmax = t1[0] + t2[0]
        cands = [t1[a] + t2[b] for a, b in PEER_CAND]
        for i, c in enumerate(cands):
            cand_ref[i:i + 1, :] = c
        call = cand_ref[...]
        n_gt = jnp.zeros(call.shape, jnp.float32)
        for c in cands:
            n_gt = n_gt + jnp.where(c > call, 1.0, 0.0)
        tau = jnp.min(jnp.where(n_gt <= PEER_TOPK - 1, call, -BIG_NEG), axis=0, keepdims=True)
        z = jnp.sum(jnp.where(call >= tau, jnp.exp(call - cmax), 0.0), axis=0, keepdims=True)
        count = [None] * PEER_TOPK
        for (a, b), c in zip(PEER_CAND, cands):
            hit = jnp.where(c >= tau, 1.0, 0.0)
            count[a] = hit if count[a] is None else count[a] + hit
        s1 = sc_ref[2 * hd]
        s2 = sc_ref[2 * hd + 1]
        cnt = jnp.zeros(s1.shape, jnp.float32)
        rank2 = jnp.zeros(s2.shape, jnp.float32)
        for a in range(PEER_TOPK):
            cnt = jnp.where(s1 == t1[a], count[a], cnt)
            rank2 = rank2 + jnp.where(t2[a] > s2, 1.0, 0.0)
        rk2_ref[hd] = rank2.astype(jnp.bfloat16)
        a2_ref[hd] = jnp.exp(s2 - t2[0]).astype(jnp.bfloat16)
        cnt_ref[hd] = cnt
        r_ref[hd] = jnp.exp(s1 - t1[0]) / z
        return carry

    lax.fori_loop(0, PEER_HEADS, head, 0)


def _peer_route(h, gain, wq_t, keys, tt):
    n, d = h.shape
    hp, nk, kd = keys.shape
    stat = jax.ShapeDtypeStruct((PEER_HEADS, nk, n), jnp.float32)
    stat_bf16 = jax.ShapeDtypeStruct((PEER_HEADS, nk, n), jnp.bfloat16)
    stat_spec = pl.BlockSpec((PEER_HEADS, nk, tt), lambda i: (0, 0, i))
    return pl.pallas_call(
        _peer_route_body,
        grid=(n // tt,),
        in_specs=[
            pl.BlockSpec((tt, d), lambda i: (i, 0)),
            pl.BlockSpec((1, d), lambda i: (0, 0)),
            pl.BlockSpec(wq_t.shape, lambda i: (0, 0)),
            pl.BlockSpec(keys.shape, lambda i: (0, 0, 0)),
        ],
        out_specs=[pl.BlockSpec((d, tt), lambda i: (0, i)), stat_spec, stat_spec, stat_spec, stat_spec],
        out_shape=[jax.ShapeDtypeStruct((d, n), jnp.bfloat16), stat_bf16, stat_bf16, stat, stat],
        scratch_shapes=[pltpu.VMEM((wq_t.shape[0], tt), jnp.bfloat16),
                        pltpu.VMEM((hp, nk, tt), jnp.float32), pltpu.VMEM((hp, nk, tt), jnp.float32),
                        pltpu.VMEM((PEER_TOPK, hp, 1, tt), jnp.float32),
                        pltpu.VMEM((PEER_CAND_ROWS, tt), jnp.float32)],
        compiler_params=pltpu.CompilerParams(
            dimension_semantics=("arbitrary",), vmem_limit_bytes=VMEM_LIMIT_BYTES),
        name="peer_route",
    )(h, gain.reshape(1, d), wq_t, keys)


def _peer_experts_body(xt_ref, u_ref, vt_ref, rk2_ref, a2_ref, cnt_ref, r_ref, o_ref, *, n_i1):
    @pl.when(pl.program_id(1) == 0)
    def _():
        o_ref[...] = jnp.zeros(o_ref.shape, jnp.float32)

    act = _gelu(jnp.dot(u_ref[...], xt_ref[...], preferred_element_type=jnp.float32))
    parts = []
    for i1 in range(n_i1):
        w = None
        for hd in range(PEER_HEADS):
            picked = rk2_ref[hd] < cnt_ref[hd, i1:i1 + 1, :].astype(jnp.bfloat16)
            term = jnp.where(picked, a2_ref[hd], 0.0) * r_ref[hd, i1:i1 + 1, :].astype(jnp.bfloat16)
            w = term if w is None else w + term
        parts.append(w * act[i1 * N_KEYS:(i1 + 1) * N_KEYS].astype(jnp.bfloat16))
    p = jnp.concatenate(parts, axis=0)
    o_ref[...] += jnp.dot(vt_ref[...], p, preferred_element_type=jnp.float32)


def _peer_experts(xt, u, vt, rk2, a2, cnt, r, tt, te):
    d, n = xt.shape
    e = u.shape[0]
    n_i1 = te // N_KEYS
    full = pl.BlockSpec((PEER_HEADS, N_KEYS, tt), lambda i, j: (0, 0, i))
    part = pl.BlockSpec((PEER_HEADS, n_i1, tt), lambda i, j: (0, j, i))
    return pl.pallas_call(
        functools.partial(_peer_experts_body, n_i1=n_i1),
        grid=(n // tt, e // te),
        in_specs=[
            pl.BlockSpec((d, tt), lambda i, j: (0, i)),
            pl.BlockSpec((te, d), lambda i, j: (j, 0)),
            pl.BlockSpec((d, te), lambda i, j: (0, j)),
            full, full, part, part,
        ],
        out_specs=pl.BlockSpec((d, tt), lambda i, j: (0, i)),
        out_shape=jax.ShapeDtypeStruct((d, n), jnp.float32),
        compiler_params=pltpu.CompilerParams(
            dimension_semantics=("arbitrary", "arbitrary"), vmem_limit_bytes=VMEM_LIMIT_BYTES),
        name="peer_experts",
    )(xt, u, vt, rk2, a2, cnt, r)


def _peer(h, gain, w_q, sub_keys, u_tab, v_tab):
    keys = sub_keys.reshape(PEER_HEADS * 2, N_KEYS, PEER_KEY_DIM // 2).astype(jnp.bfloat16)
    xt, rk2, a2, cnt, r = _peer_route(h, gain, w_q.T.astype(jnp.bfloat16), keys, 256)
    out_t = _peer_experts(xt, u_tab.astype(jnp.bfloat16), v_tab.T.astype(jnp.bfloat16),
                          rk2, a2, cnt, r, 512, 1024)
    return out_t.T


def kernel(x, norm1_w, w_in, hg_lb_logits, hg_norm_w, q_norm_w, kc_norm_w, ks_norm_w, kw_norm_w,
           cmp_pos_k, cmp_pos_v, w_ck1, w_ck2, w_cv1, w_cv2, w_out, norm2_w,
           peer_w_q, peer_sub_keys, peer_u, peer_v):
    B, T, D = x.shape
    n = B * T
    layer = 0
    lower_bounds = jnp.cumsum(jax.nn.softmax(hg_lb_logits, axis=0), axis=0)
    xt = x.reshape(n, D)

    w_in_b = jnp.pad(w_in[layer].astype(jnp.bfloat16), ((0, 0), (0, IN_COLS_PADDED - IN_COLS)))
    proj = _norm_matmul(xt, norm1_w[layer], w_in_b, 512, 1024)
    hg_out = _hgrn2(proj, lower_bounds[layer], hg_norm_w[layer], B)
    nsa_out = _nsa(proj, B, q_norm_w[layer], kc_norm_w[layer], ks_norm_w[layer], kw_norm_w[layer],
                   cmp_pos_k[layer], cmp_pos_v[layer], w_ck1[layer], w_ck2[layer], w_cv1[layer], w_cv2[layer])
    h = _out_proj(hg_out, nsa_out, w_out[layer].astype(jnp.bfloat16), xt, 512, 1024)

    y = h + _peer(h, norm2_w[layer], peer_w_q[layer], peer_sub_keys[layer], peer_u[layer], peer_v[layer])
    return y.reshape(B, T, D)
```

```python
import functools

import jax
import jax.numpy as jnp
import numpy as np
from jax import lax
from jax.experimental import pallas as pl
from jax.experimental.pallas import tpu as pltpu

D_MODEL = 2048
HG_WIDTH = 1024
HG_HEAD_DIM = 128
HG_HEADS = 8
HG_CHUNK = 64
NSA_WIDTH = 1024
NSA_HEAD_DIM = 64
NSA_Q_HEADS = 16
NSA_KV_HEADS = 4
NSA_GROUP = 4
CMP_BLOCK = 32
CMP_STRIDE = 16
CMP_HIDDEN = 256
SEL_BLOCK = 64
N_SELECT = 16
N_LOCAL = 2
WINDOW = 512
FORCE_SCORE = 1e9
NEG_INF = -1e30
MASK_OFF = 3e30
LOG2E = float(np.log2(np.e))
PEER_HEADS = 8
N_KEYS = 128
PEER_KEY_DIM = 256
PEER_TOPK = 16
NORM_EPS = 1e-6
KV_W = NSA_KV_HEADS * NSA_HEAD_DIM
N_GATES = 3 * NSA_Q_HEADS
IN_SIZES = [HG_WIDTH] * 4 + [NSA_WIDTH] + [KV_W] * 6 + [N_GATES]
IN_COLS = sum(IN_SIZES)
IN_COLS_PADDED = 7168
Q_COL0 = 4 * HG_WIDTH
KV_COL0 = Q_COL0 + NSA_WIDTH
GATE_COL0 = KV_COL0 + 6 * KV_W

VMEM_LIMIT_BYTES = 48 * 1024 * 1024
LANES = 128


def _norm_matmul_body(x_ref, g_ref, w_ref, o_ref, xn_ref):
    @pl.when(pl.program_id(1) == 0)
    def _():
        x = x_ref[...]
        r = lax.rsqrt(jnp.mean(x * x, axis=-1, keepdims=True) + NORM_EPS)
        xn_ref[...] = (x * r * g_ref[...]).astype(jnp.bfloat16)

    o_ref[...] = jnp.dot(xn_ref[...], w_ref[...], preferred_element_type=jnp.float32).astype(o_ref.dtype)


def _norm_matmul(x, gain, w, tm, tn, out_dtype=jnp.float32):
    m, k = x.shape
    n = w.shape[1]
    return pl.pallas_call(
        _norm_matmul_body,
        grid=(m // tm, n // tn),
        in_specs=[
            pl.BlockSpec((tm, k), lambda i, j: (i, 0)),
            pl.BlockSpec((1, k), lambda i, j: (0, 0)),
            pl.BlockSpec((k, tn), lambda i, j: (0, j)),
        ],
        out_specs=pl.BlockSpec((tm, tn), lambda i, j: (i, j)),
        out_shape=jax.ShapeDtypeStruct((m, n), out_dtype),
        scratch_shapes=[pltpu.VMEM((tm, k), jnp.bfloat16)],
        compiler_params=pltpu.CompilerParams(
            dimension_semantics=("arbitrary", "arbitrary"), vmem_limit_bytes=VMEM_LIMIT_BYTES),
        name="norm_matmul",
    )(x, gain.reshape(1, k), w)


def _out_proj_body(a1_ref, a2_ref, w1_ref, w2_ref, r_ref, o_ref):
    acc = jnp.dot(a1_ref[...].astype(jnp.bfloat16), w1_ref[...], preferred_element_type=jnp.float32)
    acc = acc + jnp.dot(a2_ref[...].astype(jnp.bfloat16), w2_ref[...], preferred_element_type=jnp.float32)
    o_ref[...] = r_ref[...] + acc


def _out_proj(a1, a2, w, res, tm, tn):
    m, k1 = a1.shape
    k2 = a2.shape[1]
    n = w.shape[1]
    return pl.pallas_call(
        _out_proj_body,
        grid=(m // tm, n // tn),
        in_specs=[
            pl.BlockSpec((tm, k1), lambda i, j: (i, 0)),
            pl.BlockSpec((tm, k2), lambda i, j: (i, 0)),
            pl.BlockSpec((k1, tn), lambda i, j: (0, j)),
            pl.BlockSpec((k2, tn), lambda i, j: (k1 // k2, j)),
            pl.BlockSpec((tm, tn), lambda i, j: (i, j)),
        ],
        out_specs=pl.BlockSpec((tm, tn), lambda i, j: (i, j)),
        out_shape=jax.ShapeDtypeStruct((m, n), jnp.float32),
        compiler_params=pltpu.CompilerParams(
            dimension_semantics=("arbitrary", "arbitrary"), vmem_limit_bytes=VMEM_LIMIT_BYTES),
        name="out_proj",
    )(a1, a2, w, w, res)


def _hgrn2_body(q_ref, f_ref, v_ref, g_ref, lb_ref, nw_ref, o_ref, qd_ref, kd_ref, ku_ref, vb_ref, dec_ref, sp_ref):
    t, dk = q_ref.shape
    C = HG_CHUNK
    nc = t // C
    lb = lb_ref[...]
    f = lb + (1.0 - lb) * jax.nn.sigmoid(f_ref[...])
    kf = 1.0 - f
    b = jnp.log(f)
    row = lax.broadcasted_iota(jnp.int32, (t, dk), 0) % C
    shift = 1
    while shift < C:
        b = b + jnp.where(row >= shift, pltpu.roll(b, shift, 0), 0.0)
        shift *= 2
    b3 = b.reshape(nc, C, dk)
    b_end = b3[:, C - 1:C, :]
    qd_ref[...] = (q_ref[...] * jnp.exp(b)).astype(jnp.bfloat16)
    kd_ref[...] = (kf * jnp.exp(-b)).astype(jnp.bfloat16)
    ku_ref[...] = (kf.reshape(nc, C, dk) * jnp.exp(b_end - b3)).reshape(t, dk).astype(jnp.bfloat16)
    vb_ref[...] = v_ref[...].astype(jnp.bfloat16)
    dec_ref[...] = jnp.exp(b_end)

    st = jnp.zeros((dk, dk), jnp.float32)
    for n in range(nc):
        rows = slice(n * C, (n + 1) * C)
        sp_ref[n] = st.astype(jnp.bfloat16)
        upd_t = lax.dot_general(vb_ref[rows, :], ku_ref[rows, :], (((0,), (0,)), ((), ())),
                                preferred_element_type=jnp.float32)
        st = dec_ref[n] * st + upd_t

    causal = lax.broadcasted_iota(jnp.int32, (C, C), 0) >= lax.broadcasted_iota(jnp.int32, (C, C), 1)
    for n in range(nc):
        rows = slice(n * C, (n + 1) * C)
        qd = qd_ref[rows, :]
        attn = lax.dot_general(qd, kd_ref[rows, :], (((1,), (1,)), ((), ())), preferred_element_type=jnp.float32)
        attn = jnp.where(causal, attn, 0.0).astype(jnp.bfloat16)
        o = jnp.dot(attn, vb_ref[rows, :], preferred_element_type=jnp.float32)
        o = o + lax.dot_general(qd, sp_ref[n], (((1,), (1,)), ((), ())), preferred_element_type=jnp.float32)
        o = o * lax.rsqrt(jnp.mean(o * o, axis=-1, keepdims=True) + NORM_EPS) * nw_ref[...]
        o_ref[rows, :] = o * jax.nn.silu(g_ref[rows, :])


def _hgrn2(proj, lb, norm_w, batch):
    n = proj.shape[0]
    t = n // batch
    dk, H = HG_HEAD_DIM, HG_HEADS
    nc = t // HG_CHUNK

    def part(p):
        return pl.BlockSpec((t, dk), lambda b, h: (b, p * H + h))

    slab = pltpu.VMEM((t, dk), jnp.bfloat16)
    return pl.pallas_call(
        _hgrn2_body,
        grid=(batch, H),
        in_specs=[part(0), part(1), part(2), part(3),
                  pl.BlockSpec((1, dk), lambda b, h: (0, h)),
                  pl.BlockSpec((1, dk), lambda b, h: (0, 0))],
        out_specs=pl.BlockSpec((t, dk), lambda b, h: (b, h)),
        out_shape=jax.ShapeDtypeStruct((n, H * dk), jnp.float32),
        scratch_shapes=[slab, slab, slab, slab, pltpu.VMEM((nc, 1, dk), jnp.float32),
                        pltpu.VMEM((nc, dk, dk), jnp.bfloat16)],
        compiler_params=pltpu.CompilerParams(
            dimension_semantics=("arbitrary", "arbitrary"), vmem_limit_bytes=VMEM_LIMIT_BYTES),
        name="hgrn2",
    )(proj, proj, proj, proj, lb.reshape(1, H * dk), norm_w.reshape(1, dk))


GELU_C0 = float(np.sqrt(2.0 / np.pi))
GELU_C1 = GELU_C0 * 0.044715


def _gelu(x):
    half = 0.5 * x
    return half + half * jnp.tanh(x * (GELU_C0 + GELU_C1 * (x * x)))


def _head_norm(x, w):
    return x * lax.rsqrt(jnp.mean(x * x, axis=-1, keepdims=True) + NORM_EPS) * w


def _value_slab(v):
    lane = lax.broadcasted_iota(jnp.int32, v.shape, 1)
    return jnp.concatenate([v, jnp.where(lane == 0, 1.0, 0.0)], axis=-1).astype(jnp.bfloat16)


def _kv_prep_body(c_ref, s_ref, w_ref, ksw_ref, kww_ref, kn_ref, vv_ref, cf_ref):
    dh = NSA_HEAD_DIM
    for h in range(NSA_KV_HEADS):
        k_cols = slice(h * dh, (h + 1) * dh)
        v_cols = slice(KV_W + h * dh, KV_W + (h + 1) * dh)
        cf_ref[0, h] = c_ref[:, k_cols]
        cf_ref[1, h] = c_ref[:, v_cols]
        kn_ref[0, h] = _head_norm(s_ref[:, k_cols], ksw_ref[...]).astype(jnp.bfloat16)
        vv_ref[0, h] = _value_slab(s_ref[:, v_cols])
        kn_ref[1, h] = _head_norm(w_ref[:, k_cols], kww_ref[...]).astype(jnp.bfloat16)
        vv_ref[1, h] = _value_slab(w_ref[:, v_cols])


def _kv_prep(proj, ks_w, kw_w, tm):
    n = proj.shape[0]
    dh = NSA_HEAD_DIM
    pair = 2 * KV_W
    col0 = KV_COL0 // pair
    head_block = pl.BlockSpec((2, NSA_KV_HEADS, tm, dh), lambda i: (0, 0, i, 0))
    w_spec = pl.BlockSpec((1, dh), lambda i: (0, 0))
    return pl.pallas_call(
        _kv_prep_body,
        grid=(n // tm,),
        in_specs=[pl.BlockSpec((tm, pair), lambda i: (i, col0)),
                  pl.BlockSpec((tm, pair), lambda i: (i, col0 + 1)),
                  pl.BlockSpec((tm, pair), lambda i: (i, col0 + 2)),
                  w_spec, w_spec],
        out_specs=[head_block, pl.BlockSpec((2, NSA_KV_HEADS, tm, 2 * dh), lambda i: (0, 0, i, 0)), head_block],
        out_shape=[jax.ShapeDtypeStruct((2, NSA_KV_HEADS, n, dh), jnp.bfloat16),
                   jax.ShapeDtypeStruct((2, NSA_KV_HEADS, n, 2 * dh), jnp.bfloat16),
                   jax.ShapeDtypeStruct((2, NSA_KV_HEADS, n, dh), jnp.float32)],
        compiler_params=pltpu.CompilerParams(
            dimension_semantics=("arbitrary",), vmem_limit_bytes=VMEM_LIMIT_BYTES),
        name="kv_prep",
    )(proj, proj, proj, ks_w.reshape(1, dh), kw_w.reshape(1, dh))


def _cmp_kv_body(r_ref, pos_ref, w1_ref, w2_ref, nw_ref, o_ref):
    half = CMP_STRIDE * NSA_HEAD_DIM
    n_strips = r_ref.shape[2]
    for kind in range(2):
        strips = r_ref[kind, 0]
        top = (strips + pos_ref[kind, 0:1, :]).astype(jnp.bfloat16)
        bot = (strips + pos_ref[kind, 1:2, :]).astype(jnp.bfloat16)
        a = jnp.dot(top, w1_ref[kind, :half, :], preferred_element_type=jnp.float32)
        b = jnp.dot(bot, w1_ref[kind, half:, :], preferred_element_type=jnp.float32)
        hid = _gelu(a + pltpu.roll(b, n_strips - 1, 0))
        out = jnp.dot(hid.astype(jnp.bfloat16), w2_ref[kind], preferred_element_type=jnp.float32)
        if kind == 0:
            out = _head_norm(out, nw_ref[...])
        o_ref[kind, 0, 0] = out.astype(jnp.bfloat16)


def _cmp_kv(cf, pos, w1, w2, kc_w, batch):
    _, hkv, n, dh = cf.shape
    t = n // batch
    n_strips = t // CMP_STRIDE
    strips = cf.reshape(2, hkv, n // CMP_STRIDE, CMP_STRIDE * dh)
    return pl.pallas_call(
        _cmp_kv_body,
        grid=(batch, hkv),
        in_specs=[pl.BlockSpec((2, 1, n_strips, CMP_STRIDE * dh), lambda b, g: (0, g, b, 0)),
                  pl.BlockSpec(pos.shape, lambda b, g: (0, 0, 0)),
                  pl.BlockSpec(w1.shape, lambda b, g: (0, 0, 0)),
                  pl.BlockSpec(w2.shape, lambda b, g: (0, 0, 0)),
                  pl.BlockSpec((1, dh), lambda b, g: (0, 0))],
        out_specs=pl.BlockSpec((2, 1, 1, n_strips, dh), lambda b, g: (0, g, b, 0, 0)),
        out_shape=jax.ShapeDtypeStruct((2, hkv, batch, n_strips, dh), jnp.bfloat16),
        compiler_params=pltpu.CompilerParams(
            dimension_semantics=("arbitrary", "arbitrary"), vmem_limit_bytes=VMEM_LIMIT_BYTES),
        name="cmp_kv",
    )(strips, pos, w1, w2, kc_w.reshape(1, dh))


def _flash_step(qs_ref, k, v, slope_ref, m_ref, acc_ref, base, mask_fn, tq):
    tk = k.shape[0]
    g = NSA_GROUP
    dist = base + lax.broadcasted_iota(jnp.int32, (tq, tk), 0) - lax.broadcasted_iota(jnp.int32, (tq, tk), 1)
    off = jnp.where(mask_fn(dist), 0.0, MASK_OFF)[None]
    s = lax.dot_general(qs_ref[...], k, (((1,), (1,)), ((), ())),
                        preferred_element_type=jnp.float32).reshape(g, tq, tk)
    s = (s - off) - slope_ref[0] * dist.astype(jnp.float32)[None]
    m_old = m_ref[...].reshape(g, tq, 1)
    m_new = jnp.maximum(m_old, jnp.max(s, axis=-1, keepdims=True))
    p = jnp.exp2(s - m_new)
    alpha = jnp.exp2(m_old - m_new)
    pv = jnp.dot(p.reshape(g * tq, tk).astype(jnp.bfloat16), v, preferred_element_type=jnp.float32)
    acc_ref[...] = alpha.reshape(g * tq, 1) * acc_ref[...] + pv
    m_ref[...] = m_new.reshape(g * tq, 1)


def _nsa_attn_body(q_ref, gate_ref, kc_ref, vc_ref, ks_ref, vs_ref, kw_ref, vw_ref, qw_ref, slope_ref, o_ref,
                   qs_ref, sel_ref, oc_ref, ms_ref, as_ref, mw_ref, aw_ref, *, tq, tk, n_sel):
    i = pl.program_id(2)
    j = pl.program_id(3)
    G, dh = NSA_GROUP, NSA_HEAD_DIM
    last_j = (i * tq + tq - 1) // tk

    @pl.when(j == 0)
    def _():
        x = q_ref[...]
        for r in range(G):
            qn = _head_norm(x[:, r * dh:(r + 1) * dh], qw_ref[...]) * (dh ** -0.5 * LOG2E)
            qs_ref[r * tq:(r + 1) * tq, :] = qn.astype(jnp.bfloat16)
        for ref in (ms_ref, mw_ref):
            ref[...] = jnp.full(ref.shape, NEG_INF, jnp.float32)
        for ref in (as_ref, aw_ref):
            ref[...] = jnp.zeros(ref.shape, jnp.float32)

        n_cmp_pad = kc_ref.shape[-2]
        t_pos = i * tq + lax.broadcasted_iota(jnp.int32, (tq, n_cmp_pad), 0)
        blk = lax.broadcasted_iota(jnp.int32, (tq, n_cmp_pad), 1)
        dist = t_pos - (blk * CMP_STRIDE + CMP_BLOCK - 1)
        valid = (dist >= 0) & (blk < n_cmp_pad - 1)
        distf = dist.astype(jnp.float32)
        cb = lax.broadcasted_iota(jnp.int32, (n_sel, n_cmp_pad), 1) * CMP_STRIDE
        sb = lax.broadcasted_iota(jnp.int32, (n_sel, n_cmp_pad), 0) * SEL_BLOCK
        overlap_t = ((cb < sb + SEL_BLOCK) & (cb + CMP_BLOCK > sb)).astype(jnp.bfloat16)
        imp = None
        for r in range(G):
            rows = slice(r * tq, (r + 1) * tq)
            s = lax.dot_general(qs_ref[rows, :], kc_ref[0, 0, 0], (((1,), (1,)), ((), ())),
                                preferred_element_type=jnp.float32)
            s = jnp.where(valid, s - slope_ref[0, r] * distf, NEG_INF)
            p = jnp.where(valid, jnp.exp2(s - jnp.max(s, axis=-1, keepdims=True)), 0.0)
            denom = jnp.maximum(jnp.sum(p, axis=-1, keepdims=True), 1e-30)
            p = (p / denom).astype(jnp.bfloat16)
            oc_ref[rows, :] = jnp.dot(p, vc_ref[0, 0, 0], preferred_element_type=jnp.float32)
            part = lax.dot_general(overlap_t, p, (((1,), (1,)), ((), ())), preferred_element_type=jnp.float32)
            imp = part if imp is None else imp + part

        cur = (i * tq + lax.broadcasted_iota(jnp.int32, (n_sel, tq), 1)) // SEL_BLOCK
        jb = lax.broadcasted_iota(jnp.int32, (n_sel, tq), 0)
        forced = (jb == 0) | ((jb <= cur) & (jb > cur - N_LOCAL))
        imp = jnp.where(forced, FORCE_SCORE, jnp.where(jb > cur, -FORCE_SCORE, imp))
        rank = jnp.zeros((n_sel, tq), jnp.float32)
        for c in range(n_sel):
            row = imp[c:c + 1, :]
            ahead = (row > imp) | ((row == imp) & (jb > c))
            rank = rank + jnp.where(ahead, 1.0, 0.0)
        sel_t = jnp.where(rank < min(N_SELECT, n_sel), 1.0, 0.0)
        sel_t = jnp.concatenate([sel_t, jnp.zeros((sel_ref.shape[1] - n_sel, tq), jnp.float32)], axis=0)
        sel_ref[...] = sel_t.T.astype(jnp.bfloat16)

    @pl.when(j <= last_j)
    def _():
        n_pad = sel_ref.shape[1]
        blk = lax.broadcasted_iota(jnp.int32, (n_pad, tk), 0)
        col_blk = j * (tk // SEL_BLOCK) + lax.broadcasted_iota(jnp.int32, (n_pad, tk), 1) // SEL_BLOCK
        expand = (blk == col_blk).astype(jnp.bfloat16)

        def mask(dist):
            picked = jnp.dot(sel_ref[...], expand, preferred_element_type=jnp.float32)
            return (dist >= 0) & (picked > 0.5)

        _flash_step(qs_ref, ks_ref[0, 0], vs_ref[0, 0], slope_ref, ms_ref, as_ref,
                    i * tq - j * tk, mask, tq)

    @pl.when((j <= last_j) & (j * tk + tk - 1 >= i * tq - (WINDOW - 1)))
    def _():
        _flash_step(qs_ref, kw_ref[0, 0], vw_ref[0, 0], slope_ref, mw_ref, aw_ref,
                    i * tq - j * tk, lambda dist: (dist >= 0) & (dist < WINDOW), tq)

    @pl.when(j == pl.num_programs(3) - 1)
    def _():
        gates = jax.nn.sigmoid(gate_ref[0])
        o_sel = as_ref[:, :dh] / as_ref[:, dh:dh + 1]
        o_win = aw_ref[:, :dh] / aw_ref[:, dh:dh + 1]
        o_cmp = oc_ref[...]
        outs = []
        for r in range(G):
            rows = slice(r * tq, (r + 1) * tq)
            outs.append(gates[:, 3 * r:3 * r + 1] * o_cmp[rows] + gates[:, 3 * r + 1:3 * r + 2] * o_sel[rows]
                        + gates[:, 3 * r + 2:3 * r + 3] * o_win[rows])
        o_ref[...] = jnp.concatenate(outs, axis=-1)


def _nsa_attn(proj, gates, cmp_kv, kn, vv, q_w, slopes, batch, tq=512, tk=512):
    n = proj.shape[0]
    t = n // batch
    G, dh, hkv = NSA_GROUP, NSA_HEAD_DIM, NSA_KV_HEADS
    n_cmp_pad = cmp_kv.shape[-2]
    n_sel = t // SEL_BLOCK
    q_blk0 = Q_COL0 // (G * dh)

    def kv_map(kind):
        def index(b, g, i, j):
            return (kind, g, b * (t // tk) + jnp.minimum(j, (i * tq + tq - 1) // tk), 0)
        return index

    def cmp_map(kind):
        return lambda b, g, i, j: (kind, g, b, 0, 0)

    kv_block = (1, 1, tk, dh)
    cmp_block = (1, 1, 1, n_cmp_pad, dh)
    v_block = (1, 1, tk, 2 * dh)
    slab = pltpu.VMEM((G * tq, dh), jnp.float32)
    wide = pltpu.VMEM((G * tq, 2 * dh), jnp.float32)
    stat = pltpu.VMEM((G * tq, 1), jnp.float32)
    return pl.pallas_call(
        functools.partial(_nsa_attn_body, tq=tq, tk=tk, n_sel=n_sel),
        grid=(batch, hkv, t // tq, t // tk),
        in_specs=[
            pl.BlockSpec((tq, G * dh), lambda b, g, i, j: (b * (t // tq) + i, q_blk0 + g)),
            pl.BlockSpec((1, tq, 3 * G), lambda b, g, i, j: (g, b * (t // tq) + i, 0)),
            pl.BlockSpec(cmp_block, cmp_map(0)), pl.BlockSpec(cmp_block, cmp_map(1)),
            pl.BlockSpec(kv_block, kv_map(0)), pl.BlockSpec(v_block, kv_map(0)),
            pl.BlockSpec(kv_block, kv_map(1)), pl.BlockSpec(v_block, kv_map(1)),
            pl.BlockSpec((1, dh), lambda b, g, i, j: (0, 0)),
            pl.BlockSpec((1, G, 1, 1), lambda b, g, i, j: (g, 0, 0, 0)),
        ],
        out_specs=pl.BlockSpec((tq, G * dh), lambda b, g, i, j: (b * (t // tq) + i, g)),
        out_shape=jax.ShapeDtypeStruct((n, hkv * G * dh), jnp.float32),
        scratch_shapes=[pltpu.VMEM((G * tq, dh), jnp.bfloat16), pltpu.VMEM((tq, LANES), jnp.bfloat16),
                        slab, stat, wide, stat, wide],
        compiler_params=pltpu.CompilerParams(
            dimension_semantics=("arbitrary",) * 4, vmem_limit_bytes=VMEM_LIMIT_BYTES),
        name="nsa_attn",
    )(proj, gates, cmp_kv, cmp_kv, kn, vv, kn, vv, q_w.reshape(1, dh), slopes)


def _nsa(proj, batch, q_norm_w, kc_norm_w, ks_norm_w, kw_norm_w, pos_k, pos_v, w_ck1, w_ck2, w_cv1, w_cv2):
    n = proj.shape[0]
    G, hkv = NSA_GROUP, NSA_KV_HEADS
    kn, vv, cf = _kv_prep(proj, ks_norm_w, kw_norm_w, 512)
    half = CMP_STRIDE * NSA_HEAD_DIM
    pos = jnp.stack([pos_k.reshape(2, half), pos_v.reshape(2, half)])
    w1 = jnp.stack([w_ck1, w_cv1]).astype(jnp.bfloat16)
    w2 = jnp.stack([w_ck2, w_cv2]).astype(jnp.bfloat16)
    cmp_kv = _cmp_kv(cf, pos, w1, w2, kc_norm_w, batch)
    gates = proj[:, GATE_COL0:GATE_COL0 + N_GATES].reshape(n, hkv, 3 * G).transpose(1, 0, 2)
    slopes = jnp.asarray(2.0 ** (-8.0 * np.arange(1, NSA_Q_HEADS + 1) / NSA_Q_HEADS), jnp.float32) * LOG2E
    return _nsa_attn(proj, gates, cmp_kv, kn, vv, q_norm_w, slopes.reshape(hkv, G, 1, 1), batch)


BIG_NEG = -3.0e38
PEER_CAND = [(a, b) for a in range(PEER_TOPK) for b in range(PEER_TOPK) if (a + 1) * (b + 1) <= PEER_TOPK]
PEER_CAND_ROWS = -(-len(PEER_CAND) // 8) * 8


def _peer_route_body(h_ref, g_ref, wq_ref, keys_ref, xt_ref, rk2_ref, a2_ref, cnt_ref, r_ref,
                     qt_ref, sc_ref, cur_ref, top_ref, cand_ref):
    x = h_ref[...]
    hn = x * lax.rsqrt(jnp.mean(x * x, axis=-1, keepdims=True) + NORM_EPS) * g_ref[...]
    hnt = hn.T.astype(jnp.bfloat16)
    xt_ref[...] = hnt
    qt_ref[...] = jnp.dot(wq_ref[...], hnt, preferred_element_type=jnp.float32).astype(jnp.bfloat16)
    cand_ref[...] = jnp.full(cand_ref.shape, BIG_NEG, jnp.float32)
    kd = PEER_KEY_DIM // 2
    n_parts = 2 * PEER_HEADS
    for hp in range(n_parts):
        s = jnp.dot(keys_ref[hp], qt_ref[hp * kd:(hp + 1) * kd, :], preferred_element_type=jnp.float32)
        sc_ref[hp] = s
        cur_ref[hp] = s

    def extract(k, carry):
        cur = cur_ref[...]
        mk = jnp.max(cur, axis=1, keepdims=True)
        top_ref[k] = mk
        cur_ref[...] = jnp.where(cur == mk, BIG_NEG, cur)
        return carry

    lax.fori_loop(0, PEER_TOPK, extract, 0)

    def head(hd, carry):
        t1 = [top_ref[a, 2 * hd] for a in range(PEER_TOPK)]
        t2 = [top_ref[b, 2 * hd + 1] for b in range(PEER_TOPK)]
        cmax = t1[0] + t2[0]
        cands = [t1[a] + t2[b] for a, b in PEER_CAND]
        for i, c in enumerate(cands):
            cand_ref[i:i + 1, :] = c
        call = cand_ref[...]
        n_gt = jnp.zeros(call.shape, jnp.float32)
        for c in cands:
            n_gt = n_gt + jnp.where(c > call, 1.0, 0.0)
        tau = jnp.min(jnp.where(n_gt <= PEER_TOPK - 1, call, -BIG_NEG), axis=0, keepdims=True)
        z = jnp.sum(jnp.where(call >= tau, jnp.exp(call - cmax), 0.0), axis=0, keepdims=True)
        count = [None] * PEER_TOPK
        for (a, b), c in zip(PEER_CAND, cands):
            hit = jnp.where(c >= tau, 1.0, 0.0)
            count[a] = hit if count[a] is None else count[a] + hit
        s1 = sc_ref[2 * hd]
        s2 = sc_ref[2 * hd + 1]
        cnt = jnp.zeros(s1.shape, jnp.float32)
        rank2 = jnp.zeros(s2.shape, jnp.float32)
        for a in range(PEER_TOPK):
            cnt = jnp.where(s1 == t1[a], count[a], cnt)
            rank2 = rank2 + jnp.where(t2[a] > s2, 1.0, 0.0)
        rk2_ref[hd] = rank2.astype(jnp.bfloat16)
        a2_ref[hd] = jnp.exp(s2 - t2[0]).astype(jnp.bfloat16)
        cnt_ref[hd] = cnt
        r_ref[hd] = jnp.exp(s1 - t1[0]) / z
        return carry

    lax.fori_loop(0, PEER_HEADS, head, 0)


def _peer_route(h, gain, wq_t, keys, tt):
    n, d = h.shape
    hp, nk, kd = keys.shape
    stat = jax.ShapeDtypeStruct((PEER_HEADS, nk, n), jnp.float32)
    stat_bf16 = jax.ShapeDtypeStruct((PEER_HEADS, nk, n), jnp.bfloat16)
    stat_spec = pl.BlockSpec((PEER_HEADS, nk, tt), lambda i: (0, 0, i))
    return pl.pallas_call(
        _peer_route_body,
        grid=(n // tt,),
        in_specs=[
            pl.BlockSpec((tt, d), lambda i: (i, 0)),
            pl.BlockSpec((1, d), lambda i: (0, 0)),
            pl.BlockSpec(wq_t.shape, lambda i: (0, 0)),
            pl.BlockSpec(keys.shape, lambda i: (0, 0, 0)),
        ],
        out_specs=[pl.BlockSpec((d, tt), lambda i: (0, i)), stat_spec, stat_spec, stat_spec, stat_spec],
        out_shape=[jax.ShapeDtypeStruct((d, n), jnp.bfloat16), stat_bf16, stat_bf16, stat, stat],
        scratch_shapes=[pltpu.VMEM((wq_t.shape[0], tt), jnp.bfloat16),
                        pltpu.VMEM((hp, nk, tt), jnp.float32), pltpu.VMEM((hp, nk, tt), jnp.float32),
                        pltpu.VMEM((PEER_TOPK, hp, 1, tt), jnp.float32),
                        pltpu.VMEM((PEER_CAND_ROWS, tt), jnp.float32)],
        compiler_params=pltpu.CompilerParams(
            dimension_semantics=("arbitrary",), vmem_limit_bytes=VMEM_LIMIT_BYTES),
        name="peer_route",
    )(h, gain.reshape(1, d), wq_t, keys)


def _peer_experts_body(xt_ref, u_ref, vt_ref, rk2_ref, a2_ref, cnt_ref, r_ref, o_ref, *, n_i1):
    @pl.when(pl.program_id(1) == 0)
    def _():
        o_ref[...] = jnp.zeros(o_ref.shape, jnp.float32)

    act = _gelu(jnp.dot(u_ref[...], xt_ref[...], preferred_element_type=jnp.float32))
    parts = []
    for i1 in range(n_i1):
        w = None
        for hd in range(PEER_HEADS):
            picked = rk2_ref[hd] < cnt_ref[hd, i1:i1 + 1, :].astype(jnp.bfloat16)
            term = jnp.where(picked, a2_ref[hd], 0.0) * r_ref[hd, i1:i1 + 1, :].astype(jnp.bfloat16)
            w = term if w is None else w + term
        parts.append(w * act[i1 * N_KEYS:(i1 + 1) * N_KEYS].astype(jnp.bfloat16))
    p = jnp.concatenate(parts, axis=0)
    o_ref[...] += jnp.dot(vt_ref[...], p, preferred_element_type=jnp.float32)


def _peer_experts(xt, u, vt, rk2, a2, cnt, r, tt, te):
    d, n = xt.shape
    e = u.shape[0]
    n_i1 = te // N_KEYS
    full = pl.BlockSpec((PEER_HEADS, N_KEYS, tt), lambda i, j: (0, 0, i))
    part = pl.BlockSpec((PEER_HEADS, n_i1, tt), lambda i, j: (0, j, i))
    return pl.pallas_call(
        functools.partial(_peer_experts_body, n_i1=n_i1),
        grid=(n // tt, e // te),
        in_specs=[
            pl.BlockSpec((d, tt), lambda i, j: (0, i)),
            pl.BlockSpec((te, d), lambda i, j: (j, 0)),
            pl.BlockSpec((d, te), lambda i, j: (0, j)),
            full, full, part, part,
        ],
        out_specs=pl.BlockSpec((d, tt), lambda i, j: (0, i)),
        out_shape=jax.ShapeDtypeStruct((d, n), jnp.float32),
        compiler_params=pltpu.CompilerParams(
            dimension_semantics=("arbitrary", "arbitrary"), vmem_limit_bytes=VMEM_LIMIT_BYTES),
        name="peer_experts",
    )(xt, u, vt, rk2, a2, cnt, r)


def _peer(h, gain, w_q, sub_keys, u_tab, v_tab):
    keys = sub_keys.reshape(PEER_HEADS * 2, N_KEYS, PEER_KEY_DIM // 2).astype(jnp.bfloat16)
    xt, rk2, a2, cnt, r = _peer_route(h, gain, w_q.T.astype(jnp.bfloat16), keys, 256)
    out_t = _peer_experts(xt, u_tab.astype(jnp.bfloat16), v_tab.T.astype(jnp.bfloat16),
                          rk2, a2, cnt, r, 512, 1024)
    return out_t.T


def kernel(x, norm1_w, w_in, hg_lb_logits, hg_norm_w, q_norm_w, kc_norm_w, ks_norm_w, kw_norm_w,
           cmp_pos_k, cmp_pos_v, w_ck1, w_ck2, w_cv1, w_cv2, w_out, norm2_w,
           peer_w_q, peer_sub_keys, peer_u, peer_v):
    B, T, D = x.shape
    n = B * T
    layer = 0
    lower_bounds = jnp.cumsum(jax.nn.softmax(hg_lb_logits, axis=0), axis=0)
    xt = x.reshape(n, D)

    w_in_b = jnp.pad(w_in[layer].astype(jnp.bfloat16), ((0, 0), (0, IN_COLS_PADDED - IN_COLS)))
    proj = _norm_matmul(xt, norm1_w[layer], w_in_b, 512, 1024)
    hg_out = _hgrn2(proj, lower_bounds[layer], hg_norm_w[layer], B)
    nsa_out = _nsa(proj, B, q_norm_w[layer], kc_norm_w[layer], ks_norm_w[layer], kw_norm_w[layer],
                   cmp_pos_k[layer], cmp_pos_v[layer], w_ck1[layer], w_ck2[layer], w_cv1[layer], w_cv2[layer])
    h = _out_proj(hg_out, nsa_out, w_out[layer].astype(jnp.bfloat16), xt, 512, 1024)

    y = h + _peer(h, norm2_w[layer], peer_w_q[layer], peer_sub_keys[layer], peer_u[layer], peer_v[layer])
    return y.reshape(B, T, D)
```

```python
import functools

import jax
import jax.numpy as jnp
import numpy as np
from jax import lax
from jax.experimental import pallas as pl
from jax.experimental.pallas import tpu as pltpu

D_MODEL = 2048
HG_WIDTH = 1024
HG_HEAD_DIM = 128
HG_HEADS = 8
HG_CHUNK = 64
NSA_WIDTH = 1024
NSA_HEAD_DIM = 64
NSA_Q_HEADS = 16
NSA_KV_HEADS = 4
NSA_GROUP = 4
CMP_BLOCK = 32
CMP_STRIDE = 16
CMP_HIDDEN = 256
SEL_BLOCK = 64
N_SELECT = 16
N_LOCAL = 2
WINDOW = 512
FORCE_SCORE = 1e9
NEG_INF = -1e30
MASK_OFF = 3e30
LOG2E = float(np.log2(np.e))
PEER_HEADS = 8
N_KEYS = 128
PEER_KEY_DIM = 256
PEER_TOPK = 16
NORM_EPS = 1e-6
KV_W = NSA_KV_HEADS * NSA_HEAD_DIM
N_GATES = 3 * NSA_Q_HEADS
IN_SIZES = [HG_WIDTH] * 4 + [NSA_WIDTH] + [KV_W] * 6 + [N_GATES]
IN_COLS = sum(IN_SIZES)
IN_COLS_PADDED = 7168
Q_COL0 = 4 * HG_WIDTH
KV_COL0 = Q_COL0 + NSA_WIDTH
GATE_COL0 = KV_COL0 + 6 * KV_W

VMEM_LIMIT_BYTES = 48 * 1024 * 1024
LANES = 128


def _norm_matmul_body(x_ref, g_ref, w_ref, o_ref, xn_ref):
    @pl.when(pl.program_id(1) == 0)
    def _():
        x = x_ref[...]
        r = lax.rsqrt(jnp.mean(x * x, axis=-1, keepdims=True) + NORM_EPS)
        xn_ref[...] = (x * r * g_ref[...]).astype(jnp.bfloat16)

    o_ref[...] = jnp.dot(xn_ref[...], w_ref[...], preferred_element_type=jnp.float32).astype(o_ref.dtype)


def _norm_matmul(x, gain, w, tm, tn, out_dtype=jnp.float32):
    m, k = x.shape
    n = w.shape[1]
    return pl.pallas_call(
        _norm_matmul_body,
        grid=(m // tm, n // tn),
        in_specs=[
            pl.BlockSpec((tm, k), lambda i, j: (i, 0)),
            pl.BlockSpec((1, k), lambda i, j: (0, 0)),
            pl.BlockSpec((k, tn), lambda i, j: (0, j)),
        ],
        out_specs=pl.BlockSpec((tm, tn), lambda i, j: (i, j)),
        out_shape=jax.ShapeDtypeStruct((m, n), out_dtype),
        scratch_shapes=[pltpu.VMEM((tm, k), jnp.bfloat16)],
        compiler_params=pltpu.CompilerParams(
            dimension_semantics=("arbitrary", "arbitrary"), vmem_limit_bytes=VMEM_LIMIT_BYTES),
        name="norm_matmul",
    )(x, gain.reshape(1, k), w)


def _out_proj_body(a1_ref, a2_ref, w1_ref, w2_ref, r_ref, o_ref):
    acc = jnp.dot(a1_ref[...].astype(jnp.bfloat16), w1_ref[...], preferred_element_type=jnp.float32)
    acc = acc + jnp.dot(a2_ref[...].astype(jnp.bfloat16), w2_ref[...], preferred_element_type=jnp.float32)
    o_ref[...] = r_ref[...] + acc


def _out_proj(a1, a2, w, res, tm, tn):
    m, k1 = a1.shape
    k2 = a2.shape[1]
    n = w.shape[1]
    return pl.pallas_call(
        _out_proj_body,
        grid=(m // tm, n // tn),
        in_specs=[
            pl.BlockSpec((tm, k1), lambda i, j: (i, 0)),
            pl.BlockSpec((tm, k2), lambda i, j: (i, 0)),
            pl.BlockSpec((k1, tn), lambda i, j: (0, j)),
            pl.BlockSpec((k2, tn), lambda i, j: (k1 // k2, j)),
            pl.BlockSpec((tm, tn), lambda i, j: (i, j)),
        ],
        out_specs=pl.BlockSpec((tm, tn), lambda i, j: (i, j)),
        out_shape=jax.ShapeDtypeStruct((m, n), jnp.float32),
        compiler_params=pltpu.CompilerParams(
            dimension_semantics=("arbitrary", "arbitrary"), vmem_limit_bytes=VMEM_LIMIT_BYTES),
        name="out_proj",
    )(a1, a2, w, w, res)


def _hgrn2_body(q_ref, f_ref, v_ref, g_ref, lb_ref, nw_ref, o_ref, qd_ref, kd_ref, ku_ref, vb_ref, dec_ref, sp_ref):
    t, dk = q_ref.shape
    C = HG_CHUNK
    nc = t // C
    lb = lb_ref[...]
    f = lb + (1.0 - lb) * jax.nn.sigmoid(f_ref[...])
    kf = 1.0 - f
    b = jnp.log(f)
    row = lax.broadcasted_iota(jnp.int32, (t, dk), 0) % C
    shift = 1
    while shift < C:
        b = b + jnp.where(row >= shift, pltpu.roll(b, shift, 0), 0.0)
        shift *= 2
    b3 = b.reshape(nc, C, dk)
    b_end = b3[:, C - 1:C, :]
    qd_ref[...] = (q_ref[...] * jnp.exp(b)).astype(jnp.bfloat16)
    kd_ref[...] = (kf * jnp.exp(-b)).astype(jnp.bfloat16)
    ku_ref[...] = (kf.reshape(nc, C, dk) * jnp.exp(b_end - b3)).reshape(t, dk).astype(jnp.bfloat16)
    vb_ref[...] = v_ref[...].astype(jnp.bfloat16)
    dec_ref[...] = jnp.exp(b_end)

    st = jnp.zeros((dk, dk), jnp.float32)
    for n in range(nc):
        rows = slice(n * C, (n + 1) * C)
        sp_ref[n] = st.astype(jnp.bfloat16)
        upd_t = lax.dot_general(vb_ref[rows, :], ku_ref[rows, :], (((0,), (0,)), ((), ())),
                                preferred_element_type=jnp.float32)
        st = dec_ref[n] * st + upd_t

    causal = lax.broadcasted_iota(jnp.int32, (C, C), 0) >= lax.broadcasted_iota(jnp.int32, (C, C), 1)
    for n in range(nc):
        rows = slice(n * C, (n + 1) * C)
        qd = qd_ref[rows, :]
        attn = lax.dot_general(qd, kd_ref[rows, :], (((1,), (1,)), ((), ())), preferred_element_type=jnp.float32)
        attn = jnp.where(causal, attn, 0.0).astype(jnp.bfloat16)
        o = jnp.dot(attn, vb_ref[rows, :], preferred_element_type=jnp.float32)
        o = o + lax.dot_general(qd, sp_ref[n], (((1,), (1,)), ((), ())), preferred_element_type=jnp.float32)
        o = o * lax.rsqrt(jnp.mean(o * o, axis=-1, keepdims=True) + NORM_EPS) * nw_ref[...]
        o_ref[rows, :] = o * jax.nn.silu(g_ref[rows, :])


def _hgrn2(proj, lb, norm_w, batch):
    n = proj.shape[0]
    t = n // batch
    dk, H = HG_HEAD_DIM, HG_HEADS
    nc = t // HG_CHUNK

    def part(p):
        return pl.BlockSpec((t, dk), lambda b, h: (b, p * H + h))

    slab = pltpu.VMEM((t, dk), jnp.bfloat16)
    return pl.pallas_call(
        _hgrn2_body,
        grid=(batch, H),
        in_specs=[part(0), part(1), part(2), part(3),
                  pl.BlockSpec((1, dk), lambda b, h: (0, h)),
                  pl.BlockSpec((1, dk), lambda b, h: (0, 0))],
        out_specs=pl.BlockSpec((t, dk), lambda b, h: (b, h)),
        out_shape=jax.ShapeDtypeStruct((n, H * dk), jnp.float32),
        scratch_shapes=[slab, slab, slab, slab, pltpu.VMEM((nc, 1, dk), jnp.float32),
                        pltpu.VMEM((nc, dk, dk), jnp.bfloat16)],
        compiler_params=pltpu.CompilerParams(
            dimension_semantics=("arbitrary", "arbitrary"), vmem_limit_bytes=VMEM_LIMIT_BYTES),
        name="hgrn2",
    )(proj, proj, proj, proj, lb.reshape(1, H * dk), norm_w.reshape(1, dk))


GELU_C0 = float(np.sqrt(2.0 / np.pi))
GELU_C1 = GELU_C0 * 0.044715


def _gelu(x):
    half = 0.5 * x
    return half + half * jnp.tanh(x * (GELU_C0 + GELU_C1 * (x * x)))


def _head_norm(x, w):
    return x * lax.rsqrt(jnp.mean(x * x, axis=-1, keepdims=True) + NORM_EPS) * w


def _value_slab(v):
    lane = lax.broadcasted_iota(jnp.int32, v.shape, 1)
    return jnp.concatenate([v, jnp.where(lane == 0, 1.0, 0.0)], axis=-1).astype(jnp.bfloat16)


def _kv_prep_body(c_ref, s_ref, w_ref, ksw_ref, kww_ref, kn_ref, vv_ref, cf_ref):
    dh = NSA_HEAD_DIM
    for h in range(NSA_KV_HEADS):
        k_cols = slice(h * dh, (h + 1) * dh)
        v_cols = slice(KV_W + h * dh, KV_W + (h + 1) * dh)
        cf_ref[0, h] = c_ref[:, k_cols]
        cf_ref[1, h] = c_ref[:, v_cols]
        kn_ref[0, h] = _head_norm(s_ref[:, k_cols], ksw_ref[...]).astype(jnp.bfloat16)
        vv_ref[0, h] = _value_slab(s_ref[:, v_cols])
        kn_ref[1, h] = _head_norm(w_ref[:, k_cols], kww_ref[...]).astype(jnp.bfloat16)
        vv_ref[1, h] = _value_slab(w_ref[:, v_cols])


def _kv_prep(proj, ks_w, kw_w, tm):
    n = proj.shape[0]
    dh = NSA_HEAD_DIM
    pair = 2 * KV_W
    col0 = KV_COL0 // pair
    head_block = pl.BlockSpec((2, NSA_KV_HEADS, tm, dh), lambda i: (0, 0, i, 0))
    w_spec = pl.BlockSpec((1, dh), lambda i: (0, 0))
    return pl.pallas_call(
        _kv_prep_body,
        grid=(n // tm,),
        in_specs=[pl.BlockSpec((tm, pair), lambda i: (i, col0)),
                  pl.BlockSpec((tm, pair), lambda i: (i, col0 + 1)),
                  pl.BlockSpec((tm, pair), lambda i: (i, col0 + 2)),
                  w_spec, w_spec],
        out_specs=[head_block, pl.BlockSpec((2, NSA_KV_HEADS, tm, 2 * dh), lambda i: (0, 0, i, 0)), head_block],
        out_shape=[jax.ShapeDtypeStruct((2, NSA_KV_HEADS, n, dh), jnp.bfloat16),
                   jax.ShapeDtypeStruct((2, NSA_KV_HEADS, n, 2 * dh), jnp.bfloat16),
                   jax.ShapeDtypeStruct((2, NSA_KV_HEADS, n, dh), jnp.float32)],
        compiler_params=pltpu.CompilerParams(
            dimension_semantics=("arbitrary",), vmem_limit_bytes=VMEM_LIMIT_BYTES),
        name="kv_prep",
    )(proj, proj, proj, ks_w.reshape(1, dh), kw_w.reshape(1, dh))


def _cmp_kv_body(r_ref, pos_ref, w1_ref, w2_ref, nw_ref, o_ref):
    half = CMP_STRIDE * NSA_HEAD_DIM
    n_strips = r_ref.shape[2]
    for kind in range(2):
        strips = r_ref[kind, 0]
        top = (strips + pos_ref[kind, 0:1, :]).astype(jnp.bfloat16)
        bot = (strips + pos_ref[kind, 1:2, :]).astype(jnp.bfloat16)
        a = jnp.dot(top, w1_ref[kind, :half, :], preferred_element_type=jnp.float32)
        b = jnp.dot(bot, w1_ref[kind, half:, :], preferred_element_type=jnp.float32)
        hid = _gelu(a + pltpu.roll(b, n_strips - 1, 0))
        out = jnp.dot(hid.astype(jnp.bfloat16), w2_ref[kind], preferred_element_type=jnp.float32)
        if kind == 0:
            out = _head_norm(out, nw_ref[...])
        o_ref[kind, 0, 0] = out.astype(jnp.bfloat16)


def _cmp_kv(cf, pos, w1, w2, kc_w, batch):
    _, hkv, n, dh = cf.shape
    t = n // batch
    n_strips = t // CMP_STRIDE
    strips = cf.reshape(2, hkv, n // CMP_STRIDE, CMP_STRIDE * dh)
    return pl.pallas_call(
        _cmp_kv_body,
        grid=(batch, hkv),
        in_specs=[pl.BlockSpec((2, 1, n_strips, CMP_STRIDE * dh), lambda b, g: (0, g, b, 0)),
                  pl.BlockSpec(pos.shape, lambda b, g: (0, 0, 0)),
                  pl.BlockSpec(w1.shape, lambda b, g: (0, 0, 0)),
                  pl.BlockSpec(w2.shape, lambda b, g: (0, 0, 0)),
                  pl.BlockSpec((1, dh), lambda b, g: (0, 0))],
        out_specs=pl.BlockSpec((2, 1, 1, n_strips, dh), lambda b, g: (0, g, b, 0, 0)),
        out_shape=jax.ShapeDtypeStruct((2, hkv, batch, n_strips, dh), jnp.bfloat16),
        compiler_params=pltpu.CompilerParams(
            dimension_semantics=("arbitrary", "arbitrary"), vmem_limit_bytes=VMEM_LIMIT_BYTES),
        name="cmp_kv",
    )(strips, pos, w1, w2, kc_w.reshape(1, dh))


def _flash_step(qs_ref, k, v, slope_ref, m_ref, acc_ref, base, mask_fn, tq):
    tk = k.shape[0]
    g = NSA_GROUP
    dist = base + lax.broadcasted_iota(jnp.int32, (tq, tk), 0) - lax.broadcasted_iota(jnp.int32, (tq, tk), 1)
    off = jnp.where(mask_fn(dist), 0.0, MASK_OFF)[None]
    s = lax.dot_general(qs_ref[...], k, (((1,), (1,)), ((), ())),
                        preferred_element_type=jnp.float32).reshape(g, tq, tk)
    s = (s - off) - slope_ref[0] * dist.astype(jnp.float32)[None]
    m_old = m_ref[...].reshape(g, tq, 1)
    m_new = jnp.maximum(m_old, jnp.max(s, axis=-1, keepdims=True))
    p = jnp.exp2(s - m_new)
    alpha = jnp.exp2(m_old - m_new)
    pv = jnp.dot(p.reshape(g * tq, tk).astype(jnp.bfloat16), v, preferred_element_type=jnp.float32)
    acc_ref[...] = alpha.reshape(g * tq, 1) * acc_ref[...] + pv
    m_ref[...] = m_new.reshape(g * tq, 1)


def _nsa_attn_body(q_ref, gate_ref, kc_ref, vc_ref, ks_ref, vs_ref, kw_ref, vw_ref, qw_ref, slope_ref, o_ref,
                   qs_ref, sel_ref, oc_ref, ms_ref, as_ref, mw_ref, aw_ref, *, tq, tk, n_sel):
    i = pl.program_id(2)
    j = pl.program_id(3)
    G, dh = NSA_GROUP, NSA_HEAD_DIM
    last_j = (i * tq + tq - 1) // tk

    @pl.when(j == 0)
    def _():
        x = q_ref[...]
        for r in range(G):
            qn = _head_norm(x[:, r * dh:(r + 1) * dh], qw_ref[...]) * (dh ** -0.5 * LOG2E)
            qs_ref[r * tq:(r + 1) * tq, :] = qn.astype(jnp.bfloat16)
        for ref in (ms_ref, mw_ref):
            ref[...] = jnp.full(ref.shape, NEG_INF, jnp.float32)
        for ref in (as_ref, aw_ref):
            ref[...] = jnp.zeros(ref.shape, jnp.float32)

        n_cmp_pad = kc_ref.shape[-2]
        t_pos = i * tq + lax.broadcasted_iota(jnp.int32, (tq, n_cmp_pad), 0)
        blk = lax.broadcasted_iota(jnp.int32, (tq, n_cmp_pad), 1)
        dist = t_pos - (blk * CMP_STRIDE + CMP_BLOCK - 1)
        valid = (dist >= 0) & (blk < n_cmp_pad - 1)
        distf = dist.astype(jnp.float32)
        cb = lax.broadcasted_iota(jnp.int32, (n_sel, n_cmp_pad), 1) * CMP_STRIDE
        sb = lax.broadcasted_iota(jnp.int32, (n_sel, n_cmp_pad), 0) * SEL_BLOCK
        overlap_t = ((cb < sb + SEL_BLOCK) & (cb + CMP_BLOCK > sb)).astype(jnp.bfloat16)
        imp = None
        for r in range(G):
            rows = slice(r * tq, (r + 1) * tq)
            s = lax.dot_general(qs_ref[rows, :], kc_ref[0, 0, 0], (((1,), (1,)), ((), ())),
                                preferred_element_type=jnp.float32)
            s = jnp.where(valid, s - slope_ref[0, r] * distf, NEG_INF)
            p = jnp.where(valid, jnp.exp2(s - jnp.max(s, axis=-1, keepdims=True)), 0.0)
            denom = jnp.maximum(jnp.sum(p, axis=-1, keepdims=True), 1e-30)
            p = (p / denom).astype(jnp.bfloat16)
            oc_ref[rows, :] = jnp.dot(p, vc_ref[0, 0, 0], preferred_element_type=jnp.float32)
            part = lax.dot_general(overlap_t, p, (((1,), (1,)), ((), ())), preferred_element_type=jnp.float32)
            imp = part if imp is None else imp + part

        cur = (i * tq + lax.broadcasted_iota(jnp.int32, (n_sel, tq), 1)) // SEL_BLOCK
        jb = lax.broadcasted_iota(jnp.int32, (n_sel, tq), 0)
        forced = (jb == 0) | ((jb <= cur) & (jb > cur - N_LOCAL))
        imp = jnp.where(forced, FORCE_SCORE, jnp.where(jb > cur, -FORCE_SCORE, imp))
        rank = jnp.zeros((n_sel, tq), jnp.float32)
        for c in range(n_sel):
            row = imp[c:c + 1, :]
            ahead = (row > imp) | ((row == imp) & (jb > c))
            rank = rank + jnp.where(ahead, 1.0, 0.0)
        sel_t = jnp.where(rank < min(N_SELECT, n_sel), 1.0, 0.0)
        sel_t = jnp.concatenate([sel_t, jnp.zeros((sel_ref.shape[1] - n_sel, tq), jnp.float32)], axis=0)
        sel_ref[...] = sel_t.T.astype(jnp.bfloat16)

    @pl.when(j <= last_j)
    def _():
        n_pad = sel_ref.shape[1]
        blk = lax.broadcasted_iota(jnp.int32, (n_pad, tk), 0)
        col_blk = j * (tk // SEL_BLOCK) + lax.broadcasted_iota(jnp.int32, (n_pad, tk), 1) // SEL_BLOCK
        expand = (blk == col_blk).astype(jnp.bfloat16)

        def mask(dist):
            picked = jnp.dot(sel_ref[...], expand, preferred_element_type=jnp.float32)
            return (dist >= 0) & (picked > 0.5)

        _flash_step(qs_ref, ks_ref[0, 0], vs_ref[0, 0], slope_ref, ms_ref, as_ref,
                    i * tq - j * tk, mask, tq)

    @pl.when((j <= last_j) & (j * tk + tk - 1 >= i * tq - (WINDOW - 1)))
    def _():
        _flash_step(qs_ref, kw_ref[0, 0], vw_ref[0, 0], slope_ref, mw_ref, aw_ref,
                    i * tq - j * tk, lambda dist: (dist >= 0) & (dist < WINDOW), tq)

    @pl.when(j == pl.num_programs(3) - 1)
    def _():
        gates = jax.nn.sigmoid(gate_ref[0])
        o_sel = as_ref[:, :dh] / as_ref[:, dh:dh + 1]
        o_win = aw_ref[:, :dh] / aw_ref[:, dh:dh + 1]
        o_cmp = oc_ref[...]
        outs = []
        for r in range(G):
            rows = slice(r * tq, (r + 1) * tq)
            outs.append(gates[:, 3 * r:3 * r + 1] * o_cmp[rows] + gates[:, 3 * r + 1:3 * r + 2] * o_sel[rows]
                        + gates[:, 3 * r + 2:3 * r + 3] * o_win[rows])
        o_ref[...] = jnp.concatenate(outs, axis=-1)


def _nsa_attn(proj, gates, cmp_kv, kn, vv, q_w, slopes, batch, tq=512, tk=512):
    n = proj.shape[0]
    t = n // batch
    G, dh, hkv = NSA_GROUP, NSA_HEAD_DIM, NSA_KV_HEADS
    n_cmp_pad = cmp_kv.shape[-2]
    n_sel = t // SEL_BLOCK
    q_blk0 = Q_COL0 // (G * dh)

    def kv_map(kind):
        def index(b, g, i, j):
            return (kind, g, b * (t // tk) + jnp.minimum(j, (i * tq + tq - 1) // tk), 0)
        return index

    def cmp_map(kind):
        return lambda b, g, i, j: (kind, g, b, 0, 0)

    kv_block = (1, 1, tk, dh)
    cmp_block = (1, 1, 1, n_cmp_pad, dh)
    v_block = (1, 1, tk, 2 * dh)
    slab = pltpu.VMEM((G * tq, dh), jnp.float32)
    wide = pltpu.VMEM((G * tq, 2 * dh), jnp.float32)
    stat = pltpu.VMEM((G * tq, 1), jnp.float32)
    return pl.pallas_call(
        functools.partial(_nsa_attn_body, tq=tq, tk=tk, n_sel=n_sel),
        grid=(batch, hkv, t // tq, t // tk),
        in_specs=[
            pl.BlockSpec((tq, G * dh), lambda b, g, i, j: (b * (t // tq) + i, q_blk0 + g)),
            pl.BlockSpec((1, tq, 3 * G), lambda b, g, i, j: (g, b * (t // tq) + i, 0)),
            pl.BlockSpec(cmp_block, cmp_map(0)), pl.BlockSpec(cmp_block, cmp_map(1)),
            pl.BlockSpec(kv_block, kv_map(0)), pl.BlockSpec(v_block, kv_map(0)),
            pl.BlockSpec(kv_block, kv_map(1)), pl.BlockSpec(v_block, kv_map(1)),
            pl.BlockSpec((1, dh), lambda b, g, i, j: (0, 0)),
            pl.BlockSpec((1, G, 1, 1), lambda b, g, i, j: (g, 0, 0, 0)),
        ],
        out_specs=pl.BlockSpec((tq, G * dh), lambda b, g, i, j: (b * (t // tq) + i, g)),
        out_shape=jax.ShapeDtypeStruct((n, hkv * G * dh), jnp.float32),
        scratch_shapes=[pltpu.VMEM((G * tq, dh), jnp.bfloat16), pltpu.VMEM((tq, LANES), jnp.bfloat16),
                        slab, stat, wide, stat, wide],
        compiler_params=pltpu.CompilerParams(
            dimension_semantics=("arbitrary",) * 4, vmem_limit_bytes=VMEM_LIMIT_BYTES),
        name="nsa_attn",
    )(proj, gates, cmp_kv, cmp_kv, kn, vv, kn, vv, q_w.reshape(1, dh), slopes)


def _nsa(proj, batch, q_norm_w, kc_norm_w, ks_norm_w, kw_norm_w, pos_k, pos_v, w_ck1, w_ck2, w_cv1, w_cv2):
    n = proj.shape[0]
    G, hkv = NSA_GROUP, NSA_KV_HEADS
    kn, vv, cf = _kv_prep(proj, ks_norm_w, kw_norm_w, 512)
    half = CMP_STRIDE * NSA_HEAD_DIM
    pos = jnp.stack([pos_k.reshape(2, half), pos_v.reshape(2, half)])
    w1 = jnp.stack([w_ck1, w_cv1]).astype(jnp.bfloat16)
    w2 = jnp.stack([w_ck2, w_cv2]).astype(jnp.bfloat16)
    cmp_kv = _cmp_kv(cf, pos, w1, w2, kc_norm_w, batch)
    gates = proj[:, GATE_COL0:GATE_COL0 + N_GATES].reshape(n, hkv, 3 * G).transpose(1, 0, 2)
    slopes = jnp.asarray(2.0 ** (-8.0 * np.arange(1, NSA_Q_HEADS + 1) / NSA_Q_HEADS), jnp.float32) * LOG2E
    return _nsa_attn(proj, gates, cmp_kv, kn, vv, q_norm_w, slopes.reshape(hkv, G, 1, 1), batch)


BIG_NEG = -3.0e38
PEER_CAND = [(a, b) for a in range(PEER_TOPK) for b in range(PEER_TOPK) if (a + 1) * (b + 1) <= PEER_TOPK]
PEER_CAND_ROWS = -(-len(PEER_CAND) // 8) * 8


def _peer_route_body(h_ref, g_ref, wq_ref, keys_ref, xt_ref, rk2_ref, a2_ref, cnt_ref, r_ref,
                     qt_ref, sc_ref, cur_ref, top_ref, cand_ref):
    x = h_ref[...]
    hn = x * lax.rsqrt(jnp.mean(x * x, axis=-1, keepdims=True) + NORM_EPS) * g_ref[...]
    hnt = hn.T.astype(jnp.bfloat16)
    xt_ref[...] = hnt
    qt_ref[...] = jnp.dot(wq_ref[...], hnt, preferred_element_type=jnp.float32).astype(jnp.bfloat16)
    cand_ref[...] = jnp.full(cand_ref.shape, BIG_NEG, jnp.float32)
    kd = PEER_KEY_DIM // 2
    n_parts = 2 * PEER_HEADS
    for hp in range(n_parts):
        s = jnp.dot(keys_ref[hp], qt_ref[hp * kd:(hp + 1) * kd, :], preferred_element_type=jnp.float32)
        sc_ref[hp] = s
        cur_ref[hp] = s

    def extract(k, carry):
        cur = cur_ref[...]
        mk = jnp.max(cur, axis=1, keepdims=True)
        top_ref[k] = mk
        cur_ref[...] = jnp.where(cur == mk, BIG_NEG, cur)
        return carry

    lax.fori_loop(0, PEER_TOPK, extract, 0)

    def head(hd, carry):
        t1 = [top_ref[a, 2 * hd] for a in range(PEER_TOPK)]
        t2 = [top_ref[b, 2 * hd + 1] for b in range(PEER_TOPK)]
        cmax = t1[0] + t2[0]
        cands = [t1[a] + t2[b] for a, b in PEER_CAND]
        for i, c in enumerate(cands):
            cand_ref[i:i + 1, :] = c
        call = cand_ref[...]
        n_gt = jnp.zeros(call.shape, jnp.float32)
        for c in cands:
            n_gt = n_gt + jnp.where(c > call, 1.0, 0.0)
        tau = jnp.min(jnp.where(n_gt <= PEER_TOPK - 1, call, -BIG_NEG), axis=0, keepdims=True)
        z = jnp.sum(jnp.where(call >= tau, jnp.exp(call - cmax), 0.0), axis=0, keepdims=True)
        count = [None] * PEER_TOPK
        for (a, b), c in zip(PEER_CAND, cands):
            hit = jnp.where(c >= tau, 1.0, 0.0)
            count[a] = hit if count[a] is None else count[a] + hit
        s1 = sc_ref[2 * hd]
        s2 = sc_ref[2 * hd + 1]
        cnt = jnp.zeros(s1.shape, jnp.float32)
        rank2 = jnp.zeros(s2.shape, jnp.float32)
        for a in range(PEER_TOPK):
            cnt = jnp.where(s1 == t1[a], count[a], cnt)
            rank2 = rank2 + jnp.where(t2[a] > s2, 1.0, 0.0)
        rk2_ref[hd] = rank2.astype(jnp.bfloat16)
        a2_ref[hd] = jnp.exp(s2 - t2[0]).astype(jnp.bfloat16)
        cnt_ref[hd] = cnt
        r_ref[hd] = jnp.exp(s1 - t1[0]) / z
        return carry

    lax.fori_loop(0, PEER_HEADS, head, 0)


def _peer_route(h, gain, wq_t, keys, tt):
    n, d = h.shape
    hp, nk, kd = keys.shape
    stat = jax.ShapeDtypeStruct((PEER_HEADS, nk, n), jnp.float32)
    stat_bf16 = jax.ShapeDtypeStruct((PEER_HEADS, nk, n), jnp.bfloat16)
    stat_spec = pl.BlockSpec((PEER_HEADS, nk, tt), lambda i: (0, 0, i))
    return pl.pallas_call(
        _peer_route_body,
        grid=(n // tt,),
        in_specs=[
            pl.BlockSpec((tt, d), lambda i: (i, 0)),
            pl.BlockSpec((1, d), lambda i: (0, 0)),
            pl.BlockSpec(wq_t.shape, lambda i: (0, 0)),
            pl.BlockSpec(keys.shape, lambda i: (0, 0, 0)),
        ],
        out_specs=[pl.BlockSpec((d, tt), lambda i: (0, i)), stat_spec, stat_spec, stat_spec, stat_spec],
        out_shape=[jax.ShapeDtypeStruct((d, n), jnp.bfloat16), stat_bf16, stat_bf16, stat, stat],
        scratch_shapes=[pltpu.VMEM((wq_t.shape[0], tt), jnp.bfloat16),
                        pltpu.VMEM((hp, nk, tt), jnp.float32), pltpu.VMEM((hp, nk, tt), jnp.float32),
                        pltpu.VMEM((PEER_TOPK, hp, 1, tt), jnp.float32),
                        pltpu.VMEM((PEER_CAND_ROWS, tt), jnp.float32)],
        compiler_params=pltpu.CompilerParams(
            dimension_semantics=("arbitrary",), vmem_limit_bytes=VMEM_LIMIT_BYTES),
        name="peer_route",
    )(h, gain.reshape(1, d), wq_t, keys)


def _peer_experts_body(xt_ref, u_ref, vt_ref, rk2_ref, a2_ref, cnt_ref, r_ref, o_ref, *, n_i1):
    @pl.when(pl.program_id(1) == 0)
    def _():
        o_ref[...] = jnp.zeros(o_ref.shape, jnp.float32)

    act = _gelu(jnp.dot(u_ref[...], xt_ref[...], preferred_element_type=jnp.float32))
    parts = []
    for i1 in range(n_i1):
        w = None
        for hd in range(PEER_HEADS):
            picked = rk2_ref[hd] < cnt_ref[hd, i1:i1 + 1, :].astype(jnp.bfloat16)
            term = jnp.where(picked, a2_ref[hd], 0.0) * r_ref[hd, i1:i1 + 1, :].astype(jnp.bfloat16)
            w = term if w is None else w + term
        parts.append(w * act[i1 * N_KEYS:(i1 + 1) * N_KEYS].astype(jnp.bfloat16))
    p = jnp.concatenate(parts, axis=0)
    o_ref[...] += jnp.dot(vt_ref[...], p, preferred_element_type=jnp.float32)


def _peer_experts(xt, u, vt, rk2, a2, cnt, r, tt, te):
    d, n = xt.shape
    e = u.shape[0]
    n_i1 = te // N_KEYS
    full = pl.BlockSpec((PEER_HEADS, N_KEYS, tt), lambda i, j: (0, 0, i))
    part = pl.BlockSpec((PEER_HEADS, n_i1, tt), lambda i, j: (0, j, i))
    return pl.pallas_call(
        functools.partial(_peer_experts_body, n_i1=n_i1),
        grid=(n // tt, e // te),
        in_specs=[
            pl.BlockSpec((d, tt), lambda i, j: (0, i)),
            pl.BlockSpec((te, d), lambda i, j: (j, 0)),
            pl.BlockSpec((d, te), lambda i, j: (0, j)),
            full, full, part, part,
        ],
        out_specs=pl.BlockSpec((d, tt), lambda i, j: (0, i)),
        out_shape=jax.ShapeDtypeStruct((d, n), jnp.float32),
        compiler_params=pltpu.CompilerParams(
            dimension_semantics=("arbitrary", "arbitrary"), vmem_limit_bytes=VMEM_LIMIT_BYTES),
        name="peer_experts",
    )(xt, u, vt, rk2, a2, cnt, r)


def _transpose_cast_body(x_ref, o_ref):
    o_ref[...] = x_ref[...].T.astype(o_ref.dtype)


def _transpose_cast(x, blk, dtype):
    r, c = x.shape
    return pl.pallas_call(
        _transpose_cast_body,
        grid=(r // blk, c // blk),
        in_specs=[pl.BlockSpec((blk, blk), lambda i, j: (i, j))],
        out_specs=pl.BlockSpec((blk, blk), lambda i, j: (j, i)),
        out_shape=jax.ShapeDtypeStruct((c, r), dtype),
        compiler_params=pltpu.CompilerParams(
            dimension_semantics=("arbitrary", "arbitrary"), vmem_limit_bytes=VMEM_LIMIT_BYTES),
        name="transpose_cast",
    )(x)


def _peer(h, gain, w_q, sub_keys, u_tab, v_tab):
    keys = sub_keys.reshape(PEER_HEADS * 2, N_KEYS, PEER_KEY_DIM // 2).astype(jnp.bfloat16)
    xt, rk2, a2, cnt, r = _peer_route(h, gain, w_q.T.astype(jnp.bfloat16), keys, 256)
    out_t = _peer_experts(xt, u_tab.astype(jnp.bfloat16), _transpose_cast(v_tab, 1024, jnp.bfloat16),
                          rk2, a2, cnt, r, 512, 1024)
    return out_t.T


def kernel(x, norm1_w, w_in, hg_lb_logits, hg_norm_w, q_norm_w, kc_norm_w, ks_norm_w, kw_norm_w,
           cmp_pos_k, cmp_pos_v, w_ck1, w_ck2, w_cv1, w_cv2, w_out, norm2_w,
           peer_w_q, peer_sub_keys, peer_u, peer_v):
    B, T, D = x.shape
    n = B * T
    layer = 0
    lower_bounds = jnp.cumsum(jax.nn.softmax(hg_lb_logits, axis=0), axis=0)
    xt = x.reshape(n, D)

    w_in_b = jnp.pad(w_in[layer].astype(jnp.bfloat16), ((0, 0), (0, IN_COLS_PADDED - IN_COLS)))
    proj = _norm_matmul(xt, norm1_w[layer], w_in_b, 1024, 1024)
    hg_out = _hgrn2(proj, lower_bounds[layer], hg_norm_w[layer], B)
    nsa_out = _nsa(proj, B, q_norm_w[layer], kc_norm_w[layer], ks_norm_w[layer], kw_norm_w[layer],
                   cmp_pos_k[layer], cmp_pos_v[layer], w_ck1[layer], w_ck2[layer], w_cv1[layer], w_cv2[layer])
    h = _out_proj(hg_out, nsa_out, w_out[layer].astype(jnp.bfloat16), xt, 512, 1024)

    y = h + _peer(h, norm2_w[layer], peer_w_q[layer], peer_sub_keys[layer], peer_u[layer], peer_v[layer])
    return y.reshape(B, T, D)
```

```python
import functools

import jax
import jax.numpy as jnp
import numpy as np
from jax import lax
from jax.experimental import pallas as pl
from jax.experimental.pallas import tpu as pltpu

D_MODEL = 2048
HG_WIDTH = 1024
HG_HEAD_DIM = 128
HG_HEADS = 8
HG_CHUNK = 64
NSA_WIDTH = 1024
NSA_HEAD_DIM = 64
NSA_Q_HEADS = 16
NSA_KV_HEADS = 4
NSA_GROUP = 4
CMP_BLOCK = 32
CMP_STRIDE = 16
CMP_HIDDEN = 256
SEL_BLOCK = 64
N_SELECT = 16
N_LOCAL = 2
WINDOW = 512
FORCE_SCORE = 1e9
NEG_INF = -1e30
MASK_OFF = 3e30
LOG2E = float(np.log2(np.e))
PEER_HEADS = 8
N_KEYS = 128
PEER_KEY_DIM = 256
PEER_TOPK = 16
NORM_EPS = 1e-6
KV_W = NSA_KV_HEADS * NSA_HEAD_DIM
N_GATES = 3 * NSA_Q_HEADS
IN_SIZES = [HG_WIDTH] * 4 + [NSA_WIDTH] + [KV_W] * 6 + [N_GATES]
IN_COLS = sum(IN_SIZES)
IN_COLS_PADDED = 7168
Q_COL0 = 4 * HG_WIDTH
KV_COL0 = Q_COL0 + NSA_WIDTH
GATE_COL0 = KV_COL0 + 6 * KV_W

VMEM_LIMIT_BYTES = 48 * 1024 * 1024
LANES = 128


def _norm_matmul_body(x_ref, g_ref, w_ref, o_ref, xn_ref):
    @pl.when(pl.program_id(1) == 0)
    def _():
        x = x_ref[...]
        r = lax.rsqrt(jnp.mean(x * x, axis=-1, keepdims=True) + NORM_EPS)
        xn_ref[...] = (x * r * g_ref[...]).astype(jnp.bfloat16)

    o_ref[...] = jnp.dot(xn_ref[...], w_ref[...], preferred_element_type=jnp.float32).astype(o_ref.dtype)


def _norm_matmul(x, gain, w, tm, tn, out_dtype=jnp.float32):
    m, k = x.shape
    n = w.shape[1]
    return pl.pallas_call(
        _norm_matmul_body,
        grid=(m // tm, n // tn),
        in_specs=[
            pl.BlockSpec((tm, k), lambda i, j: (i, 0)),
            pl.BlockSpec((1, k), lambda i, j: (0, 0)),
            pl.BlockSpec((k, tn), lambda i, j: (0, j)),
        ],
        out_specs=pl.BlockSpec((tm, tn), lambda i, j: (i, j)),
        out_shape=jax.ShapeDtypeStruct((m, n), out_dtype),
        scratch_shapes=[pltpu.VMEM((tm, k), jnp.bfloat16)],
        compiler_params=pltpu.CompilerParams(
            dimension_semantics=("arbitrary", "arbitrary"), vmem_limit_bytes=VMEM_LIMIT_BYTES),
        name="norm_matmul",
    )(x, gain.reshape(1, k), w)


def _out_proj_body(a1_ref, a2_ref, w1_ref, w2_ref, r_ref, o_ref):
    acc = jnp.dot(a1_ref[...].astype(jnp.bfloat16), w1_ref[...], preferred_element_type=jnp.float32)
    acc = acc + jnp.dot(a2_ref[...].astype(jnp.bfloat16), w2_ref[...], preferred_element_type=jnp.float32)
    o_ref[...] = r_ref[...] + acc


def _out_proj(a1, a2, w, res, tm, tn):
    m, k1 = a1.shape
    k2 = a2.shape[1]
    n = w.shape[1]
    return pl.pallas_call(
        _out_proj_body,
        grid=(m // tm, n // tn),
        in_specs=[
            pl.BlockSpec((tm, k1), lambda i, j: (i, 0)),
            pl.BlockSpec((tm, k2), lambda i, j: (i, 0)),
            pl.BlockSpec((k1, tn), lambda i, j: (0, j)),
            pl.BlockSpec((k2, tn), lambda i, j: (k1 // k2, j)),
            pl.BlockSpec((tm, tn), lambda i, j: (i, j)),
        ],
        out_specs=pl.BlockSpec((tm, tn), lambda i, j: (i, j)),
        out_shape=jax.ShapeDtypeStruct((m, n), jnp.float32),
        compiler_params=pltpu.CompilerParams(
            dimension_semantics=("arbitrary", "arbitrary"), vmem_limit_bytes=VMEM_LIMIT_BYTES),
        name="out_proj",
    )(a1, a2, w, w, res)


def _hgrn2_body(q_ref, f_ref, v_ref, g_ref, lb_ref, nw_ref, o_ref, qd_ref, kd_ref, ku_ref, vb_ref, dec_ref, sp_ref):
    t, dk = q_ref.shape
    C = HG_CHUNK
    nc = t // C
    lb = lb_ref[...]
    f = lb + (1.0 - lb) * jax.nn.sigmoid(f_ref[...])
    kf = 1.0 - f
    b = jnp.log(f)
    row = lax.broadcasted_iota(jnp.int32, (t, dk), 0) % C
    shift = 1
    while shift < C:
        b = b + jnp.where(row >= shift, pltpu.roll(b, shift, 0), 0.0)
        shift *= 2
    b3 = b.reshape(nc, C, dk)
    b_end = b3[:, C - 1:C, :]
    qd_ref[...] = (q_ref[...] * jnp.exp(b)).astype(jnp.bfloat16)
    kd_ref[...] = (kf * jnp.exp(-b)).astype(jnp.bfloat16)
    ku_ref[...] = (kf.reshape(nc, C, dk) * jnp.exp(b_end - b3)).reshape(t, dk).astype(jnp.bfloat16)
    vb_ref[...] = v_ref[...].astype(jnp.bfloat16)
    dec_ref[...] = jnp.exp(b_end)

    st = jnp.zeros((dk, dk), jnp.float32)
    for n in range(nc):
        rows = slice(n * C, (n + 1) * C)
        sp_ref[n] = st.astype(jnp.bfloat16)
        upd_t = lax.dot_general(vb_ref[rows, :], ku_ref[rows, :], (((0,), (0,)), ((), ())),
                                preferred_element_type=jnp.float32)
        st = dec_ref[n] * st + upd_t

    causal = lax.broadcasted_iota(jnp.int32, (C, C), 0) >= lax.broadcasted_iota(jnp.int32, (C, C), 1)
    for n in range(nc):
        rows = slice(n * C, (n + 1) * C)
        qd = qd_ref[rows, :]
        attn = lax.dot_general(qd, kd_ref[rows, :], (((1,), (1,)), ((), ())), preferred_element_type=jnp.float32)
        attn = jnp.where(causal, attn, 0.0).astype(jnp.bfloat16)
        o = jnp.dot(attn, vb_ref[rows, :], preferred_element_type=jnp.float32)
        o = o + lax.dot_general(qd, sp_ref[n], (((1,), (1,)), ((), ())), preferred_element_type=jnp.float32)
        o = o * lax.rsqrt(jnp.mean(o * o, axis=-1, keepdims=True) + NORM_EPS) * nw_ref[...]
        o_ref[rows, :] = o * jax.nn.silu(g_ref[rows, :])


def _hgrn2(proj, lb, norm_w, batch):
    n = proj.shape[0]
    t = n // batch
    dk, H = HG_HEAD_DIM, HG_HEADS
    nc = t // HG_CHUNK

    def part(p):
        return pl.BlockSpec((t, dk), lambda b, h: (b, p * H + h))

    slab = pltpu.VMEM((t, dk), jnp.bfloat16)
    return pl.pallas_call(
        _hgrn2_body,
        grid=(batch, H),
        in_specs=[part(0), part(1), part(2), part(3),
                  pl.BlockSpec((1, dk), lambda b, h: (0, h)),
                  pl.BlockSpec((1, dk), lambda b, h: (0, 0))],
        out_specs=pl.BlockSpec((t, dk), lambda b, h: (b, h)),
        out_shape=jax.ShapeDtypeStruct((n, H * dk), jnp.float32),
        scratch_shapes=[slab, slab, slab, slab, pltpu.VMEM((nc, 1, dk), jnp.float32),
                        pltpu.VMEM((nc, dk, dk), jnp.bfloat16)],
        compiler_params=pltpu.CompilerParams(
            dimension_semantics=("arbitrary", "arbitrary"), vmem_limit_bytes=VMEM_LIMIT_BYTES),
        name="hgrn2",
    )(proj, proj, proj, proj, lb.reshape(1, H * dk), norm_w.reshape(1, dk))


GELU_C0 = float(np.sqrt(2.0 / np.pi))
GELU_C1 = GELU_C0 * 0.044715


def _gelu(x):
    half = 0.5 * x
    return half + half * jnp.tanh(x * (GELU_C0 + GELU_C1 * (x * x)))


def _head_norm(x, w):
    return x * lax.rsqrt(jnp.mean(x * x, axis=-1, keepdims=True) + NORM_EPS) * w


def _value_slab(v):
    lane = lax.broadcasted_iota(jnp.int32, v.shape, 1)
    return jnp.concatenate([v, jnp.where(lane == 0, 1.0, 0.0)], axis=-1).astype(jnp.bfloat16)


def _kv_prep_body(c_ref, s_ref, w_ref, ksw_ref, kww_ref, kn_ref, vv_ref, cf_ref):
    dh = NSA_HEAD_DIM
    for h in range(NSA_KV_HEADS):
        k_cols = slice(h * dh, (h + 1) * dh)
        v_cols = slice(KV_W + h * dh, KV_W + (h + 1) * dh)
        cf_ref[0, h] = c_ref[:, k_cols]
        cf_ref[1, h] = c_ref[:, v_cols]
        kn_ref[0, h] = _head_norm(s_ref[:, k_cols], ksw_ref[...]).astype(jnp.bfloat16)
        vv_ref[0, h] = _value_slab(s_ref[:, v_cols])
        kn_ref[1, h] = _head_norm(w_ref[:, k_cols], kww_ref[...]).astype(jnp.bfloat16)
        vv_ref[1, h] = _value_slab(w_ref[:, v_cols])


def _kv_prep(proj, ks_w, kw_w, tm):
    n = proj.shape[0]
    dh = NSA_HEAD_DIM
    pair = 2 * KV_W
    col0 = KV_COL0 // pair
    head_block = pl.BlockSpec((2, NSA_KV_HEADS, tm, dh), lambda i: (0, 0, i, 0))
    w_spec = pl.BlockSpec((1, dh), lambda i: (0, 0))
    return pl.pallas_call(
        _kv_prep_body,
        grid=(n // tm,),
        in_specs=[pl.BlockSpec((tm, pair), lambda i: (i, col0)),
                  pl.BlockSpec((tm, pair), lambda i: (i, col0 + 1)),
                  pl.BlockSpec((tm, pair), lambda i: (i, col0 + 2)),
                  w_spec, w_spec],
        out_specs=[head_block, pl.BlockSpec((2, NSA_KV_HEADS, tm, 2 * dh), lambda i: (0, 0, i, 0)), head_block],
        out_shape=[jax.ShapeDtypeStruct((2, NSA_KV_HEADS, n, dh), jnp.bfloat16),
                   jax.ShapeDtypeStruct((2, NSA_KV_HEADS, n, 2 * dh), jnp.bfloat16),
                   jax.ShapeDtypeStruct((2, NSA_KV_HEADS, n, dh), jnp.float32)],
        compiler_params=pltpu.CompilerParams(
            dimension_semantics=("arbitrary",), vmem_limit_bytes=VMEM_LIMIT_BYTES),
        name="kv_prep",
    )(proj, proj, proj, ks_w.reshape(1, dh), kw_w.reshape(1, dh))


def _cmp_kv_body(r_ref, pos_ref, w1_ref, w2_ref, nw_ref, o_ref):
    half = CMP_STRIDE * NSA_HEAD_DIM
    n_strips = r_ref.shape[2]
    for kind in range(2):
        strips = r_ref[kind, 0]
        top = (strips + pos_ref[kind, 0:1, :]).astype(jnp.bfloat16)
        bot = (strips + pos_ref[kind, 1:2, :]).astype(jnp.bfloat16)
        a = jnp.dot(top, w1_ref[kind, :half, :], preferred_element_type=jnp.float32)
        b = jnp.dot(bot, w1_ref[kind, half:, :], preferred_element_type=jnp.float32)
        hid = _gelu(a + pltpu.roll(b, n_strips - 1, 0))
        out = jnp.dot(hid.astype(jnp.bfloat16), w2_ref[kind], preferred_element_type=jnp.float32)
        if kind == 0:
            out = _head_norm(out, nw_ref[...])
        o_ref[kind, 0, 0] = out.astype(jnp.bfloat16)


def _cmp_kv(cf, pos, w1, w2, kc_w, batch):
    _, hkv, n, dh = cf.shape
    t = n // batch
    n_strips = t // CMP_STRIDE
    strips = cf.reshape(2, hkv, n // CMP_STRIDE, CMP_STRIDE * dh)
    return pl.pallas_call(
        _cmp_kv_body,
        grid=(batch, hkv),
        in_specs=[pl.BlockSpec((2, 1, n_strips, CMP_STRIDE * dh), lambda b, g: (0, g, b, 0)),
                  pl.BlockSpec(pos.shape, lambda b, g: (0, 0, 0)),
                  pl.BlockSpec(w1.shape, lambda b, g: (0, 0, 0)),
                  pl.BlockSpec(w2.shape, lambda b, g: (0, 0, 0)),
                  pl.BlockSpec((1, dh), lambda b, g: (0, 0))],
        out_specs=pl.BlockSpec((2, 1, 1, n_strips, dh), lambda b, g: (0, g, b, 0, 0)),
        out_shape=jax.ShapeDtypeStruct((2, hkv, batch, n_strips, dh), jnp.bfloat16),
        compiler_params=pltpu.CompilerParams(
            dimension_semantics=("arbitrary", "arbitrary"), vmem_limit_bytes=VMEM_LIMIT_BYTES),
        name="cmp_kv",
    )(strips, pos, w1, w2, kc_w.reshape(1, dh))


def _flash_step(qs_ref, k, v, slope_ref, m_ref, acc_ref, base, mask_fn, tq):
    tk = k.shape[0]
    g = NSA_GROUP
    dist = base + lax.broadcasted_iota(jnp.int32, (tq, tk), 0) - lax.broadcasted_iota(jnp.int32, (tq, tk), 1)
    off = jnp.where(mask_fn(dist), 0.0, MASK_OFF)[None]
    s = lax.dot_general(qs_ref[...], k, (((1,), (1,)), ((), ())),
                        preferred_element_type=jnp.float32).reshape(g, tq, tk)
    s = (s - off) - slope_ref[0] * dist.astype(jnp.float32)[None]
    m_old = m_ref[...].reshape(g, tq, 1)
    m_new = jnp.maximum(m_old, jnp.max(s, axis=-1, keepdims=True))
    p = jnp.exp2(s - m_new)
    alpha = jnp.exp2(m_old - m_new)
    pv = jnp.dot(p.reshape(g * tq, tk).astype(jnp.bfloat16), v, preferred_element_type=jnp.float32)
    acc_ref[...] = alpha.reshape(g * tq, 1) * acc_ref[...] + pv
    m_ref[...] = m_new.reshape(g * tq, 1)


def _nsa_attn_body(q_ref, gate_ref, kc_ref, vc_ref, ks_ref, vs_ref, kw_ref, vw_ref, qw_ref, slope_ref, o_ref,
                   qs_ref, sel_ref, oc_ref, ms_ref, as_ref, mw_ref, aw_ref, *, tq, tk, n_sel):
    i = pl.program_id(2)
    j = pl.program_id(3)
    G, dh = NSA_GROUP, NSA_HEAD_DIM
    last_j = (i * tq + tq - 1) // tk

    @pl.when(j == 0)
    def _():
        x = q_ref[...]
        for r in range(G):
            qn = _head_norm(x[:, r * dh:(r + 1) * dh], qw_ref[...]) * (dh ** -0.5 * LOG2E)
            qs_ref[r * tq:(r + 1) * tq, :] = qn.astype(jnp.bfloat16)
        for ref in (ms_ref, mw_ref):
            ref[...] = jnp.full(ref.shape, NEG_INF, jnp.float32)
        for ref in (as_ref, aw_ref):
            ref[...] = jnp.zeros(ref.shape, jnp.float32)

        n_cmp_pad = kc_ref.shape[-2]
        t_pos = i * tq + lax.broadcasted_iota(jnp.int32, (tq, n_cmp_pad), 0)
        blk = lax.broadcasted_iota(jnp.int32, (tq, n_cmp_pad), 1)
        dist = t_pos - (blk * CMP_STRIDE + CMP_BLOCK - 1)
        valid = (dist >= 0) & (blk < n_cmp_pad - 1)
        distf = dist.astype(jnp.float32)
        cb = lax.broadcasted_iota(jnp.int32, (n_sel, n_cmp_pad), 1) * CMP_STRIDE
        sb = lax.broadcasted_iota(jnp.int32, (n_sel, n_cmp_pad), 0) * SEL_BLOCK
        overlap_t = ((cb < sb + SEL_BLOCK) & (cb + CMP_BLOCK > sb)).astype(jnp.bfloat16)
        imp = None
        for r in range(G):
            rows = slice(r * tq, (r + 1) * tq)
            s = lax.dot_general(qs_ref[rows, :], kc_ref[0, 0, 0], (((1,), (1,)), ((), ())),
                                preferred_element_type=jnp.float32)
            s = jnp.where(valid, s - slope_ref[0, r] * distf, NEG_INF)
            p = jnp.where(valid, jnp.exp2(s - jnp.max(s, axis=-1, keepdims=True)), 0.0)
            denom = jnp.maximum(jnp.sum(p, axis=-1, keepdims=True), 1e-30)
            p = (p / denom).astype(jnp.bfloat16)
            oc_ref[rows, :] = jnp.dot(p, vc_ref[0, 0, 0], preferred_element_type=jnp.float32)
            part = lax.dot_general(overlap_t, p, (((1,), (1,)), ((), ())), preferred_element_type=jnp.float32)
            imp = part if imp is None else imp + part

        cur = (i * tq + lax.broadcasted_iota(jnp.int32, (n_sel, tq), 1)) // SEL_BLOCK
        jb = lax.broadcasted_iota(jnp.int32, (n_sel, tq), 0)
        forced = (jb == 0) | ((jb <= cur) & (jb > cur - N_LOCAL))
        imp = jnp.where(forced, FORCE_SCORE, jnp.where(jb > cur, -FORCE_SCORE, imp))
        rank = jnp.zeros((n_sel, tq), jnp.float32)
        for c in range(n_sel):
            row = imp[c:c + 1, :]
            ahead = (row > imp) | ((row == imp) & (jb > c))
            rank = rank + jnp.where(ahead, 1.0, 0.0)
        sel_t = jnp.where(rank < min(N_SELECT, n_sel), 1.0, 0.0)
        sel_t = jnp.concatenate([sel_t, jnp.zeros((sel_ref.shape[1] - n_sel, tq), jnp.float32)], axis=0)
        sel_ref[...] = sel_t.T.astype(jnp.bfloat16)

    @pl.when(j <= last_j)
    def _():
        n_pad = sel_ref.shape[1]
        blk = lax.broadcasted_iota(jnp.int32, (n_pad, tk), 0)
        col_blk = j * (tk // SEL_BLOCK) + lax.broadcasted_iota(jnp.int32, (n_pad, tk), 1) // SEL_BLOCK
        expand = (blk == col_blk).astype(jnp.bfloat16)

        def mask(dist):
            picked = jnp.dot(sel_ref[...], expand, preferred_element_type=jnp.float32)
            return (dist >= 0) & (picked > 0.5)

        _flash_step(qs_ref, ks_ref[0, 0], vs_ref[0, 0], slope_ref, ms_ref, as_ref,
                    i * tq - j * tk, mask, tq)

    @pl.when((j <= last_j) & (j * tk + tk - 1 >= i * tq - (WINDOW - 1)))
    def _():
        _flash_step(qs_ref, kw_ref[0, 0], vw_ref[0, 0], slope_ref, mw_ref, aw_ref,
                    i * tq - j * tk, lambda dist: (dist >= 0) & (dist < WINDOW), tq)

    @pl.when(j == pl.num_programs(3) - 1)
    def _():
        gates = jax.nn.sigmoid(gate_ref[0])
        o_sel = as_ref[:, :dh] / as_ref[:, dh:dh + 1]
        o_win = aw_ref[:, :dh] / aw_ref[:, dh:dh + 1]
        o_cmp = oc_ref[...]
        outs = []
        for r in range(G):
            rows = slice(r * tq, (r + 1) * tq)
            outs.append(gates[:, 3 * r:3 * r + 1] * o_cmp[rows] + gates[:, 3 * r + 1:3 * r + 2] * o_sel[rows]
                        + gates[:, 3 * r + 2:3 * r + 3] * o_win[rows])
        o_ref[...] = jnp.concatenate(outs, axis=-1)


def _nsa_attn(proj, gates, cmp_kv, kn, vv, q_w, slopes, batch, tq=512, tk=512):
    n = proj.shape[0]
    t = n // batch
    G, dh, hkv = NSA_GROUP, NSA_HEAD_DIM, NSA_KV_HEADS
    n_cmp_pad = cmp_kv.shape[-2]
    n_sel = t // SEL_BLOCK
    q_blk0 = Q_COL0 // (G * dh)

    def kv_map(kind):
        def index(b, g, i, j):
            return (kind, g, b * (t // tk) + jnp.minimum(j, (i * tq + tq - 1) // tk), 0)
        return index

    def cmp_map(kind):
        return lambda b, g, i, j: (kind, g, b, 0, 0)

    kv_block = (1, 1, tk, dh)
    cmp_block = (1, 1, 1, n_cmp_pad, dh)
    v_block = (1, 1, tk, 2 * dh)
    slab = pltpu.VMEM((G * tq, dh), jnp.float32)
    wide = pltpu.VMEM((G * tq, 2 * dh), jnp.float32)
    stat = pltpu.VMEM((G * tq, 1), jnp.float32)
    return pl.pallas_call(
        functools.partial(_nsa_attn_body, tq=tq, tk=tk, n_sel=n_sel),
        grid=(batch, hkv, t // tq, t // tk),
        in_specs=[
            pl.BlockSpec((tq, G * dh), lambda b, g, i, j: (b * (t // tq) + i, q_blk0 + g)),
            pl.BlockSpec((1, tq, 3 * G), lambda b, g, i, j: (g, b * (t // tq) + i, 0)),
            pl.BlockSpec(cmp_block, cmp_map(0)), pl.BlockSpec(cmp_block, cmp_map(1)),
            pl.BlockSpec(kv_block, kv_map(0)), pl.BlockSpec(v_block, kv_map(0)),
            pl.BlockSpec(kv_block, kv_map(1)), pl.BlockSpec(v_block, kv_map(1)),
            pl.BlockSpec((1, dh), lambda b, g, i, j: (0, 0)),
            pl.BlockSpec((1, G, 1, 1), lambda b, g, i, j: (g, 0, 0, 0)),
        ],
        out_specs=pl.BlockSpec((tq, G * dh), lambda b, g, i, j: (b * (t // tq) + i, g)),
        out_shape=jax.ShapeDtypeStruct((n, hkv * G * dh), jnp.float32),
        scratch_shapes=[pltpu.VMEM((G * tq, dh), jnp.bfloat16), pltpu.VMEM((tq, LANES), jnp.bfloat16),
                        slab, stat, wide, stat, wide],
        compiler_params=pltpu.CompilerParams(
            dimension_semantics=("arbitrary",) * 4, vmem_limit_bytes=VMEM_LIMIT_BYTES),
        name="nsa_attn",
    )(proj, gates, cmp_kv, cmp_kv, kn, vv, kn, vv, q_w.reshape(1, dh), slopes)


def _nsa(proj, batch, q_norm_w, kc_norm_w, ks_norm_w, kw_norm_w, pos_k, pos_v, w_ck1, w_ck2, w_cv1, w_cv2):
    n = proj.shape[0]
    G, hkv = NSA_GROUP, NSA_KV_HEADS
    kn, vv, cf = _kv_prep(proj, ks_norm_w, kw_norm_w, 512)
    half = CMP_STRIDE * NSA_HEAD_DIM
    pos = jnp.stack([pos_k.reshape(2, half), pos_v.reshape(2, half)])
    w1 = jnp.stack([w_ck1, w_cv1]).astype(jnp.bfloat16)
    w2 = jnp.stack([w_ck2, w_cv2]).astype(jnp.bfloat16)
    cmp_kv = _cmp_kv(cf, pos, w1, w2, kc_norm_w, batch)
    gates = proj[:, GATE_COL0:GATE_COL0 + N_GATES].reshape(n, hkv, 3 * G).transpose(1, 0, 2)
    slopes = jnp.asarray(2.0 ** (-8.0 * np.arange(1, NSA_Q_HEADS + 1) / NSA_Q_HEADS), jnp.float32) * LOG2E
    return _nsa_attn(proj, gates, cmp_kv, kn, vv, q_norm_w, slopes.reshape(hkv, G, 1, 1), batch)


BIG_NEG = -3.0e38
PEER_CAND = [(a, b) for a in range(PEER_TOPK) for b in range(PEER_TOPK) if (a + 1) * (b + 1) <= PEER_TOPK]
PEER_CAND_ROWS = -(-len(PEER_CAND) // 8) * 8


def _peer_route_body(h_ref, g_ref, wq_ref, keys_ref, xt_ref, rk2_ref, a2_ref, cnt_ref, r_ref,
                     qt_ref, sc_ref, cur_ref, top_ref, cand_ref):
    x = h_ref[...]
    hn = x * lax.rsqrt(jnp.mean(x * x, axis=-1, keepdims=True) + NORM_EPS) * g_ref[...]
    hnt = hn.T.astype(jnp.bfloat16)
    xt_ref[...] = hnt
    qt_ref[...] = jnp.dot(wq_ref[...], hnt, preferred_element_type=jnp.float32).astype(jnp.bfloat16)
    cand_ref[...] = jnp.full(cand_ref.shape, BIG_NEG, jnp.float32)
    kd = PEER_KEY_DIM // 2
    n_parts = 2 * PEER_HEADS
    for hp in range(n_parts):
        s = jnp.dot(keys_ref[hp], qt_ref[hp * kd:(hp + 1) * kd, :], preferred_element_type=jnp.float32)
        sc_ref[hp] = s
        cur_ref[hp] = s

    def extract(k, carry):
        cur = cur_ref[...]
        mk = jnp.max(cur, axis=1, keepdims=True)
        top_ref[k] = mk
        cur_ref[...] = jnp.where(cur == mk, BIG_NEG, cur)
        return carry

    lax.fori_loop(0, PEER_TOPK, extract, 0)

    def head(hd, carry):
        t1 = [top_ref[a, 2 * hd] for a in range(PEER_TOPK)]
        t2 = [top_ref[b, 2 * hd + 1] for b in range(PEER_TOPK)]
        cmax = t1[0] + t2[0]
        cands = [t1[a] + t2[b] for a, b in PEER_CAND]
        for i, c in enumerate(cands):
            cand_ref[i:i + 1, :] = c
        call = cand_ref[...]
        n_gt = jnp.zeros(call.shape, jnp.float32)
        for c in cands:
            n_gt = n_gt + jnp.where(c > call, 1.0, 0.0)
        tau = jnp.min(jnp.where(n_gt <= PEER_TOPK - 1, call, -BIG_NEG), axis=0, keepdims=True)
        z = jnp.sum(jnp.where(call >= tau, jnp.exp(call - cmax), 0.0), axis=0, keepdims=True)
        count = [None] * PEER_TOPK
        for (a, b), c in zip(PEER_CAND, cands):
            hit = jnp.where(c >= tau, 1.0, 0.0)
            count[a] = hit if count[a] is None else count[a] + hit
        s1 = sc_ref[2 * hd]
        s2 = sc_ref[2 * hd + 1]
        cnt = jnp.zeros(s1.shape, jnp.float32)
        rank2 = jnp.zeros(s2.shape, jnp.float32)
        for a in range(PEER_TOPK):
            cnt = jnp.where(s1 == t1[a], count[a], cnt)
            rank2 = rank2 + jnp.where(t2[a] > s2, 1.0, 0.0)
        rk2_ref[hd] = rank2.astype(jnp.bfloat16)
        a2_ref[hd] = jnp.exp(s2 - t2[0]).astype(jnp.bfloat16)
        cnt_ref[hd] = cnt
        r_ref[hd] = jnp.exp(s1 - t1[0]) / z
        return carry

    lax.fori_loop(0, PEER_HEADS, head, 0)


def _peer_route(h, gain, wq_t, keys, tt):
    n, d = h.shape
    hp, nk, kd = keys.shape
    stat = jax.ShapeDtypeStruct((PEER_HEADS, nk, n), jnp.float32)
    stat_bf16 = jax.ShapeDtypeStruct((PEER_HEADS, nk, n), jnp.bfloat16)
    stat_spec = pl.BlockSpec((PEER_HEADS, nk, tt), lambda i: (0, 0, i))
    return pl.pallas_call(
        _peer_route_body,
        grid=(n // tt,),
        in_specs=[
            pl.BlockSpec((tt, d), lambda i: (i, 0)),
            pl.BlockSpec((1, d), lambda i: (0, 0)),
            pl.BlockSpec(wq_t.shape, lambda i: (0, 0)),
            pl.BlockSpec(keys.shape, lambda i: (0, 0, 0)),
        ],
        out_specs=[pl.BlockSpec((d, tt), lambda i: (0, i)), stat_spec, stat_spec, stat_spec, stat_spec],
        out_shape=[jax.ShapeDtypeStruct((d, n), jnp.bfloat16), stat_bf16, stat_bf16, stat, stat],
        scratch_shapes=[pltpu.VMEM((wq_t.shape[0], tt), jnp.bfloat16),
                        pltpu.VMEM((hp, nk, tt), jnp.float32), pltpu.VMEM((hp, nk, tt), jnp.float32),
                        pltpu.VMEM((PEER_TOPK, hp, 1, tt), jnp.float32),
                        pltpu.VMEM((PEER_CAND_ROWS, tt), jnp.float32)],
        compiler_params=pltpu.CompilerParams(
            dimension_semantics=("arbitrary",), vmem_limit_bytes=VMEM_LIMIT_BYTES),
        name="peer_route",
    )(h, gain.reshape(1, d), wq_t, keys)


def _peer_experts_body(xt_ref, u_ref, vt_ref, rk2_ref, a2_ref, cnt_ref, r_ref, o_ref, *, n_i1):
    @pl.when(pl.program_id(1) == 0)
    def _():
        o_ref[...] = jnp.zeros(o_ref.shape, jnp.float32)

    act = _gelu(jnp.dot(u_ref[...], xt_ref[...], preferred_element_type=jnp.float32))
    parts = []
    for i1 in range(n_i1):
        w = None
        for hd in range(PEER_HEADS):
            picked = rk2_ref[hd] < cnt_ref[hd, i1:i1 + 1, :].astype(jnp.bfloat16)
            term = jnp.where(picked, a2_ref[hd], 0.0) * r_ref[hd, i1:i1 + 1, :].astype(jnp.bfloat16)
            w = term if w is None else w + term
        parts.append(w * act[i1 * N_KEYS:(i1 + 1) * N_KEYS].astype(jnp.bfloat16))
    p = jnp.concatenate(parts, axis=0)
    o_ref[...] += jnp.dot(vt_ref[...], p, preferred_element_type=jnp.float32)


def _peer_experts(xt, u, vt, rk2, a2, cnt, r, tt, te):
    d, n = xt.shape
    e = u.shape[0]
    n_i1 = te // N_KEYS
    full = pl.BlockSpec((PEER_HEADS, N_KEYS, tt), lambda i, j: (0, 0, i))
    part = pl.BlockSpec((PEER_HEADS, n_i1, tt), lambda i, j: (0, j, i))
    return pl.pallas_call(
        functools.partial(_peer_experts_body, n_i1=n_i1),
        grid=(n // tt, e // te),
        in_specs=[
            pl.BlockSpec((d, tt), lambda i, j: (0, i)),
            pl.BlockSpec((te, d), lambda i, j: (j, 0)),
            pl.BlockSpec((d, te), lambda i, j: (0, j)),
            full, full, part, part,
        ],
        out_specs=pl.BlockSpec((d, tt), lambda i, j: (0, i)),
        out_shape=jax.ShapeDtypeStruct((d, n), jnp.float32),
        compiler_params=pltpu.CompilerParams(
            dimension_semantics=("arbitrary", "arbitrary"), vmem_limit_bytes=VMEM_LIMIT_BYTES),
        name="peer_experts",
    )(xt, u, vt, rk2, a2, cnt, r)


def _transpose_cast_body(x_ref, o_ref):
    o_ref[...] = x_ref[...].T.astype(o_ref.dtype)


def _transpose_cast(x, blk, dtype):
    r, c = x.shape
    return pl.pallas_call(
        _transpose_cast_body,
        grid=(r // blk, c // blk),
        in_specs=[pl.BlockSpec((blk, blk), lambda i, j: (i, j))],
        out_specs=pl.BlockSpec((blk, blk), lambda i, j: (j, i)),
        out_shape=jax.ShapeDtypeStruct((c, r), dtype),
        compiler_params=pltpu.CompilerParams(
            dimension_semantics=("arbitrary", "arbitrary"), vmem_limit_bytes=VMEM_LIMIT_BYTES),
        name="transpose_cast",
    )(x)


def _transpose_add_body(x_ref, r_ref, o_ref):
    o_ref[...] = r_ref[...] + x_ref[...].T


def _transpose_add(x_t, res, blk):
    d, n = x_t.shape
    return pl.pallas_call(
        _transpose_add_body,
        grid=(d // blk, n // blk),
        in_specs=[pl.BlockSpec((blk, blk), lambda i, j: (i, j)),
                  pl.BlockSpec((blk, blk), lambda i, j: (j, i))],
        out_specs=pl.BlockSpec((blk, blk), lambda i, j: (j, i)),
        out_shape=jax.ShapeDtypeStruct((n, d), jnp.float32),
        compiler_params=pltpu.CompilerParams(
            dimension_semantics=("arbitrary", "arbitrary"), vmem_limit_bytes=VMEM_LIMIT_BYTES),
        name="transpose_add",
    )(x_t, res)


def _peer(h, gain, w_q, sub_keys, u_tab, v_tab):
    keys = sub_keys.reshape(PEER_HEADS * 2, N_KEYS, PEER_KEY_DIM // 2).astype(jnp.bfloat16)
    xt, rk2, a2, cnt, r = _peer_route(h, gain, _transpose_cast(w_q, 1024, jnp.bfloat16), keys, 256)
    out_t = _peer_experts(xt, u_tab.astype(jnp.bfloat16), _transpose_cast(v_tab, 1024, jnp.bfloat16),
                          rk2, a2, cnt, r, 512, 1024)
    return _transpose_add(out_t, h, min(1024, h.shape[0]))


def kernel(x, norm1_w, w_in, hg_lb_logits, hg_norm_w, q_norm_w, kc_norm_w, ks_norm_w, kw_norm_w,
           cmp_pos_k, cmp_pos_v, w_ck1, w_ck2, w_cv1, w_cv2, w_out, norm2_w,
           peer_w_q, peer_sub_keys, peer_u, peer_v):
    B, T, D = x.shape
    n = B * T
    layer = 0
    lower_bounds = jnp.cumsum(jax.nn.softmax(hg_lb_logits, axis=0), axis=0)
    xt = x.reshape(n, D)

    w_in_b = jnp.pad(w_in[layer].astype(jnp.bfloat16), ((0, 0), (0, IN_COLS_PADDED - IN_COLS)))
    proj = _norm_matmul(xt, norm1_w[layer], w_in_b, 1024, 1024)
    hg_out = _hgrn2(proj, lower_bounds[layer], hg_norm_w[layer], B)
    nsa_out = _nsa(proj, B, q_norm_w[layer], kc_norm_w[layer], ks_norm_w[layer], kw_norm_w[layer],
                   cmp_pos_k[layer], cmp_pos_v[layer], w_ck1[layer], w_ck2[layer], w_cv1[layer], w_cv2[layer])
    h = _out_proj(hg_out, nsa_out, w_out[layer].astype(jnp.bfloat16), xt, 512, 1024)

    y = _peer(h, norm2_w[layer], peer_w_q[layer], peer_sub_keys[layer], peer_u[layer], peer_v[layer])
    return y.reshape(B, T, D)
```

```python
import functools

import jax
import jax.numpy as jnp
import numpy as np
from jax import lax
from jax.experimental import pallas as pl
from jax.experimental.pallas import tpu as pltpu

D_MODEL = 2048
HG_WIDTH = 1024
HG_HEAD_DIM = 128
HG_HEADS = 8
HG_CHUNK = 64
NSA_WIDTH = 1024
NSA_HEAD_DIM = 64
NSA_Q_HEADS = 16
NSA_KV_HEADS = 4
NSA_GROUP = 4
CMP_BLOCK = 32
CMP_STRIDE = 16
CMP_HIDDEN = 256
SEL_BLOCK = 64
N_SELECT = 16
N_LOCAL = 2
WINDOW = 512
FORCE_SCORE = 1e9
NEG_INF = -1e30
MASK_OFF = 3e30
LOG2E = float(np.log2(np.e))
PEER_HEADS = 8
N_KEYS = 128
PEER_KEY_DIM = 256
PEER_TOPK = 16
NORM_EPS = 1e-6
KV_W = NSA_KV_HEADS * NSA_HEAD_DIM
N_GATES = 3 * NSA_Q_HEADS
IN_SIZES = [HG_WIDTH] * 4 + [NSA_WIDTH] + [KV_W] * 6 + [N_GATES]
IN_COLS = sum(IN_SIZES)
IN_COLS_PADDED = 7168
Q_COL0 = 4 * HG_WIDTH
KV_COL0 = Q_COL0 + NSA_WIDTH
GATE_COL0 = KV_COL0 + 6 * KV_W

VMEM_LIMIT_BYTES = 48 * 1024 * 1024
LANES = 128
MXU_DEPTH = 256


def _norm_matmul_body(x_ref, g_ref, w_ref, o_ref, xn_ref):
    @pl.when(pl.program_id(1) == 0)
    def _():
        x = x_ref[...]
        r = lax.rsqrt(jnp.mean(x * x, axis=-1, keepdims=True) + NORM_EPS)
        xn_ref[...] = (x * r * g_ref[...]).astype(jnp.bfloat16)

    o_ref[...] = jnp.dot(xn_ref[...], w_ref[...], preferred_element_type=jnp.float32).astype(o_ref.dtype)


def _norm_matmul(x, gain, w, tm, tn, out_dtype=jnp.float32):
    m, k = x.shape
    n = w.shape[1]
    return pl.pallas_call(
        _norm_matmul_body,
        grid=(m // tm, n // tn),
        in_specs=[
            pl.BlockSpec((tm, k), lambda i, j: (i, 0)),
            pl.BlockSpec((1, k), lambda i, j: (0, 0)),
            pl.BlockSpec((k, tn), lambda i, j: (0, j)),
        ],
        out_specs=pl.BlockSpec((tm, tn), lambda i, j: (i, j)),
        out_shape=jax.ShapeDtypeStruct((m, n), out_dtype),
        scratch_shapes=[pltpu.VMEM((tm, k), jnp.bfloat16)],
        compiler_params=pltpu.CompilerParams(
            dimension_semantics=("arbitrary", "arbitrary"), vmem_limit_bytes=VMEM_LIMIT_BYTES),
        name="norm_matmul",
    )(x, gain.reshape(1, k), w)


def _out_proj_body(a1_ref, a2_ref, w1_ref, w2_ref, r_ref, o_ref):
    acc = jnp.dot(a1_ref[...].astype(jnp.bfloat16), w1_ref[...], preferred_element_type=jnp.float32)
    acc = acc + jnp.dot(a2_ref[...].astype(jnp.bfloat16), w2_ref[...], preferred_element_type=jnp.float32)
    o_ref[...] = r_ref[...] + acc


def _out_proj(a1, a2, w, res, tm, tn):
    m, k1 = a1.shape
    k2 = a2.shape[1]
    n = w.shape[1]
    return pl.pallas_call(
        _out_proj_body,
        grid=(m // tm, n // tn),
        in_specs=[
            pl.BlockSpec((tm, k1), lambda i, j: (i, 0)),
            pl.BlockSpec((tm, k2), lambda i, j: (i, 0)),
            pl.BlockSpec((k1, tn), lambda i, j: (0, j)),
            pl.BlockSpec((k2, tn), lambda i, j: (k1 // k2, j)),
            pl.BlockSpec((tm, tn), lambda i, j: (i, j)),
        ],
        out_specs=pl.BlockSpec((tm, tn), lambda i, j: (i, j)),
        out_shape=jax.ShapeDtypeStruct((m, n), jnp.float32),
        compiler_params=pltpu.CompilerParams(
            dimension_semantics=("arbitrary", "arbitrary"), vmem_limit_bytes=VMEM_LIMIT_BYTES),
        name="out_proj",
    )(a1, a2, w, w, res)


def _hgrn2_body(q_ref, f_ref, v_ref, g_ref, lb_ref, nw_ref, o_ref, qd_ref, kd_ref, ku_ref, vb_ref, dec_ref, sp_ref):
    t, dk = q_ref.shape
    C = HG_CHUNK
    nc = t // C
    lb = lb_ref[...]
    f = lb + (1.0 - lb) * jax.nn.sigmoid(f_ref[...])
    kf = 1.0 - f
    b = jnp.log(f)
    row = lax.broadcasted_iota(jnp.int32, (t, dk), 0) % C
    shift = 1
    while shift < C:
        b = b + jnp.where(row >= shift, pltpu.roll(b, shift, 0), 0.0)
        shift *= 2
    b3 = b.reshape(nc, C, dk)
    b_end = b3[:, C - 1:C, :]
    qd_ref[...] = (q_ref[...] * jnp.exp(b)).astype(jnp.bfloat16)
    kd_ref[...] = (kf * jnp.exp(-b)).astype(jnp.bfloat16)
    ku_ref[...] = (kf.reshape(nc, C, dk) * jnp.exp(b_end - b3)).reshape(t, dk).astype(jnp.bfloat16)
    vb_ref[...] = v_ref[...].astype(jnp.bfloat16)
    dec_ref[...] = jnp.exp(b_end)

    st = jnp.zeros((dk, dk), jnp.float32)
    for n in range(nc):
        rows = slice(n * C, (n + 1) * C)
        sp_ref[n] = st.astype(jnp.bfloat16)
        upd_t = lax.dot_general(vb_ref[rows, :], ku_ref[rows, :], (((0,), (0,)), ((), ())),
                                preferred_element_type=jnp.float32)
        st = dec_ref[n] * st + upd_t

    causal = lax.broadcasted_iota(jnp.int32, (C, C), 0) >= lax.broadcasted_iota(jnp.int32, (C, C), 1)
    for n in range(nc):
        rows = slice(n * C, (n + 1) * C)
        qd = qd_ref[rows, :]
        attn = lax.dot_general(qd, kd_ref[rows, :], (((1,), (1,)), ((), ())), preferred_element_type=jnp.float32)
        attn = jnp.where(causal, attn, 0.0).astype(jnp.bfloat16)
        o = jnp.dot(attn, vb_ref[rows, :], preferred_element_type=jnp.float32)
        o = o + lax.dot_general(qd, sp_ref[n], (((1,), (1,)), ((), ())), preferred_element_type=jnp.float32)
        o = o * lax.rsqrt(jnp.mean(o * o, axis=-1, keepdims=True) + NORM_EPS) * nw_ref[...]
        o_ref[rows, :] = o * jax.nn.silu(g_ref[rows, :])


def _hgrn2(proj, lb, norm_w, batch):
    n = proj.shape[0]
    t = n // batch
    dk, H = HG_HEAD_DIM, HG_HEADS
    nc = t // HG_CHUNK

    def part(p):
        return pl.BlockSpec((t, dk), lambda b, h: (b, p * H + h))

    slab = pltpu.VMEM((t, dk), jnp.bfloat16)
    return pl.pallas_call(
        _hgrn2_body,
        grid=(batch, H),
        in_specs=[part(0), part(1), part(2), part(3),
                  pl.BlockSpec((1, dk), lambda b, h: (0, h)),
                  pl.BlockSpec((1, dk), lambda b, h: (0, 0))],
        out_specs=pl.BlockSpec((t, dk), lambda b, h: (b, h)),
        out_shape=jax.ShapeDtypeStruct((n, H * dk), jnp.float32),
        scratch_shapes=[slab, slab, slab, slab, pltpu.VMEM((nc, 1, dk), jnp.float32),
                        pltpu.VMEM((nc, dk, dk), jnp.bfloat16)],
        compiler_params=pltpu.CompilerParams(
            dimension_semantics=("arbitrary", "arbitrary"), vmem_limit_bytes=VMEM_LIMIT_BYTES),
        name="hgrn2",
    )(proj, proj, proj, proj, lb.reshape(1, H * dk), norm_w.reshape(1, dk))


GELU_C0 = float(np.sqrt(2.0 / np.pi))
GELU_C1 = GELU_C0 * 0.044715


def _gelu(x):
    half = 0.5 * x
    return half + half * jnp.tanh(x * (GELU_C0 + GELU_C1 * (x * x)))


def _head_norm(x, w):
    return x * lax.rsqrt(jnp.mean(x * x, axis=-1, keepdims=True) + NORM_EPS) * w


def _value_slab_t(v):
    lane = lax.broadcasted_iota(jnp.int32, v.shape, 1)
    return jnp.concatenate([v, jnp.where(lane == 0, 1.0, 0.0)], axis=-1).T.astype(jnp.bfloat16)


POS_BASE = 64


def _key_slab(k, pos):
    lane = lax.broadcasted_iota(jnp.int32, k.shape, 1)
    hi = (pos // POS_BASE).astype(jnp.float32)
    lo = (pos % POS_BASE).astype(jnp.float32)
    tail = jnp.where(lane < 2, hi, jnp.where(lane < 4, lo, 0.0))
    return jnp.concatenate([k, tail], axis=-1).astype(jnp.bfloat16)


def _kv_prep_body(c_ref, s_ref, w_ref, ksw_ref, kww_ref, kn_ref, vv_ref, cf_ref, *, seq):
    dh = NSA_HEAD_DIM
    tm = c_ref.shape[0]
    pos = (pl.program_id(0) * tm) % seq + lax.broadcasted_iota(jnp.int32, (tm, dh), 0)
    for h in range(NSA_KV_HEADS):
        k_cols = slice(h * dh, (h + 1) * dh)
        v_cols = slice(KV_W + h * dh, KV_W + (h + 1) * dh)
        cf_ref[0, h] = c_ref[:, k_cols]
        cf_ref[1, h] = c_ref[:, v_cols]
        kn_ref[0, h] = _key_slab(_head_norm(s_ref[:, k_cols], ksw_ref[...]), pos)
        vv_ref[0, h] = _value_slab_t(s_ref[:, v_cols])
        kn_ref[1, h] = _key_slab(_head_norm(w_ref[:, k_cols], kww_ref[...]), pos)
        vv_ref[1, h] = _value_slab_t(w_ref[:, v_cols])


def _kv_prep(proj, ks_w, kw_w, tm, seq):
    n = proj.shape[0]
    dh = NSA_HEAD_DIM
    pair = 2 * KV_W
    col0 = KV_COL0 // pair
    head_block = pl.BlockSpec((2, NSA_KV_HEADS, tm, dh), lambda i: (0, 0, i, 0))
    wide_block = pl.BlockSpec((2, NSA_KV_HEADS, tm, 2 * dh), lambda i: (0, 0, i, 0))
    w_spec = pl.BlockSpec((1, dh), lambda i: (0, 0))
    return pl.pallas_call(
        functools.partial(_kv_prep_body, seq=seq),
        grid=(n // tm,),
        in_specs=[pl.BlockSpec((tm, pair), lambda i: (i, col0)),
                  pl.BlockSpec((tm, pair), lambda i: (i, col0 + 1)),
                  pl.BlockSpec((tm, pair), lambda i: (i, col0 + 2)),
                  w_spec, w_spec],
        out_specs=[wide_block, pl.BlockSpec((2, NSA_KV_HEADS, 2 * dh, tm), lambda i: (0, 0, 0, i)), head_block],
        out_shape=[jax.ShapeDtypeStruct((2, NSA_KV_HEADS, n, 2 * dh), jnp.bfloat16),
                   jax.ShapeDtypeStruct((2, NSA_KV_HEADS, 2 * dh, n), jnp.bfloat16),
                   jax.ShapeDtypeStruct((2, NSA_KV_HEADS, n, dh), jnp.float32)],
        compiler_params=pltpu.CompilerParams(
            dimension_semantics=("arbitrary",), vmem_limit_bytes=VMEM_LIMIT_BYTES),
        name="kv_prep",
    )(proj, proj, proj, ks_w.reshape(1, dh), kw_w.reshape(1, dh))


def _cmp_kv_body(r_ref, pos_ref, w1_ref, w2_ref, nw_ref, o_ref):
    half = CMP_STRIDE * NSA_HEAD_DIM
    n_strips = r_ref.shape[2]
    for kind in range(2):
        strips = r_ref[kind, 0]
        top = (strips + pos_ref[kind, 0:1, :]).astype(jnp.bfloat16)
        bot = (strips + pos_ref[kind, 1:2, :]).astype(jnp.bfloat16)
        a = jnp.dot(top, w1_ref[kind, :half, :], preferred_element_type=jnp.float32)
        b = jnp.dot(bot, w1_ref[kind, half:, :], preferred_element_type=jnp.float32)
        hid = _gelu(a + pltpu.roll(b, n_strips - 1, 0))
        out = jnp.dot(hid.astype(jnp.bfloat16), w2_ref[kind], preferred_element_type=jnp.float32)
        if kind == 0:
            out = _head_norm(out, nw_ref[...])
        o_ref[kind, 0, 0] = out.astype(jnp.bfloat16)


def _cmp_kv(cf, pos, w1, w2, kc_w, batch):
    _, hkv, n, dh = cf.shape
    t = n // batch
    n_strips = t // CMP_STRIDE
    strips = cf.reshape(2, hkv, n // CMP_STRIDE, CMP_STRIDE * dh)
    return pl.pallas_call(
        _cmp_kv_body,
        grid=(batch, hkv),
        in_specs=[pl.BlockSpec((2, 1, n_strips, CMP_STRIDE * dh), lambda b, g: (0, g, b, 0)),
                  pl.BlockSpec(pos.shape, lambda b, g: (0, 0, 0)),
                  pl.BlockSpec(w1.shape, lambda b, g: (0, 0, 0)),
                  pl.BlockSpec(w2.shape, lambda b, g: (0, 0, 0)),
                  pl.BlockSpec((1, dh), lambda b, g: (0, 0))],
        out_specs=pl.BlockSpec((2, 1, 1, n_strips, dh), lambda b, g: (0, g, b, 0, 0)),
        out_shape=jax.ShapeDtypeStruct((2, hkv, batch, n_strips, dh), jnp.bfloat16),
        compiler_params=pltpu.CompilerParams(
            dimension_semantics=("arbitrary", "arbitrary"), vmem_limit_bytes=VMEM_LIMIT_BYTES),
        name="cmp_kv",
    )(strips, pos, w1, w2, kc_w.reshape(1, dh))


def _flash_step(qt_ref, k, vt, m_ref, acc_ref, base, mask_fn, tq):
    tk = k.shape[0]
    dist = base + lax.broadcasted_iota(jnp.int32, (tk, tq), 1) - lax.broadcasted_iota(jnp.int32, (tk, tq), 0)
    off = jnp.where(mask_fn(dist), 0.0, MASK_OFF)
    for r in range(NSA_GROUP):
        cols = slice(r * tq, (r + 1) * tq)
        s = jnp.dot(k, qt_ref[:, cols], preferred_element_type=jnp.float32) - off
        m_old = m_ref[:, cols]
        m_new = jnp.maximum(m_old, jnp.max(s, axis=0, keepdims=True))
        p = jnp.exp2(s - m_new).astype(jnp.bfloat16)
        acc_ref[:, cols] = jnp.exp2(m_old - m_new) * acc_ref[:, cols] \
            + jnp.dot(vt, p, preferred_element_type=jnp.float32)
        m_ref[:, cols] = m_new


def _nsa_attn_body(q_ref, gate_ref, kc_ref, vc_ref, ks_ref, vs_ref, kw_ref, vw_ref, qw_ref, slope_ref, o_ref,
                   qt_ref, sel_ref, oc_ref, ms_ref, as_ref, mw_ref, aw_ref, *, tq, tk, n_sel):
    i = pl.program_id(2)
    j = pl.program_id(3)
    G, dh = NSA_GROUP, NSA_HEAD_DIM
    last_j = (i * tq + tq - 1) // tk

    @pl.when(j == 0)
    def _():
        x = q_ref[...]
        for r in range(G):
            qn = _head_norm(x[:, r * dh:(r + 1) * dh], qw_ref[...]) * (dh ** -0.5 * LOG2E)
            slope = slope_ref[0, r]
            s_hi = slope.astype(jnp.bfloat16).astype(jnp.float32)
            s_lo = slope - s_hi
            lane = lax.broadcasted_iota(jnp.int32, (tq, dh), 1)
            tail = jnp.where(lane == 0, POS_BASE * s_hi, jnp.where(lane == 1, POS_BASE * s_lo,
                             jnp.where(lane == 2, s_hi, jnp.where(lane == 3, s_lo, 0.0))))
            qt_ref[:, r * tq:(r + 1) * tq] = jnp.concatenate([qn, tail], axis=-1).T.astype(jnp.bfloat16)
        for ref in (ms_ref, mw_ref):
            ref[...] = jnp.full(ref.shape, NEG_INF, jnp.float32)
        for ref in (as_ref, aw_ref):
            ref[...] = jnp.zeros(ref.shape, jnp.float32)

        n_cmp_pad = kc_ref.shape[-2]
        t_pos = i * tq + lax.broadcasted_iota(jnp.int32, (n_cmp_pad, tq), 1)
        blk = lax.broadcasted_iota(jnp.int32, (n_cmp_pad, tq), 0)
        dist = t_pos - (blk * CMP_STRIDE + CMP_BLOCK - 1)
        valid = (dist >= 0) & (blk < n_cmp_pad - 1)
        distf = dist.astype(jnp.float32)
        cb = lax.broadcasted_iota(jnp.int32, (n_sel, n_cmp_pad), 1) * CMP_STRIDE
        sb = lax.broadcasted_iota(jnp.int32, (n_sel, n_cmp_pad), 0) * SEL_BLOCK
        overlap_t = ((cb < sb + SEL_BLOCK) & (cb + CMP_BLOCK > sb)).astype(jnp.bfloat16)
        imp = None
        for r in range(G):
            cols = slice(r * tq, (r + 1) * tq)
            s = jnp.dot(kc_ref[0, 0, 0], qt_ref[:dh, cols], preferred_element_type=jnp.float32)
            s = jnp.where(valid, s - slope_ref[0, r] * distf, NEG_INF)
            p = jnp.where(valid, jnp.exp2(s - jnp.max(s, axis=0, keepdims=True)), 0.0)
            denom = jnp.maximum(jnp.sum(p, axis=0, keepdims=True), 1e-30)
            p = (p / denom).astype(jnp.bfloat16)
            oc_ref[:, cols] = lax.dot_general(vc_ref[0, 0, 0], p, (((0,), (0,)), ((), ())),
                                              preferred_element_type=jnp.float32)
            part = jnp.dot(overlap_t, p, preferred_element_type=jnp.float32)
            imp = part if imp is None else imp + part

        cur = (i * tq + lax.broadcasted_iota(jnp.int32, (n_sel, tq), 1)) // SEL_BLOCK
        jb = lax.broadcasted_iota(jnp.int32, (n_sel, tq), 0)
        forced = (jb == 0) | ((jb <= cur) & (jb > cur - N_LOCAL))
        imp = jnp.where(forced, FORCE_SCORE, jnp.where(jb > cur, -FORCE_SCORE, imp))
        rank = jnp.zeros((n_sel, tq), jnp.float32)
        for c in range(n_sel):
            row = imp[c:c + 1, :]
            ahead = (row > imp) | ((row == imp) & (jb > c))
            rank = rank + jnp.where(ahead, 1.0, 0.0)
        sel_t = jnp.where(rank < min(N_SELECT, n_sel), 1.0, 0.0)
        sel_t = jnp.concatenate([sel_t, jnp.zeros((sel_ref.shape[0] - n_sel, tq), jnp.float32)], axis=0)
        sel_ref[...] = sel_t.astype(jnp.bfloat16)

    @pl.when(j <= last_j)
    def _():
        n_pad = sel_ref.shape[0]
        key_blk = j * (tk // SEL_BLOCK) + lax.broadcasted_iota(jnp.int32, (tk, n_pad), 0) // SEL_BLOCK
        expand_t = (key_blk == lax.broadcasted_iota(jnp.int32, (tk, n_pad), 1)).astype(jnp.bfloat16)

        def mask(dist):
            picked = jnp.dot(expand_t, sel_ref[...], preferred_element_type=jnp.float32)
            return (dist >= 0) & (picked > 0.5)

        _flash_step(qt_ref, ks_ref[0, 0], vs_ref[0, 0], ms_ref, as_ref, i * tq - j * tk, mask, tq)

    @pl.when((j <= last_j) & (j * tk + tk - 1 >= i * tq - (WINDOW - 1)))
    def _():
        _flash_step(qt_ref, kw_ref[0, 0], vw_ref[0, 0], mw_ref, aw_ref, i * tq - j * tk,
                    lambda dist: (dist >= 0) & (dist < WINDOW), tq)

    @pl.when(j == pl.num_programs(3) - 1)
    def _():
        gates = jax.nn.sigmoid(gate_ref[0])
        outs = []
        for r in range(G):
            cols = slice(r * tq, (r + 1) * tq)
            o_sel = as_ref[:dh, cols] / as_ref[dh:dh + 1, cols]
            o_win = aw_ref[:dh, cols] / aw_ref[dh:dh + 1, cols]
            outs.append(gates[3 * r:3 * r + 1] * oc_ref[:, cols] + gates[3 * r + 1:3 * r + 2] * o_sel
                        + gates[3 * r + 2:3 * r + 3] * o_win)
        o_ref[...] = jnp.concatenate(outs, axis=0).T


def _nsa_attn(proj, gates, cmp_kv, kn, vv, q_w, slopes, batch, tq=512, tk=512):
    n = proj.shape[0]
    t = n // batch
    G, dh, hkv = NSA_GROUP, NSA_HEAD_DIM, NSA_KV_HEADS
    n_cmp_pad = cmp_kv.shape[-2]
    n_sel = t // SEL_BLOCK
    q_blk0 = Q_COL0 // (G * dh)

    def kv_tile(b, i, j):
        return b * (t // tk) + jnp.minimum(j, (i * tq + tq - 1) // tk)

    def k_map(kind):
        return lambda b, g, i, j: (kind, g, kv_tile(b, i, j), 0)

    def v_map(kind):
        return lambda b, g, i, j: (kind, g, 0, kv_tile(b, i, j))

    def cmp_map(kind):
        return lambda b, g, i, j: (kind, g, b, 0, 0)

    cmp_block = (1, 1, 1, n_cmp_pad, dh)
    k_block = (1, 1, tk, 2 * dh)
    v_block = (1, 1, 2 * dh, tk)
    slab = pltpu.VMEM((dh, G * tq), jnp.float32)
    wide = pltpu.VMEM((2 * dh, G * tq), jnp.float32)
    stat = pltpu.VMEM((1, G * tq), jnp.float32)
    return pl.pallas_call(
        functools.partial(_nsa_attn_body, tq=tq, tk=tk, n_sel=n_sel),
        grid=(batch, hkv, t // tq, t // tk),
        in_specs=[
            pl.BlockSpec((tq, G * dh), lambda b, g, i, j: (b * (t // tq) + i, q_blk0 + g)),
            pl.BlockSpec((1, 3 * G, tq), lambda b, g, i, j: (g, 0, b * (t // tq) + i)),
            pl.BlockSpec(cmp_block, cmp_map(0)), pl.BlockSpec(cmp_block, cmp_map(1)),
            pl.BlockSpec(k_block, k_map(0)), pl.BlockSpec(v_block, v_map(0)),
            pl.BlockSpec(k_block, k_map(1)), pl.BlockSpec(v_block, v_map(1)),
            pl.BlockSpec((1, dh), lambda b, g, i, j: (0, 0)),
            pl.BlockSpec((1, G, 1, 1), lambda b, g, i, j: (g, 0, 0, 0)),
        ],
        out_specs=pl.BlockSpec((tq, G * dh), lambda b, g, i, j: (b * (t // tq) + i, g)),
        out_shape=jax.ShapeDtypeStruct((n, hkv * G * dh), jnp.float32),
        scratch_shapes=[pltpu.VMEM((2 * dh, G * tq), jnp.bfloat16), pltpu.VMEM((LANES, tq), jnp.bfloat16),
                        slab, stat, wide, stat, wide],
        compiler_params=pltpu.CompilerParams(
            dimension_semantics=("arbitrary",) * 4, vmem_limit_bytes=VMEM_LIMIT_BYTES),
        name="nsa_attn",
    )(proj, gates, cmp_kv, cmp_kv, kn, vv, kn, vv, q_w.reshape(1, dh), slopes)


def _nsa(proj, batch, q_norm_w, kc_norm_w, ks_norm_w, kw_norm_w, pos_k, pos_v, w_ck1, w_ck2, w_cv1, w_cv2):
    n = proj.shape[0]
    G, hkv = NSA_GROUP, NSA_KV_HEADS
    kn, vv, cf = _kv_prep(proj, ks_norm_w, kw_norm_w, 512, n // batch)
    half = CMP_STRIDE * NSA_HEAD_DIM
    pos = jnp.stack([pos_k.reshape(2, half), pos_v.reshape(2, half)])
    w1 = jnp.stack([w_ck1, w_cv1]).astype(jnp.bfloat16)
    w2 = jnp.stack([w_ck2, w_cv2]).astype(jnp.bfloat16)
    cmp_kv = _cmp_kv(cf, pos, w1, w2, kc_norm_w, batch)
    gates = proj[:, GATE_COL0:GATE_COL0 + N_GATES].reshape(n, hkv, 3 * G).transpose(1, 2, 0)
    slopes = jnp.asarray(2.0 ** (-8.0 * np.arange(1, NSA_Q_HEADS + 1) / NSA_Q_HEADS), jnp.float32) * LOG2E
    return _nsa_attn(proj, gates, cmp_kv, kn, vv, q_norm_w, slopes.reshape(hkv, G, 1, 1), batch)


BIG_NEG = -3.0e38
PEER_CAND = [(a, b) for a in range(PEER_TOPK) for b in range(PEER_TOPK) if (a + 1) * (b + 1) <= PEER_TOPK]
PEER_CAND_ROWS = -(-len(PEER_CAND) // 8) * 8


def _peer_route_body(h_ref, g_ref, wq_ref, keys_ref, xt_ref, rk2_ref, a2_ref, cnt_ref, r_ref,
                     qt_ref, sc_ref, cur_ref, top_ref, cand_ref):
    x = h_ref[...]
    hn = x * lax.rsqrt(jnp.mean(x * x, axis=-1, keepdims=True) + NORM_EPS) * g_ref[...]
    hnt = hn.T.astype(jnp.bfloat16)
    xt_ref[...] = hnt
    qt_ref[...] = jnp.dot(wq_ref[...], hnt, preferred_element_type=jnp.float32).astype(jnp.bfloat16)
    cand_ref[...] = jnp.full(cand_ref.shape, BIG_NEG, jnp.float32)
    kd = PEER_KEY_DIM // 2
    n_parts = 2 * PEER_HEADS
    for hp in range(n_parts):
        s = jnp.dot(keys_ref[hp], qt_ref[hp * kd:(hp + 1) * kd, :], preferred_element_type=jnp.float32)
        sc_ref[hp] = s
        cur_ref[hp] = s

    def extract(k, carry):
        cur = cur_ref[...]
        mk = jnp.max(cur, axis=1, keepdims=True)
        top_ref[k] = mk
        cur_ref[...] = jnp.where(cur == mk, BIG_NEG, cur)
        return carry

    lax.fori_loop(0, PEER_TOPK, extract, 0)

    def head(hd, carry):
        t1 = [top_ref[a, 2 * hd] for a in range(PEER_TOPK)]
        t2 = [top_ref[b, 2 * hd + 1] for b in range(PEER_TOPK)]
        cmax = t1[0] + t2[0]
        cands = [t1[a] + t2[b] for a, b in PEER_CAND]
        for i, c in enumerate(cands):
            cand_ref[i:i + 1, :] = c
        call = cand_ref[...]
        n_gt = jnp.zeros(call.shape, jnp.float32)
        for c in cands:
            n_gt = n_gt + jnp.where(c > call, 1.0, 0.0)
        tau = jnp.min(jnp.where(n_gt <= PEER_TOPK - 1, call, -BIG_NEG), axis=0, keepdims=True)
        z = jnp.sum(jnp.where(call >= tau, jnp.exp(call - cmax), 0.0), axis=0, keepdims=True)
        count = [None] * PEER_TOPK
        for (a, b), c in zip(PEER_CAND, cands):
            hit = jnp.where(c >= tau, 1.0, 0.0)
            count[a] = hit if count[a] is None else count[a] + hit
        s1 = sc_ref[2 * hd]
        s2 = sc_ref[2 * hd + 1]
        cnt = jnp.zeros(s1.shape, jnp.float32)
        rank2 = jnp.zeros(s2.shape, jnp.float32)
        for a in range(PEER_TOPK):
            cnt = jnp.where(s1 == t1[a], count[a], cnt)
            rank2 = rank2 + jnp.where(t2[a] > s2, 1.0, 0.0)
        rk2_ref[hd] = rank2.astype(jnp.bfloat16)
        a2_ref[hd] = jnp.exp(s2 - t2[0]).astype(jnp.bfloat16)
        cnt_ref[hd] = cnt
        r_ref[hd] = jnp.exp(s1 - t1[0]) / z
        return carry

    lax.fori_loop(0, PEER_HEADS, head, 0)


def _peer_route(h, gain, wq_t, keys, tt):
    n, d = h.shape
    hp, nk, kd = keys.shape
    stat = jax.ShapeDtypeStruct((PEER_HEADS, nk, n), jnp.float32)
    stat_bf16 = jax.ShapeDtypeStruct((PEER_HEADS, nk, n), jnp.bfloat16)
    stat_spec = pl.BlockSpec((PEER_HEADS, nk, tt), lambda i: (0, 0, i))
    return pl.pallas_call(
        _peer_route_body,
        grid=(n // tt,),
        in_specs=[
            pl.BlockSpec((tt, d), lambda i: (i, 0)),
            pl.BlockSpec((1, d), lambda i: (0, 0)),
            pl.BlockSpec(wq_t.shape, lambda i: (0, 0)),
            pl.BlockSpec(keys.shape, lambda i: (0, 0, 0)),
        ],
        out_specs=[pl.BlockSpec((d, tt), lambda i: (0, i)), stat_spec, stat_spec, stat_spec, stat_spec],
        out_shape=[jax.ShapeDtypeStruct((d, n), jnp.bfloat16), stat_bf16, stat_bf16, stat, stat],
        scratch_shapes=[pltpu.VMEM((wq_t.shape[0], tt), jnp.bfloat16),
                        pltpu.VMEM((hp, nk, tt), jnp.float32), pltpu.VMEM((hp, nk, tt), jnp.float32),
                        pltpu.VMEM((PEER_TOPK, hp, 1, tt), jnp.float32),
                        pltpu.VMEM((PEER_CAND_ROWS, tt), jnp.float32)],
        compiler_params=pltpu.CompilerParams(
            dimension_semantics=("arbitrary",), vmem_limit_bytes=VMEM_LIMIT_BYTES),
        name="peer_route",
    )(h, gain.reshape(1, d), wq_t, keys)


def _peer_experts_body(xt_ref, u_ref, vt_ref, rk2_ref, a2_ref, cnt_ref, r_ref, o_ref, *, n_i1):
    @pl.when(pl.program_id(1) == 0)
    def _():
        o_ref[...] = jnp.zeros(o_ref.shape, jnp.float32)

    act = _gelu(jnp.dot(u_ref[...], xt_ref[...], preferred_element_type=jnp.float32))
    parts = []
    for i1 in range(n_i1):
        w = None
        for hd in range(PEER_HEADS):
            picked = rk2_ref[hd] < cnt_ref[hd, i1:i1 + 1, :].astype(jnp.bfloat16)
            term = jnp.where(picked, a2_ref[hd], 0.0) * r_ref[hd, i1:i1 + 1, :].astype(jnp.bfloat16)
            w = term if w is None else w + term
        parts.append(w * act[i1 * N_KEYS:(i1 + 1) * N_KEYS].astype(jnp.bfloat16))
    p = jnp.concatenate(parts, axis=0)
    o_ref[...] += jnp.dot(vt_ref[...], p, preferred_element_type=jnp.float32)


def _peer_experts(xt, u, vt, rk2, a2, cnt, r, tt, te):
    d, n = xt.shape
    e = u.shape[0]
    n_i1 = te // N_KEYS
    full = pl.BlockSpec((PEER_HEADS, N_KEYS, tt), lambda i, j: (0, 0, i))
    part = pl.BlockSpec((PEER_HEADS, n_i1, tt), lambda i, j: (0, j, i))
    return pl.pallas_call(
        functools.partial(_peer_experts_body, n_i1=n_i1),
        grid=(n // tt, e // te),
        in_specs=[
            pl.BlockSpec((d, tt), lambda i, j: (0, i)),
            pl.BlockSpec((te, d), lambda i, j: (j, 0)),
            pl.BlockSpec((d, te), lambda i, j: (0, j)),
            full, full, part, part,
        ],
        out_specs=pl.BlockSpec((d, tt), lambda i, j: (0, i)),
        out_shape=jax.ShapeDtypeStruct((d, n), jnp.float32),
        compiler_params=pltpu.CompilerParams(
            dimension_semantics=("arbitrary", "arbitrary"), vmem_limit_bytes=VMEM_LIMIT_BYTES),
        name="peer_experts",
    )(xt, u, vt, rk2, a2, cnt, r)


def _transpose_cast_body(x_ref, o_ref):
    o_ref[...] = x_ref[...].T.astype(o_ref.dtype)


def _transpose_cast(x, blk, dtype):
    r, c = x.shape
    return pl.pallas_call(
        _transpose_cast_body,
        grid=(r // blk, c // blk),
        in_specs=[pl.BlockSpec((blk, blk), lambda i, j: (i, j))],
        out_specs=pl.BlockSpec((blk, blk), lambda i, j: (j, i)),
        out_shape=jax.ShapeDtypeStruct((c, r), dtype),
        compiler_params=pltpu.CompilerParams(
            dimension_semantics=("arbitrary", "arbitrary"), vmem_limit_bytes=VMEM_LIMIT_BYTES),
        name="transpose_cast",
    )(x)


def _transpose_add_body(x_ref, r_ref, o_ref):
    o_ref[...] = r_ref[...] + x_ref[...].T


def _transpose_add(x_t, res, blk):
    d, n = x_t.shape
    return pl.pallas_call(
        _transpose_add_body,
        grid=(d // blk, n // blk),
        in_specs=[pl.BlockSpec((blk, blk), lambda i, j: (i, j)),
                  pl.BlockSpec((blk, blk), lambda i, j: (j, i))],
        out_specs=pl.BlockSpec((blk, blk), lambda i, j: (j, i)),
        out_shape=jax.ShapeDtypeStruct((n, d), jnp.float32),
        compiler_params=pltpu.CompilerParams(
            dimension_semantics=("arbitrary", "arbitrary"), vmem_limit_bytes=VMEM_LIMIT_BYTES),
        name="transpose_add",
    )(x_t, res)


def _peer(h, gain, w_q, sub_keys, u_tab, v_tab):
    keys = sub_keys.reshape(PEER_HEADS * 2, N_KEYS, PEER_KEY_DIM // 2).astype(jnp.bfloat16)
    xt, rk2, a2, cnt, r = _peer_route(h, gain, _transpose_cast(w_q, 1024, jnp.bfloat16), keys, 256)
    out_t = _peer_experts(xt, u_tab.astype(jnp.bfloat16), _transpose_cast(v_tab, 1024, jnp.bfloat16),
                          rk2, a2, cnt, r, 512, 1024)
    return _transpose_add(out_t, h, min(1024, h.shape[0]))


def kernel(x, norm1_w, w_in, hg_lb_logits, hg_norm_w, q_norm_w, kc_norm_w, ks_norm_w, kw_norm_w,
           cmp_pos_k, cmp_pos_v, w_ck1, w_ck2, w_cv1, w_cv2, w_out, norm2_w,
           peer_w_q, peer_sub_keys, peer_u, peer_v):
    B, T, D = x.shape
    n = B * T
    layer = 0
    lower_bounds = jnp.cumsum(jax.nn.softmax(hg_lb_logits, axis=0), axis=0)
    xt = x.reshape(n, D)

    w_in_b = jnp.pad(w_in[layer].astype(jnp.bfloat16), ((0, 0), (0, IN_COLS_PADDED - IN_COLS)))
    proj = _norm_matmul(xt, norm1_w[layer], w_in_b, 1024, 1024)
    hg_out = _hgrn2(proj, lower_bounds[layer], hg_norm_w[layer], B)
    nsa_out = _nsa(proj, B, q_norm_w[layer], kc_norm_w[layer], ks_norm_w[layer], kw_norm_w[layer],
                   cmp_pos_k[layer], cmp_pos_v[layer], w_ck1[layer], w_ck2[layer], w_cv1[layer], w_cv2[layer])
    h = _out_proj(hg_out, nsa_out, w_out[layer].astype(jnp.bfloat16), xt, 512, 1024)

    y = _peer(h, norm2_w[layer], peer_w_q[layer], peer_sub_keys[layer], peer_u[layer], peer_v[layer])
    return y.reshape(B, T, D)
```

```python
import functools

import jax
import jax.numpy as jnp
import numpy as np
from jax import lax
from jax.experimental import pallas as pl
from jax.experimental.pallas import tpu as pltpu

D_MODEL = 2048
HG_WIDTH = 1024
HG_HEAD_DIM = 128
HG_HEADS = 8
HG_CHUNK = 64
NSA_WIDTH = 1024
NSA_HEAD_DIM = 64
NSA_Q_HEADS = 16
NSA_KV_HEADS = 4
NSA_GROUP = 4
CMP_BLOCK = 32
CMP_STRIDE = 16
CMP_HIDDEN = 256
SEL_BLOCK = 64
N_SELECT = 16
N_LOCAL = 2
WINDOW = 512
FORCE_SCORE = 1e9
NEG_INF = -1e30
MASK_OFF = 3e30
LOG2E = float(np.log2(np.e))
PEER_HEADS = 8
N_KEYS = 128
PEER_KEY_DIM = 256
PEER_TOPK = 16
NORM_EPS = 1e-6
KV_W = NSA_KV_HEADS * NSA_HEAD_DIM
N_GATES = 3 * NSA_Q_HEADS
IN_SIZES = [HG_WIDTH] * 4 + [NSA_WIDTH] + [KV_W] * 6 + [N_GATES]
IN_COLS = sum(IN_SIZES)
IN_COLS_PADDED = 7168
Q_COL0 = 4 * HG_WIDTH
KV_COL0 = Q_COL0 + NSA_WIDTH
GATE_COL0 = KV_COL0 + 6 * KV_W

VMEM_LIMIT_BYTES = 48 * 1024 * 1024
LANES = 128
PEER_VMEM_LIMIT_BYTES = 60 * 1024 * 1024


def _norm_matmul_body(x_ref, g_ref, w_ref, o_ref, xn_ref):
    @pl.when(pl.program_id(1) == 0)
    def _():
        x = x_ref[...]
        r = lax.rsqrt(jnp.mean(x * x, axis=-1, keepdims=True) + NORM_EPS)
        xn_ref[...] = (x * r * g_ref[...]).astype(jnp.bfloat16)

    o_ref[...] = jnp.dot(xn_ref[...], w_ref[...], preferred_element_type=jnp.float32).astype(o_ref.dtype)


def _norm_matmul(x, gain, w, tm, tn, out_dtype=jnp.float32):
    m, k = x.shape
    n = w.shape[1]
    return pl.pallas_call(
        _norm_matmul_body,
        grid=(m // tm, n // tn),
        in_specs=[
            pl.BlockSpec((tm, k), lambda i, j: (i, 0)),
            pl.BlockSpec((1, k), lambda i, j: (0, 0)),
            pl.BlockSpec((k, tn), lambda i, j: (0, j)),
        ],
        out_specs=pl.BlockSpec((tm, tn), lambda i, j: (i, j)),
        out_shape=jax.ShapeDtypeStruct((m, n), out_dtype),
        scratch_shapes=[pltpu.VMEM((tm, k), jnp.bfloat16)],
        compiler_params=pltpu.CompilerParams(
            dimension_semantics=("arbitrary", "arbitrary"), vmem_limit_bytes=VMEM_LIMIT_BYTES),
        name="norm_matmul",
    )(x, gain.reshape(1, k), w)


def _out_proj_body(a1_ref, a2_ref, w1_ref, w2_ref, r_ref, o_ref):
    acc = jnp.dot(a1_ref[...].astype(jnp.bfloat16), w1_ref[...], preferred_element_type=jnp.float32)
    acc = acc + jnp.dot(a2_ref[...].astype(jnp.bfloat16), w2_ref[...], preferred_element_type=jnp.float32)
    o_ref[...] = r_ref[...] + acc


def _out_proj(a1, a2, w, res, tm, tn):
    m, k1 = a1.shape
    k2 = a2.shape[1]
    n = w.shape[1]
    return pl.pallas_call(
        _out_proj_body,
        grid=(m // tm, n // tn),
        in_specs=[
            pl.BlockSpec((tm, k1), lambda i, j: (i, 0)),
            pl.BlockSpec((tm, k2), lambda i, j: (i, 0)),
            pl.BlockSpec((k1, tn), lambda i, j: (0, j)),
            pl.BlockSpec((k2, tn), lambda i, j: (k1 // k2, j)),
            pl.BlockSpec((tm, tn), lambda i, j: (i, j)),
        ],
        out_specs=pl.BlockSpec((tm, tn), lambda i, j: (i, j)),
        out_shape=jax.ShapeDtypeStruct((m, n), jnp.float32),
        compiler_params=pltpu.CompilerParams(
            dimension_semantics=("arbitrary", "arbitrary"), vmem_limit_bytes=VMEM_LIMIT_BYTES),
        name="out_proj",
    )(a1, a2, w, w, res)


def _hgrn2_body(q_ref, f_ref, v_ref, g_ref, lb_ref, nw_ref, o_ref, qd_ref, kd_ref, ku_ref, vb_ref, dec_ref, sp_ref):
    t, dk = q_ref.shape
    C = HG_CHUNK
    nc = t // C
    lb = lb_ref[...]
    f = lb + (1.0 - lb) * jax.nn.sigmoid(f_ref[...])
    kf = 1.0 - f
    b = jnp.log(f)
    row = lax.broadcasted_iota(jnp.int32, (t, dk), 0) % C
    shift = 1
    while shift < C:
        b = b + jnp.where(row >= shift, pltpu.roll(b, shift, 0), 0.0)
        shift *= 2
    b3 = b.reshape(nc, C, dk)
    b_end = b3[:, C - 1:C, :]
    qd_ref[...] = (q_ref[...] * jnp.exp(b)).astype(jnp.bfloat16)
    kd_ref[...] = (kf * jnp.exp(-b)).astype(jnp.bfloat16)
    ku_ref[...] = (kf.reshape(nc, C, dk) * jnp.exp(b_end - b3)).reshape(t, dk).astype(jnp.bfloat16)
    vb_ref[...] = v_ref[...].astype(jnp.bfloat16)
    dec_ref[...] = jnp.exp(b_end)

    st = jnp.zeros((dk, dk), jnp.float32)
    for n in range(nc):
        rows = slice(n * C, (n + 1) * C)
        sp_ref[n] = st.astype(jnp.bfloat16)
        upd_t = lax.dot_general(vb_ref[rows, :], ku_ref[rows, :], (((0,), (0,)), ((), ())),
                                preferred_element_type=jnp.float32)
        st = dec_ref[n] * st + upd_t

    causal = lax.broadcasted_iota(jnp.int32, (C, C), 0) >= lax.broadcasted_iota(jnp.int32, (C, C), 1)
    for n in range(nc):
        rows = slice(n * C, (n + 1) * C)
        qd = qd_ref[rows, :]
        attn = lax.dot_general(qd, kd_ref[rows, :], (((1,), (1,)), ((), ())), preferred_element_type=jnp.float32)
        attn = jnp.where(causal, attn, 0.0).astype(jnp.bfloat16)
        o = jnp.dot(attn, vb_ref[rows, :], preferred_element_type=jnp.float32)
        o = o + lax.dot_general(qd, sp_ref[n], (((1,), (1,)), ((), ())), preferred_element_type=jnp.float32)
        o = o * lax.rsqrt(jnp.mean(o * o, axis=-1, keepdims=True) + NORM_EPS) * nw_ref[...]
        o_ref[rows, :] = o * jax.nn.silu(g_ref[rows, :])


def _hgrn2(proj, lb, norm_w, batch):
    n = proj.shape[0]
    t = n // batch
    dk, H = HG_HEAD_DIM, HG_HEADS
    nc = t // HG_CHUNK

    def part(p):
        return pl.BlockSpec((t, dk), lambda b, h: (b, p * H + h))

    slab = pltpu.VMEM((t, dk), jnp.bfloat16)
    return pl.pallas_call(
        _hgrn2_body,
        grid=(batch, H),
        in_specs=[part(0), part(1), part(2), part(3),
                  pl.BlockSpec((1, dk), lambda b, h: (0, h)),
                  pl.BlockSpec((1, dk), lambda b, h: (0, 0))],
        out_specs=pl.BlockSpec((t, dk), lambda b, h: (b, h)),
        out_shape=jax.ShapeDtypeStruct((n, H * dk), jnp.float32),
        scratch_shapes=[slab, slab, slab, slab, pltpu.VMEM((nc, 1, dk), jnp.float32),
                        pltpu.VMEM((nc, dk, dk), jnp.bfloat16)],
        compiler_params=pltpu.CompilerParams(
            dimension_semantics=("arbitrary", "arbitrary"), vmem_limit_bytes=VMEM_LIMIT_BYTES),
        name="hgrn2",
    )(proj, proj, proj, proj, lb.reshape(1, H * dk), norm_w.reshape(1, dk))


GELU_C0 = float(np.sqrt(2.0 / np.pi))
GELU_C1 = GELU_C0 * 0.044715


def _gelu(x):
    half = 0.5 * x
    return half + half * jnp.tanh(x * (GELU_C0 + GELU_C1 * (x * x)))


def _head_norm(x, w):
    return x * lax.rsqrt(jnp.mean(x * x, axis=-1, keepdims=True) + NORM_EPS) * w


def _value_slab_t(v):
    lane = lax.broadcasted_iota(jnp.int32, v.shape, 1)
    return jnp.concatenate([v, jnp.where(lane == 0, 1.0, 0.0)], axis=-1).T.astype(jnp.bfloat16)


POS_BASE = 64


def _key_slab(k, pos):
    lane = lax.broadcasted_iota(jnp.int32, k.shape, 1)
    hi = (pos // POS_BASE).astype(jnp.float32)
    lo = (pos % POS_BASE).astype(jnp.float32)
    tail = jnp.where(lane < 2, hi, jnp.where(lane < 4, lo, 0.0))
    return jnp.concatenate([k, tail], axis=-1).astype(jnp.bfloat16)


def _kv_prep_body(c_ref, s_ref, w_ref, ksw_ref, kww_ref, kn_ref, vv_ref, cf_ref, *, seq):
    dh = NSA_HEAD_DIM
    tm = c_ref.shape[0]
    pos = (pl.program_id(0) * tm) % seq + lax.broadcasted_iota(jnp.int32, (tm, dh), 0)
    for h in range(NSA_KV_HEADS):
        k_cols = slice(h * dh, (h + 1) * dh)
        v_cols = slice(KV_W + h * dh, KV_W + (h + 1) * dh)
        cf_ref[0, h] = c_ref[:, k_cols]
        cf_ref[1, h] = c_ref[:, v_cols]
        kn_ref[0, h] = _key_slab(_head_norm(s_ref[:, k_cols], ksw_ref[...]), pos)
        vv_ref[0, h] = _value_slab_t(s_ref[:, v_cols])
        kn_ref[1, h] = _key_slab(_head_norm(w_ref[:, k_cols], kww_ref[...]), pos)
        vv_ref[1, h] = _value_slab_t(w_ref[:, v_cols])


def _kv_prep(proj, ks_w, kw_w, tm, seq):
    n = proj.shape[0]
    dh = NSA_HEAD_DIM
    pair = 2 * KV_W
    col0 = KV_COL0 // pair
    head_block = pl.BlockSpec((2, NSA_KV_HEADS, tm, dh), lambda i: (0, 0, i, 0))
    wide_block = pl.BlockSpec((2, NSA_KV_HEADS, tm, 2 * dh), lambda i: (0, 0, i, 0))
    w_spec = pl.BlockSpec((1, dh), lambda i: (0, 0))
    return pl.pallas_call(
        functools.partial(_kv_prep_body, seq=seq),
        grid=(n // tm,),
        in_specs=[pl.BlockSpec((tm, pair), lambda i: (i, col0)),
                  pl.BlockSpec((tm, pair), lambda i: (i, col0 + 1)),
                  pl.BlockSpec((tm, pair), lambda i: (i, col0 + 2)),
                  w_spec, w_spec],
        out_specs=[wide_block, pl.BlockSpec((2, NSA_KV_HEADS, 2 * dh, tm), lambda i: (0, 0, 0, i)), head_block],
        out_shape=[jax.ShapeDtypeStruct((2, NSA_KV_HEADS, n, 2 * dh), jnp.bfloat16),
                   jax.ShapeDtypeStruct((2, NSA_KV_HEADS, 2 * dh, n), jnp.bfloat16),
                   jax.ShapeDtypeStruct((2, NSA_KV_HEADS, n, dh), jnp.float32)],
        compiler_params=pltpu.CompilerParams(
            dimension_semantics=("arbitrary",), vmem_limit_bytes=VMEM_LIMIT_BYTES),
        name="kv_prep",
    )(proj, proj, proj, ks_w.reshape(1, dh), kw_w.reshape(1, dh))


def _cmp_kv_body(r_ref, pos_ref, w1_ref, w2_ref, nw_ref, o_ref):
    half = CMP_STRIDE * NSA_HEAD_DIM
    n_strips = r_ref.shape[2]
    for kind in range(2):
        strips = r_ref[kind, 0]
        top = (strips + pos_ref[kind, 0:1, :]).astype(jnp.bfloat16)
        bot = (strips + pos_ref[kind, 1:2, :]).astype(jnp.bfloat16)
        a = jnp.dot(top, w1_ref[kind, :half, :], preferred_element_type=jnp.float32)
        b = jnp.dot(bot, w1_ref[kind, half:, :], preferred_element_type=jnp.float32)
        hid = _gelu(a + pltpu.roll(b, n_strips - 1, 0))
        out = jnp.dot(hid.astype(jnp.bfloat16), w2_ref[kind], preferred_element_type=jnp.float32)
        if kind == 0:
            out = _head_norm(out, nw_ref[...])
        o_ref[kind, 0, 0] = out.astype(jnp.bfloat16)


def _cmp_kv(cf, pos, w1, w2, kc_w, batch):
    _, hkv, n, dh = cf.shape
    t = n // batch
    n_strips = t // CMP_STRIDE
    strips = cf.reshape(2, hkv, n // CMP_STRIDE, CMP_STRIDE * dh)
    return pl.pallas_call(
        _cmp_kv_body,
        grid=(batch, hkv),
        in_specs=[pl.BlockSpec((2, 1, n_strips, CMP_STRIDE * dh), lambda b, g: (0, g, b, 0)),
                  pl.BlockSpec(pos.shape, lambda b, g: (0, 0, 0)),
                  pl.BlockSpec(w1.shape, lambda b, g: (0, 0, 0)),
                  pl.BlockSpec(w2.shape, lambda b, g: (0, 0, 0)),
                  pl.BlockSpec((1, dh), lambda b, g: (0, 0))],
        out_specs=pl.BlockSpec((2, 1, 1, n_strips, dh), lambda b, g: (0, g, b, 0, 0)),
        out_shape=jax.ShapeDtypeStruct((2, hkv, batch, n_strips, dh), jnp.bfloat16),
        compiler_params=pltpu.CompilerParams(
            dimension_semantics=("arbitrary", "arbitrary"), vmem_limit_bytes=VMEM_LIMIT_BYTES),
        name="cmp_kv",
    )(strips, pos, w1, w2, kc_w.reshape(1, dh))


def _flash_step(qt_ref, k, vt, m_ref, acc_ref, base, mask_fn, tq):
    tk = k.shape[0]
    dist = base + lax.broadcasted_iota(jnp.int32, (tk, tq), 1) - lax.broadcasted_iota(jnp.int32, (tk, tq), 0)
    off = jnp.where(mask_fn(dist), 0.0, MASK_OFF)
    for r in range(NSA_GROUP):
        cols = slice(r * tq, (r + 1) * tq)
        s = jnp.dot(k, qt_ref[:, cols], preferred_element_type=jnp.float32) - off
        m_old = m_ref[:, cols]
        m_new = jnp.maximum(m_old, jnp.max(s, axis=0, keepdims=True))
        p = jnp.exp2(s - m_new).astype(jnp.bfloat16)
        acc_ref[:, cols] = jnp.exp2(m_old - m_new) * acc_ref[:, cols] \
            + jnp.dot(vt, p, preferred_element_type=jnp.float32)
        m_ref[:, cols] = m_new


def _nsa_attn_body(q_ref, gate_ref, kc_ref, vc_ref, ks_ref, vs_ref, kw_ref, vw_ref, qw_ref, slope_ref, o_ref,
                   qt_ref, sel_ref, oc_ref, ms_ref, as_ref, mw_ref, aw_ref, *, tq, tk, n_sel):
    i = pl.program_id(2)
    j = pl.program_id(3)
    G, dh = NSA_GROUP, NSA_HEAD_DIM
    last_j = (i * tq + tq - 1) // tk

    @pl.when(j == 0)
    def _():
        x = q_ref[...]
        for r in range(G):
            qn = _head_norm(x[:, r * dh:(r + 1) * dh], qw_ref[...]) * (dh ** -0.5 * LOG2E)
            slope = slope_ref[0, r]
            s_hi = slope.astype(jnp.bfloat16).astype(jnp.float32)
            s_lo = slope - s_hi
            lane = lax.broadcasted_iota(jnp.int32, (tq, dh), 1)
            tail = jnp.where(lane == 0, POS_BASE * s_hi, jnp.where(lane == 1, POS_BASE * s_lo,
                             jnp.where(lane == 2, s_hi, jnp.where(lane == 3, s_lo, 0.0))))
            qt_ref[:, r * tq:(r + 1) * tq] = jnp.concatenate([qn, tail], axis=-1).T.astype(jnp.bfloat16)
        for ref in (ms_ref, mw_ref):
            ref[...] = jnp.full(ref.shape, NEG_INF, jnp.float32)
        for ref in (as_ref, aw_ref):
            ref[...] = jnp.zeros(ref.shape, jnp.float32)

        n_cmp_pad = kc_ref.shape[-2]
        t_pos = i * tq + lax.broadcasted_iota(jnp.int32, (n_cmp_pad, tq), 1)
        blk = lax.broadcasted_iota(jnp.int32, (n_cmp_pad, tq), 0)
        dist = t_pos - (blk * CMP_STRIDE + CMP_BLOCK - 1)
        valid = (dist >= 0) & (blk < n_cmp_pad - 1)
        distf = dist.astype(jnp.float32)
        cb = lax.broadcasted_iota(jnp.int32, (n_sel, n_cmp_pad), 1) * CMP_STRIDE
        sb = lax.broadcasted_iota(jnp.int32, (n_sel, n_cmp_pad), 0) * SEL_BLOCK
        overlap_t = ((cb < sb + SEL_BLOCK) & (cb + CMP_BLOCK > sb)).astype(jnp.bfloat16)
        imp = None
        for r in range(G):
            cols = slice(r * tq, (r + 1) * tq)
            s = jnp.dot(kc_ref[0, 0, 0], qt_ref[:dh, cols], preferred_element_type=jnp.float32)
            s = jnp.where(valid, s - slope_ref[0, r] * distf, NEG_INF)
            p = jnp.where(valid, jnp.exp2(s - jnp.max(s, axis=0, keepdims=True)), 0.0)
            denom = jnp.maximum(jnp.sum(p, axis=0, keepdims=True), 1e-30)
            p = (p / denom).astype(jnp.bfloat16)
            oc_ref[:, cols] = lax.dot_general(vc_ref[0, 0, 0], p, (((0,), (0,)), ((), ())),
                                              preferred_element_type=jnp.float32)
            part = jnp.dot(overlap_t, p, preferred_element_type=jnp.float32)
            imp = part if imp is None else imp + part

        cur = (i * tq + lax.broadcasted_iota(jnp.int32, (n_sel, tq), 1)) // SEL_BLOCK
        jb = lax.broadcasted_iota(jnp.int32, (n_sel, tq), 0)
        forced = (jb == 0) | ((jb <= cur) & (jb > cur - N_LOCAL))
        imp = jnp.where(forced, FORCE_SCORE, jnp.where(jb > cur, -FORCE_SCORE, imp))
        rank = jnp.zeros((n_sel, tq), jnp.float32)
        for c in range(n_sel):
            row = imp[c:c + 1, :]
            ahead = (row > imp) | ((row == imp) & (jb > c))
            rank = rank + jnp.where(ahead, 1.0, 0.0)
        sel_t = jnp.where(rank < min(N_SELECT, n_sel), 1.0, 0.0)
        sel_t = jnp.concatenate([sel_t, jnp.zeros((sel_ref.shape[0] - n_sel, tq), jnp.float32)], axis=0)
        sel_ref[...] = sel_t.astype(jnp.bfloat16)

    @pl.when(j <= last_j)
    def _():
        n_pad = sel_ref.shape[0]
        key_blk = j * (tk // SEL_BLOCK) + lax.broadcasted_iota(jnp.int32, (tk, n_pad), 0) // SEL_BLOCK
        expand_t = (key_blk == lax.broadcasted_iota(jnp.int32, (tk, n_pad), 1)).astype(jnp.bfloat16)

        def mask(dist):
            picked = jnp.dot(expand_t, sel_ref[...], preferred_element_type=jnp.float32)
            return (dist >= 0) & (picked > 0.5)

        _flash_step(qt_ref, ks_ref[0, 0], vs_ref[0, 0], ms_ref, as_ref, i * tq - j * tk, mask, tq)

    @pl.when((j <= last_j) & (j * tk + tk - 1 >= i * tq - (WINDOW - 1)))
    def _():
        _flash_step(qt_ref, kw_ref[0, 0], vw_ref[0, 0], mw_ref, aw_ref, i * tq - j * tk,
                    lambda dist: (dist >= 0) & (dist < WINDOW), tq)

    @pl.when(j == pl.num_programs(3) - 1)
    def _():
        gates = jax.nn.sigmoid(gate_ref[0])
        outs = []
        for r in range(G):
            cols = slice(r * tq, (r + 1) * tq)
            o_sel = as_ref[:dh, cols] / as_ref[dh:dh + 1, cols]
            o_win = aw_ref[:dh, cols] / aw_ref[dh:dh + 1, cols]
            outs.append(gates[3 * r:3 * r + 1] * oc_ref[:, cols] + gates[3 * r + 1:3 * r + 2] * o_sel
                        + gates[3 * r + 2:3 * r + 3] * o_win)
        o_ref[...] = jnp.concatenate(outs, axis=0).T


def _nsa_attn(proj, gates, cmp_kv, kn, vv, q_w, slopes, batch, tq=512, tk=512):
    n = proj.shape[0]
    t = n // batch
    G, dh, hkv = NSA_GROUP, NSA_HEAD_DIM, NSA_KV_HEADS
    n_cmp_pad = cmp_kv.shape[-2]
    n_sel = t // SEL_BLOCK
    q_blk0 = Q_COL0 // (G * dh)

    def kv_tile(b, i, j):
        return b * (t // tk) + jnp.minimum(j, (i * tq + tq - 1) // tk)

    def k_map(kind):
        return lambda b, g, i, j: (kind, g, kv_tile(b, i, j), 0)

    def v_map(kind):
        return lambda b, g, i, j: (kind, g, 0, kv_tile(b, i, j))

    def cmp_map(kind):
        return lambda b, g, i, j: (kind, g, b, 0, 0)

    cmp_block = (1, 1, 1, n_cmp_pad, dh)
    k_block = (1, 1, tk, 2 * dh)
    v_block = (1, 1, 2 * dh, tk)
    slab = pltpu.VMEM((dh, G * tq), jnp.float32)
    wide = pltpu.VMEM((2 * dh, G * tq), jnp.float32)
    stat = pltpu.VMEM((1, G * tq), jnp.float32)
    return pl.pallas_call(
        functools.partial(_nsa_attn_body, tq=tq, tk=tk, n_sel=n_sel),
        grid=(batch, hkv, t // tq, t // tk),
        in_specs=[
            pl.BlockSpec((tq, G * dh), lambda b, g, i, j: (b * (t // tq) + i, q_blk0 + g)),
            pl.BlockSpec((1, 3 * G, tq), lambda b, g, i, j: (g, 0, b * (t // tq) + i)),
            pl.BlockSpec(cmp_block, cmp_map(0)), pl.BlockSpec(cmp_block, cmp_map(1)),
            pl.BlockSpec(k_block, k_map(0)), pl.BlockSpec(v_block, v_map(0)),
            pl.BlockSpec(k_block, k_map(1)), pl.BlockSpec(v_block, v_map(1)),
            pl.BlockSpec((1, dh), lambda b, g, i, j: (0, 0)),
            pl.BlockSpec((1, G, 1, 1), lambda b, g, i, j: (g, 0, 0, 0)),
        ],
        out_specs=pl.BlockSpec((tq, G * dh), lambda b, g, i, j: (b * (t // tq) + i, g)),
        out_shape=jax.ShapeDtypeStruct((n, hkv * G * dh), jnp.float32),
        scratch_shapes=[pltpu.VMEM((2 * dh, G * tq), jnp.bfloat16), pltpu.VMEM((LANES, tq), jnp.bfloat16),
                        slab, stat, wide, stat, wide],
        compiler_params=pltpu.CompilerParams(
            dimension_semantics=("arbitrary",) * 4, vmem_limit_bytes=VMEM_LIMIT_BYTES),
        name="nsa_attn",
    )(proj, gates, cmp_kv, cmp_kv, kn, vv, kn, vv, q_w.reshape(1, dh), slopes)


def _nsa(proj, batch, q_norm_w, kc_norm_w, ks_norm_w, kw_norm_w, pos_k, pos_v, w_ck1, w_ck2, w_cv1, w_cv2):
    n = proj.shape[0]
    G, hkv = NSA_GROUP, NSA_KV_HEADS
    kn, vv, cf = _kv_prep(proj, ks_norm_w, kw_norm_w, 512, n // batch)
    half = CMP_STRIDE * NSA_HEAD_DIM
    pos = jnp.stack([pos_k.reshape(2, half), pos_v.reshape(2, half)])
    w1 = jnp.stack([w_ck1, w_cv1]).astype(jnp.bfloat16)
    w2 = jnp.stack([w_ck2, w_cv2]).astype(jnp.bfloat16)
    cmp_kv = _cmp_kv(cf, pos, w1, w2, kc_norm_w, batch)
    gates = proj[:, GATE_COL0:GATE_COL0 + N_GATES].reshape(n, hkv, 3 * G).transpose(1, 2, 0)
    slopes = jnp.asarray(2.0 ** (-8.0 * np.arange(1, NSA_Q_HEADS + 1) / NSA_Q_HEADS), jnp.float32) * LOG2E
    return _nsa_attn(proj, gates, cmp_kv, kn, vv, q_norm_w, slopes.reshape(hkv, G, 1, 1), batch)


BIG_NEG = -3.0e38
PEER_CAND = [(a, b) for a in range(PEER_TOPK) for b in range(PEER_TOPK) if (a + 1) * (b + 1) <= PEER_TOPK]
PEER_CAND_ROWS = -(-len(PEER_CAND) // 8) * 8


def _peer_route_body(h_ref, g_ref, wq_ref, keys_ref, xt_ref, rk2_ref, a2_ref, cnt_ref, r_ref,
                     qt_ref, sc_ref, cur_ref, top_ref, cand_ref):
    x = h_ref[...]
    hn = x * lax.rsqrt(jnp.mean(x * x, axis=-1, keepdims=True) + NORM_EPS) * g_ref[...]
    hnt = hn.T.astype(jnp.bfloat16)
    xt_ref[...] = hnt
    qt_ref[...] = jnp.dot(wq_ref[...], hnt, preferred_element_type=jnp.float32).astype(jnp.bfloat16)
    cand_ref[...] = jnp.full(cand_ref.shape, BIG_NEG, jnp.float32)
    kd = PEER_KEY_DIM // 2
    n_parts = 2 * PEER_HEADS
    for hp in range(n_parts):
        s = jnp.dot(keys_ref[hp], qt_ref[hp * kd:(hp + 1) * kd, :], preferred_element_type=jnp.float32)
        sc_ref[hp] = s
        cur_ref[hp] = s

    def extract(k, carry):
        cur = cur_ref[...]
        mk = jnp.max(cur, axis=1, keepdims=True)
        top_ref[k] = mk
        cur_ref[...] = jnp.where(cur == mk, BIG_NEG, cur)
        return carry

    lax.fori_loop(0, PEER_TOPK, extract, 0)

    def head(hd, carry):
        t1 = [top_ref[a, 2 * hd] for a in range(PEER_TOPK)]
        t2 = [top_ref[b, 2 * hd + 1] for b in range(PEER_TOPK)]
        cmax = t1[0] + t2[0]
        cands = [t1[a] + t2[b] for a, b in PEER_CAND]
        for i, c in enumerate(cands):
            cand_ref[i:i + 1, :] = c
        call = cand_ref[...]
        n_gt = jnp.zeros(call.shape, jnp.float32)
        for c in cands:
            n_gt = n_gt + jnp.where(c > call, 1.0, 0.0)
        tau = jnp.min(jnp.where(n_gt <= PEER_TOPK - 1, call, -BIG_NEG), axis=0, keepdims=True)
        z = jnp.sum(jnp.where(call >= tau, jnp.exp(call - cmax), 0.0), axis=0, keepdims=True)
        count = [None] * PEER_TOPK
        for (a, b), c in zip(PEER_CAND, cands):
            hit = jnp.where(c >= tau, 1.0, 0.0)
            count[a] = hit if count[a] is None else count[a] + hit
        s1 = sc_ref[2 * hd]
        s2 = sc_ref[2 * hd + 1]
        cnt = jnp.zeros(s1.shape, jnp.float32)
        rank2 = jnp.zeros(s2.shape, jnp.float32)
        for a in range(PEER_TOPK):
            cnt = jnp.where(s1 == t1[a], count[a], cnt)
            rank2 = rank2 + jnp.where(t2[a] > s2, 1.0, 0.0)
        rk2_ref[hd] = rank2.astype(jnp.bfloat16)
        a2_ref[hd] = jnp.exp(s2 - t2[0]).astype(jnp.bfloat16)
        cnt_ref[hd] = cnt
        r_ref[hd] = jnp.exp(s1 - t1[0]) / z
        return carry

    lax.fori_loop(0, PEER_HEADS, head, 0)


def _peer_route(h, gain, wq_t, keys, tt):
    n, d = h.shape
    hp, nk, kd = keys.shape
    stat = jax.ShapeDtypeStruct((PEER_HEADS, nk, n), jnp.float32)
    stat_bf16 = jax.ShapeDtypeStruct((PEER_HEADS, nk, n), jnp.bfloat16)
    stat_spec = pl.BlockSpec((PEER_HEADS, nk, tt), lambda i: (0, 0, i))
    return pl.pallas_call(
        _peer_route_body,
        grid=(n // tt,),
        in_specs=[
            pl.BlockSpec((tt, d), lambda i: (i, 0)),
            pl.BlockSpec((1, d), lambda i: (0, 0)),
            pl.BlockSpec(wq_t.shape, lambda i: (0, 0)),
            pl.BlockSpec(keys.shape, lambda i: (0, 0, 0)),
        ],
        out_specs=[pl.BlockSpec((d, tt), lambda i: (0, i)), stat_spec, stat_spec, stat_spec, stat_spec],
        out_shape=[jax.ShapeDtypeStruct((d, n), jnp.bfloat16), stat_bf16, stat_bf16, stat, stat],
        scratch_shapes=[pltpu.VMEM((wq_t.shape[0], tt), jnp.bfloat16),
                        pltpu.VMEM((hp, nk, tt), jnp.float32), pltpu.VMEM((hp, nk, tt), jnp.float32),
                        pltpu.VMEM((PEER_TOPK, hp, 1, tt), jnp.float32),
                        pltpu.VMEM((PEER_CAND_ROWS, tt), jnp.float32)],
        compiler_params=pltpu.CompilerParams(
            dimension_semantics=("arbitrary",), vmem_limit_bytes=VMEM_LIMIT_BYTES),
        name="peer_route",
    )(h, gain.reshape(1, d), wq_t, keys)


def _peer_experts_body(xt_ref, u_ref, vt_ref, rk2_ref, a2_ref, cnt_ref, r_ref, o_ref, *, n_i1):
    @pl.when(pl.program_id(1) == 0)
    def _():
        o_ref[...] = jnp.zeros(o_ref.shape, jnp.float32)

    act = _gelu(jnp.dot(u_ref[...], xt_ref[...], preferred_element_type=jnp.float32))
    parts = []
    for i1 in range(n_i1):
        w = None
        for hd in range(PEER_HEADS):
            picked = rk2_ref[hd] < cnt_ref[hd, i1:i1 + 1, :].astype(jnp.bfloat16)
            term = jnp.where(picked, a2_ref[hd], 0.0) * r_ref[hd, i1:i1 + 1, :].astype(jnp.bfloat16)
            w = term if w is None else w + term
        parts.append(w * act[i1 * N_KEYS:(i1 + 1) * N_KEYS].astype(jnp.bfloat16))
    p = jnp.concatenate(parts, axis=0)
    o_ref[...] += jnp.dot(vt_ref[...], p, preferred_element_type=jnp.float32)


def _peer_experts(xt, u, vt, rk2, a2, cnt, r, tt, te):
    d, n = xt.shape
    e = u.shape[0]
    n_i1 = te // N_KEYS
    full = pl.BlockSpec((PEER_HEADS, N_KEYS, tt), lambda i, j: (0, 0, i))
    part = pl.BlockSpec((PEER_HEADS, n_i1, tt), lambda i, j: (0, j, i))
    return pl.pallas_call(
        functools.partial(_peer_experts_body, n_i1=n_i1),
        grid=(n // tt, e // te),
        in_specs=[
            pl.BlockSpec((d, tt), lambda i, j: (0, i)),
            pl.BlockSpec((te, d), lambda i, j: (j, 0)),
            pl.BlockSpec((d, te), lambda i, j: (0, j)),
            full, full, part, part,
        ],
        out_specs=pl.BlockSpec((d, tt), lambda i, j: (0, i)),
        out_shape=jax.ShapeDtypeStruct((d, n), jnp.float32),
        compiler_params=pltpu.CompilerParams(
            dimension_semantics=("arbitrary", "arbitrary"), vmem_limit_bytes=PEER_VMEM_LIMIT_BYTES),
        name="peer_experts",
    )(xt, u, vt, rk2, a2, cnt, r)


def _transpose_cast_body(x_ref, o_ref):
    o_ref[...] = x_ref[...].T.astype(o_ref.dtype)


def _transpose_cast(x, blk, dtype):
    r, c = x.shape
    return pl.pallas_call(
        _transpose_cast_body,
        grid=(r // blk, c // blk),
        in_specs=[pl.BlockSpec((blk, blk), lambda i, j: (i, j))],
        out_specs=pl.BlockSpec((blk, blk), lambda i, j: (j, i)),
        out_shape=jax.ShapeDtypeStruct((c, r), dtype),
        compiler_params=pltpu.CompilerParams(
            dimension_semantics=("arbitrary", "arbitrary"), vmem_limit_bytes=VMEM_LIMIT_BYTES),
        name="transpose_cast",
    )(x)


def _transpose_add_body(x_ref, r_ref, o_ref):
    o_ref[...] = r_ref[...] + x_ref[...].T


def _transpose_add(x_t, res, blk):
    d, n = x_t.shape
    return pl.pallas_call(
        _transpose_add_body,
        grid=(d // blk, n // blk),
        in_specs=[pl.BlockSpec((blk, blk), lambda i, j: (i, j)),
                  pl.BlockSpec((blk, blk), lambda i, j: (j, i))],
        out_specs=pl.BlockSpec((blk, blk), lambda i, j: (j, i)),
        out_shape=jax.ShapeDtypeStruct((n, d), jnp.float32),
        compiler_params=pltpu.CompilerParams(
            dimension_semantics=("arbitrary", "arbitrary"), vmem_limit_bytes=VMEM_LIMIT_BYTES),
        name="transpose_add",
    )(x_t, res)


def _peer(h, gain, w_q, sub_keys, u_tab, v_tab):
    keys = sub_keys.reshape(PEER_HEADS * 2, N_KEYS, PEER_KEY_DIM // 2).astype(jnp.bfloat16)
    xt, rk2, a2, cnt, r = _peer_route(h, gain, _transpose_cast(w_q, 1024, jnp.bfloat16), keys, 256)
    out_t = _peer_experts(xt, u_tab.astype(jnp.bfloat16), _transpose_cast(v_tab, 1024, jnp.bfloat16),
                          rk2, a2, cnt, r, 512, 2048)
    return _transpose_add(out_t, h, min(1024, h.shape[0]))


def kernel(x, norm1_w, w_in, hg_lb_logits, hg_norm_w, q_norm_w, kc_norm_w, ks_norm_w, kw_norm_w,
           cmp_pos_k, cmp_pos_v, w_ck1, w_ck2, w_cv1, w_cv2, w_out, norm2_w,
           peer_w_q, peer_sub_keys, peer_u, peer_v):
    B, T, D = x.shape
    n = B * T
    layer = 0
    lower_bounds = jnp.cumsum(jax.nn.softmax(hg_lb_logits, axis=0), axis=0)
    xt = x.reshape(n, D)

    w_in_b = jnp.pad(w_in[layer].astype(jnp.bfloat16), ((0, 0), (0, IN_COLS_PADDED - IN_COLS)))
    proj = _norm_matmul(xt, norm1_w[layer], w_in_b, 1024, 1024)
    hg_out = _hgrn2(proj, lower_bounds[layer], hg_norm_w[layer], B)
    nsa_out = _nsa(proj, B, q_norm_w[layer], kc_norm_w[layer], ks_norm_w[layer], kw_norm_w[layer],
                   cmp_pos_k[layer], cmp_pos_v[layer], w_ck1[layer], w_ck2[layer], w_cv1[layer], w_cv2[layer])
    h = _out_proj(hg_out, nsa_out, w_out[layer].astype(jnp.bfloat16), xt, 1024, 1024)

    y = _peer(h, norm2_w[layer], peer_w_q[layer], peer_sub_keys[layer], peer_u[layer], peer_v[layer])
    return y.reshape(B, T, D)
```

```python
import functools

import jax
import jax.numpy as jnp
import numpy as np
from jax import lax
from jax.experimental import pallas as pl
from jax.experimental.pallas import tpu as pltpu

D_MODEL = 2048
HG_WIDTH = 1024
HG_HEAD_DIM = 128
HG_HEADS = 8
HG_CHUNK = 64
NSA_WIDTH = 1024
NSA_HEAD_DIM = 64
NSA_Q_HEADS = 16
NSA_KV_HEADS = 4
NSA_GROUP = 4
CMP_BLOCK = 32
CMP_STRIDE = 16
CMP_HIDDEN = 256
SEL_BLOCK = 64
N_SELECT = 16
N_LOCAL = 2
WINDOW = 512
FORCE_SCORE = 1e9
NEG_INF = -1e30
MASK_OFF = 3e30
LOG2E = float(np.log2(np.e))
PEER_HEADS = 8
N_KEYS = 128
PEER_KEY_DIM = 256
PEER_TOPK = 16
NORM_EPS = 1e-6
KV_W = NSA_KV_HEADS * NSA_HEAD_DIM
N_GATES = 3 * NSA_Q_HEADS
IN_SIZES = [HG_WIDTH] * 4 + [NSA_WIDTH] + [KV_W] * 6 + [N_GATES]
IN_COLS = sum(IN_SIZES)
IN_COLS_PADDED = 7168
Q_COL0 = 4 * HG_WIDTH
KV_COL0 = Q_COL0 + NSA_WIDTH
GATE_COL0 = KV_COL0 + 6 * KV_W

VMEM_LIMIT_BYTES = 48 * 1024 * 1024
LANES = 128
PEER_VMEM_LIMIT_BYTES = 60 * 1024 * 1024


def _norm_matmul_body(x_ref, g_ref, w_ref, o_ref, xn_ref):
    @pl.when(pl.program_id(1) == 0)
    def _():
        x = x_ref[...]
        r = lax.rsqrt(jnp.mean(x * x, axis=-1, keepdims=True) + NORM_EPS)
        xn_ref[...] = (x * r * g_ref[...]).astype(jnp.bfloat16)

    o_ref[...] = jnp.dot(xn_ref[...], w_ref[...], preferred_element_type=jnp.float32).astype(o_ref.dtype)


def _norm_matmul(x, gain, w, tm, tn, out_dtype=jnp.float32):
    m, k = x.shape
    n = w.shape[1]
    return pl.pallas_call(
        _norm_matmul_body,
        grid=(m // tm, n // tn),
        in_specs=[
            pl.BlockSpec((tm, k), lambda i, j: (i, 0)),
            pl.BlockSpec((1, k), lambda i, j: (0, 0)),
            pl.BlockSpec((k, tn), lambda i, j: (0, j)),
        ],
        out_specs=pl.BlockSpec((tm, tn), lambda i, j: (i, j)),
        out_shape=jax.ShapeDtypeStruct((m, n), out_dtype),
        scratch_shapes=[pltpu.VMEM((tm, k), jnp.bfloat16)],
        compiler_params=pltpu.CompilerParams(
            dimension_semantics=("arbitrary", "arbitrary"), vmem_limit_bytes=VMEM_LIMIT_BYTES),
        name="norm_matmul",
    )(x, gain.reshape(1, k), w)


def _out_proj_body(a1_ref, a2_ref, w1_ref, w2_ref, r_ref, o_ref):
    acc = jnp.dot(a1_ref[...].astype(jnp.bfloat16), w1_ref[...], preferred_element_type=jnp.float32)
    acc = acc + jnp.dot(a2_ref[...].astype(jnp.bfloat16), w2_ref[...], preferred_element_type=jnp.float32)
    o_ref[...] = r_ref[...] + acc


def _out_proj(a1, a2, w, res, tm, tn):
    m, k1 = a1.shape
    k2 = a2.shape[1]
    n = w.shape[1]
    return pl.pallas_call(
        _out_proj_body,
        grid=(m // tm, n // tn),
        in_specs=[
            pl.BlockSpec((tm, k1), lambda i, j: (i, 0)),
            pl.BlockSpec((tm, k2), lambda i, j: (i, 0)),
            pl.BlockSpec((k1, tn), lambda i, j: (0, j)),
            pl.BlockSpec((k2, tn), lambda i, j: (k1 // k2, j)),
            pl.BlockSpec((tm, tn), lambda i, j: (i, j)),
        ],
        out_specs=pl.BlockSpec((tm, tn), lambda i, j: (i, j)),
        out_shape=jax.ShapeDtypeStruct((m, n), jnp.float32),
        compiler_params=pltpu.CompilerParams(
            dimension_semantics=("arbitrary", "arbitrary"), vmem_limit_bytes=VMEM_LIMIT_BYTES),
        name="out_proj",
    )(a1, a2, w, w, res)


def _hgrn2_body(q_ref, f_ref, v_ref, g_ref, lb_ref, nw_ref, o_ref, qd_ref, kd_ref, ku_ref, vb_ref, dec_ref, sp_ref):
    t, dk = q_ref.shape
    C = HG_CHUNK
    nc = t // C
    lb = lb_ref[...]
    f = lb + (1.0 - lb) * jax.nn.sigmoid(f_ref[...])
    kf = 1.0 - f
    b = jnp.log(f)
    row = lax.broadcasted_iota(jnp.int32, (t, dk), 0) % C
    shift = 1
    while shift < C:
        b = b + jnp.where(row >= shift, pltpu.roll(b, shift, 0), 0.0)
        shift *= 2
    b3 = b.reshape(nc, C, dk)
    b_end = b3[:, C - 1:C, :]
    qd_ref[...] = (q_ref[...] * jnp.exp(b)).astype(jnp.bfloat16)
    kd_ref[...] = (kf * jnp.exp(-b)).astype(jnp.bfloat16)
    ku_ref[...] = (kf.reshape(nc, C, dk) * jnp.exp(b_end - b3)).reshape(t, dk).astype(jnp.bfloat16)
    vb_ref[...] = v_ref[...].astype(jnp.bfloat16)
    dec_ref[...] = jnp.exp(b_end)

    st = jnp.zeros((dk, dk), jnp.float32)
    for n in range(nc):
        rows = slice(n * C, (n + 1) * C)
        sp_ref[n] = st.astype(jnp.bfloat16)
        upd_t = lax.dot_general(vb_ref[rows, :], ku_ref[rows, :], (((0,), (0,)), ((), ())),
                                preferred_element_type=jnp.float32)
        st = dec_ref[n] * st + upd_t

    causal = lax.broadcasted_iota(jnp.int32, (C, C), 0) >= lax.broadcasted_iota(jnp.int32, (C, C), 1)
    for n in range(nc):
        rows = slice(n * C, (n + 1) * C)
        qd = qd_ref[rows, :]
        attn = lax.dot_general(qd, kd_ref[rows, :], (((1,), (1,)), ((), ())), preferred_element_type=jnp.float32)
        attn = jnp.where(causal, attn, 0.0).astype(jnp.bfloat16)
        o = jnp.dot(attn, vb_ref[rows, :], preferred_element_type=jnp.float32)
        o = o + lax.dot_general(qd, sp_ref[n], (((1,), (1,)), ((), ())), preferred_element_type=jnp.float32)
        o = o * lax.rsqrt(jnp.mean(o * o, axis=-1, keepdims=True) + NORM_EPS) * nw_ref[...]
        o_ref[rows, :] = o * jax.nn.silu(g_ref[rows, :])


def _hgrn2(proj, lb, norm_w, batch):
    n = proj.shape[0]
    t = n // batch
    dk, H = HG_HEAD_DIM, HG_HEADS
    nc = t // HG_CHUNK

    def part(p):
        return pl.BlockSpec((t, dk), lambda b, h: (b, p * H + h))

    slab = pltpu.VMEM((t, dk), jnp.bfloat16)
    return pl.pallas_call(
        _hgrn2_body,
        grid=(batch, H),
        in_specs=[part(0), part(1), part(2), part(3),
                  pl.BlockSpec((1, dk), lambda b, h: (0, h)),
                  pl.BlockSpec((1, dk), lambda b, h: (0, 0))],
        out_specs=pl.BlockSpec((t, dk), lambda b, h: (b, h)),
        out_shape=jax.ShapeDtypeStruct((n, H * dk), jnp.float32),
        scratch_shapes=[slab, slab, slab, slab, pltpu.VMEM((nc, 1, dk), jnp.float32),
                        pltpu.VMEM((nc, dk, dk), jnp.bfloat16)],
        compiler_params=pltpu.CompilerParams(
            dimension_semantics=("arbitrary", "arbitrary"), vmem_limit_bytes=VMEM_LIMIT_BYTES),
        name="hgrn2",
    )(proj, proj, proj, proj, lb.reshape(1, H * dk), norm_w.reshape(1, dk))


GELU_C0 = float(np.sqrt(2.0 / np.pi))
GELU_C1 = GELU_C0 * 0.044715


def _gelu(x):
    half = 0.5 * x
    return half + half * jnp.tanh(x * (GELU_C0 + GELU_C1 * (x * x)))


def _head_norm(x, w):
    return x * lax.rsqrt(jnp.mean(x * x, axis=-1, keepdims=True) + NORM_EPS) * w


def _value_slab_t(v):
    lane = lax.broadcasted_iota(jnp.int32, v.shape, 1)
    return jnp.concatenate([v, jnp.where(lane == 0, 1.0, 0.0)], axis=-1).T.astype(jnp.bfloat16)


POS_BASE = 64


def _key_slab(k, pos):
    lane = lax.broadcasted_iota(jnp.int32, k.shape, 1)
    hi = (pos // POS_BASE).astype(jnp.float32)
    lo = (pos % POS_BASE).astype(jnp.float32)
    tail = jnp.where(lane < 2, hi, jnp.where(lane < 4, lo, 0.0))
    return jnp.concatenate([k, tail], axis=-1).astype(jnp.bfloat16)


def _kv_prep_body(c_ref, s_ref, w_ref, ksw_ref, kww_ref, kn_ref, vv_ref, cf_ref, *, seq):
    dh = NSA_HEAD_DIM
    tm = c_ref.shape[0]
    pos = (pl.program_id(0) * tm) % seq + lax.broadcasted_iota(jnp.int32, (tm, dh), 0)
    for h in range(NSA_KV_HEADS):
        k_cols = slice(h * dh, (h + 1) * dh)
        v_cols = slice(KV_W + h * dh, KV_W + (h + 1) * dh)
        cf_ref[0, h] = c_ref[:, k_cols]
        cf_ref[1, h] = c_ref[:, v_cols]
        kn_ref[0, h] = _key_slab(_head_norm(s_ref[:, k_cols], ksw_ref[...]), pos)
        vv_ref[0, h] = _value_slab_t(s_ref[:, v_cols])
        kn_ref[1, h] = _key_slab(_head_norm(w_ref[:, k_cols], kww_ref[...]), pos)
        vv_ref[1, h] = _value_slab_t(w_ref[:, v_cols])


def _kv_prep(proj, ks_w, kw_w, tm, seq):
    n = proj.shape[0]
    dh = NSA_HEAD_DIM
    pair = 2 * KV_W
    col0 = KV_COL0 // pair
    head_block = pl.BlockSpec((2, NSA_KV_HEADS, tm, dh), lambda i: (0, 0, i, 0))
    wide_block = pl.BlockSpec((2, NSA_KV_HEADS, tm, 2 * dh), lambda i: (0, 0, i, 0))
    w_spec = pl.BlockSpec((1, dh), lambda i: (0, 0))
    return pl.pallas_call(
        functools.partial(_kv_prep_body, seq=seq),
        grid=(n // tm,),
        in_specs=[pl.BlockSpec((tm, pair), lambda i: (i, col0)),
                  pl.BlockSpec((tm, pair), lambda i: (i, col0 + 1)),
                  pl.BlockSpec((tm, pair), lambda i: (i, col0 + 2)),
                  w_spec, w_spec],
        out_specs=[wide_block, pl.BlockSpec((2, NSA_KV_HEADS, 2 * dh, tm), lambda i: (0, 0, 0, i)), head_block],
        out_shape=[jax.ShapeDtypeStruct((2, NSA_KV_HEADS, n, 2 * dh), jnp.bfloat16),
                   jax.ShapeDtypeStruct((2, NSA_KV_HEADS, 2 * dh, n), jnp.bfloat16),
                   jax.ShapeDtypeStruct((2, NSA_KV_HEADS, n, dh), jnp.float32)],
        compiler_params=pltpu.CompilerParams(
            dimension_semantics=("arbitrary",), vmem_limit_bytes=VMEM_LIMIT_BYTES),
        name="kv_prep",
    )(proj, proj, proj, ks_w.reshape(1, dh), kw_w.reshape(1, dh))


def _cmp_kv_body(r_ref, pos_ref, w1_ref, w2_ref, nw_ref, o_ref):
    half = CMP_STRIDE * NSA_HEAD_DIM
    n_strips = r_ref.shape[2]
    for kind in range(2):
        strips = r_ref[kind, 0]
        top = (strips + pos_ref[kind, 0:1, :]).astype(jnp.bfloat16)
        bot = (strips + pos_ref[kind, 1:2, :]).astype(jnp.bfloat16)
        a = jnp.dot(top, w1_ref[kind, :half, :], preferred_element_type=jnp.float32)
        b = jnp.dot(bot, w1_ref[kind, half:, :], preferred_element_type=jnp.float32)
        hid = _gelu(a + pltpu.roll(b, n_strips - 1, 0))
        out = jnp.dot(hid.astype(jnp.bfloat16), w2_ref[kind], preferred_element_type=jnp.float32)
        if kind == 0:
            out = _head_norm(out, nw_ref[...])
        o_ref[kind, 0, 0] = out.astype(jnp.bfloat16)


def _cmp_kv(cf, pos, w1, w2, kc_w, batch):
    _, hkv, n, dh = cf.shape
    t = n // batch
    n_strips = t // CMP_STRIDE
    strips = cf.reshape(2, hkv, n // CMP_STRIDE, CMP_STRIDE * dh)
    return pl.pallas_call(
        _cmp_kv_body,
        grid=(batch, hkv),
        in_specs=[pl.BlockSpec((2, 1, n_strips, CMP_STRIDE * dh), lambda b, g: (0, g, b, 0)),
                  pl.BlockSpec(pos.shape, lambda b, g: (0, 0, 0)),
                  pl.BlockSpec(w1.shape, lambda b, g: (0, 0, 0)),
                  pl.BlockSpec(w2.shape, lambda b, g: (0, 0, 0)),
                  pl.BlockSpec((1, dh), lambda b, g: (0, 0))],
        out_specs=pl.BlockSpec((2, 1, 1, n_strips, dh), lambda b, g: (0, g, b, 0, 0)),
        out_shape=jax.ShapeDtypeStruct((2, hkv, batch, n_strips, dh), jnp.bfloat16),
        compiler_params=pltpu.CompilerParams(
            dimension_semantics=("arbitrary", "arbitrary"), vmem_limit_bytes=VMEM_LIMIT_BYTES),
        name="cmp_kv",
    )(strips, pos, w1, w2, kc_w.reshape(1, dh))


def _flash_step(qt_ref, k, vt, m_ref, acc_ref, base, mask_fn, tq):
    tk = k.shape[0]
    dist = base + lax.broadcasted_iota(jnp.int32, (tk, tq), 1) - lax.broadcasted_iota(jnp.int32, (tk, tq), 0)
    off = jnp.where(mask_fn(dist), 0.0, MASK_OFF)
    for r in range(NSA_GROUP):
        cols = slice(r * tq, (r + 1) * tq)
        s = jnp.dot(k, qt_ref[:, cols], preferred_element_type=jnp.float32) - off
        m_old = m_ref[:, cols]
        m_new = jnp.maximum(m_old, jnp.max(s, axis=0, keepdims=True))
        p = jnp.exp2(s - m_new).astype(jnp.bfloat16)
        acc_ref[:, cols] = jnp.exp2(m_old - m_new) * acc_ref[:, cols] \
            + jnp.dot(vt, p, preferred_element_type=jnp.float32)
        m_ref[:, cols] = m_new


def _nsa_attn_body(q_ref, gate_ref, kc_ref, vc_ref, ks_ref, vs_ref, kw_ref, vw_ref, qw_ref, slope_ref, o_ref,
                   qt_ref, sel_ref, oc_ref, ms_ref, as_ref, mw_ref, aw_ref, *, tq, tk, n_sel):
    i = pl.program_id(2)
    j = pl.program_id(3)
    G, dh = NSA_GROUP, NSA_HEAD_DIM
    last_j = (i * tq + tq - 1) // tk

    @pl.when(j == 0)
    def _():
        xt = q_ref[...].T
        row = lax.broadcasted_iota(jnp.int32, (dh, tq), 0)
        for r in range(G):
            xr = xt[r * dh:(r + 1) * dh, :]
            inv = lax.rsqrt(jnp.mean(xr * xr, axis=0, keepdims=True) + NORM_EPS)
            qn = xr * inv * qw_ref[...] * (dh ** -0.5 * LOG2E)
            slope = slope_ref[0, r]
            s_hi = slope.astype(jnp.bfloat16).astype(jnp.float32)
            s_lo = slope - s_hi
            tail = jnp.where(row == 0, POS_BASE * s_hi, jnp.where(row == 1, POS_BASE * s_lo,
                             jnp.where(row == 2, s_hi, jnp.where(row == 3, s_lo, 0.0))))
            qt_ref[:, r * tq:(r + 1) * tq] = jnp.concatenate([qn, tail], axis=0).astype(jnp.bfloat16)
        for ref in (ms_ref, mw_ref):
            ref[...] = jnp.full(ref.shape, NEG_INF, jnp.float32)
        for ref in (as_ref, aw_ref):
            ref[...] = jnp.zeros(ref.shape, jnp.float32)

        n_cmp_pad = kc_ref.shape[-2]
        t_pos = i * tq + lax.broadcasted_iota(jnp.int32, (n_cmp_pad, tq), 1)
        blk = lax.broadcasted_iota(jnp.int32, (n_cmp_pad, tq), 0)
        dist = t_pos - (blk * CMP_STRIDE + CMP_BLOCK - 1)
        valid = (dist >= 0) & (blk < n_cmp_pad - 1)
        distf = dist.astype(jnp.float32)
        cb = lax.broadcasted_iota(jnp.int32, (n_sel, n_cmp_pad), 1) * CMP_STRIDE
        sb = lax.broadcasted_iota(jnp.int32, (n_sel, n_cmp_pad), 0) * SEL_BLOCK
        overlap_t = ((cb < sb + SEL_BLOCK) & (cb + CMP_BLOCK > sb)).astype(jnp.bfloat16)
        imp = None
        for r in range(G):
            cols = slice(r * tq, (r + 1) * tq)
            s = jnp.dot(kc_ref[0, 0, 0], qt_ref[:dh, cols], preferred_element_type=jnp.float32)
            s = jnp.where(valid, s - slope_ref[0, r] * distf, NEG_INF)
            p = jnp.where(valid, jnp.exp2(s - jnp.max(s, axis=0, keepdims=True)), 0.0)
            denom = jnp.maximum(jnp.sum(p, axis=0, keepdims=True), 1e-30)
            p = (p / denom).astype(jnp.bfloat16)
            oc_ref[:, cols] = lax.dot_general(vc_ref[0, 0, 0], p, (((0,), (0,)), ((), ())),
                                              preferred_element_type=jnp.float32)
            part = jnp.dot(overlap_t, p, preferred_element_type=jnp.float32)
            imp = part if imp is None else imp + part

        cur = (i * tq + lax.broadcasted_iota(jnp.int32, (n_sel, tq), 1)) // SEL_BLOCK
        jb = lax.broadcasted_iota(jnp.int32, (n_sel, tq), 0)
        forced = (jb == 0) | ((jb <= cur) & (jb > cur - N_LOCAL))
        imp = jnp.where(forced, FORCE_SCORE, jnp.where(jb > cur, -FORCE_SCORE, imp))
        rank = jnp.zeros((n_sel, tq), jnp.float32)
        for c in range(n_sel):
            row = imp[c:c + 1, :]
            ahead = (row > imp) | ((row == imp) & (jb > c))
            rank = rank + jnp.where(ahead, 1.0, 0.0)
        sel_t = jnp.where(rank < min(N_SELECT, n_sel), 1.0, 0.0)
        sel_t = jnp.concatenate([sel_t, jnp.zeros((sel_ref.shape[0] - n_sel, tq), jnp.float32)], axis=0)
        sel_ref[...] = sel_t

    @pl.when(j <= last_j)
    def _():
        per_tile = tk // SEL_BLOCK

        def mask(dist):
            picked = jnp.concatenate(
                [jnp.broadcast_to(sel_ref[pl.ds(j * per_tile + b, 1), :], (SEL_BLOCK, tq)) for b in range(per_tile)],
                axis=0)
            return (dist >= 0) & (picked > 0.5)

        _flash_step(qt_ref, ks_ref[0, 0], vs_ref[0, 0], ms_ref, as_ref, i * tq - j * tk, mask, tq)

    @pl.when((j <= last_j) & (j * tk + tk - 1 >= i * tq - (WINDOW - 1)))
    def _():
        _flash_step(qt_ref, kw_ref[0, 0], vw_ref[0, 0], mw_ref, aw_ref, i * tq - j * tk,
                    lambda dist: (dist >= 0) & (dist < WINDOW), tq)

    @pl.when(j == pl.num_programs(3) - 1)
    def _():
        gates = jax.nn.sigmoid(gate_ref[0])
        outs = []
        for r in range(G):
            cols = slice(r * tq, (r + 1) * tq)
            o_sel = as_ref[:dh, cols] / as_ref[dh:dh + 1, cols]
            o_win = aw_ref[:dh, cols] / aw_ref[dh:dh + 1, cols]
            outs.append(gates[3 * r:3 * r + 1] * oc_ref[:, cols] + gates[3 * r + 1:3 * r + 2] * o_sel
                        + gates[3 * r + 2:3 * r + 3] * o_win)
        o_ref[...] = jnp.concatenate(outs, axis=0).T


def _nsa_attn(proj, gates, cmp_kv, kn, vv, q_w, slopes, batch, tq=512, tk=512):
    n = proj.shape[0]
    t = n // batch
    G, dh, hkv = NSA_GROUP, NSA_HEAD_DIM, NSA_KV_HEADS
    n_cmp_pad = cmp_kv.shape[-2]
    n_sel = t // SEL_BLOCK
    q_blk0 = Q_COL0 // (G * dh)

    def kv_tile(b, i, j):
        return b * (t // tk) + jnp.minimum(j, (i * tq + tq - 1) // tk)

    def k_map(kind):
        return lambda b, g, i, j: (kind, g, kv_tile(b, i, j), 0)

    def v_map(kind):
        return lambda b, g, i, j: (kind, g, 0, kv_tile(b, i, j))

    def cmp_map(kind):
        return lambda b, g, i, j: (kind, g, b, 0, 0)

    cmp_block = (1, 1, 1, n_cmp_pad, dh)
    k_block = (1, 1, tk, 2 * dh)
    v_block = (1, 1, 2 * dh, tk)
    slab = pltpu.VMEM((dh, G * tq), jnp.float32)
    wide = pltpu.VMEM((2 * dh, G * tq), jnp.float32)
    stat = pltpu.VMEM((1, G * tq), jnp.float32)
    return pl.pallas_call(
        functools.partial(_nsa_attn_body, tq=tq, tk=tk, n_sel=n_sel),
        grid=(batch, hkv, t // tq, t // tk),
        in_specs=[
            pl.BlockSpec((tq, G * dh), lambda b, g, i, j: (b * (t // tq) + i, q_blk0 + g)),
            pl.BlockSpec((1, 3 * G, tq), lambda b, g, i, j: (g, 0, b * (t // tq) + i)),
            pl.BlockSpec(cmp_block, cmp_map(0)), pl.BlockSpec(cmp_block, cmp_map(1)),
            pl.BlockSpec(k_block, k_map(0)), pl.BlockSpec(v_block, v_map(0)),
            pl.BlockSpec(k_block, k_map(1)), pl.BlockSpec(v_block, v_map(1)),
            pl.BlockSpec((dh, 1), lambda b, g, i, j: (0, 0)),
            pl.BlockSpec((1, G, 1, 1), lambda b, g, i, j: (g, 0, 0, 0)),
        ],
        out_specs=pl.BlockSpec((tq, G * dh), lambda b, g, i, j: (b * (t // tq) + i, g)),
        out_shape=jax.ShapeDtypeStruct((n, hkv * G * dh), jnp.float32),
        scratch_shapes=[pltpu.VMEM((2 * dh, G * tq), jnp.bfloat16), pltpu.VMEM((LANES, tq), jnp.float32),
                        slab, stat, wide, stat, wide],
        compiler_params=pltpu.CompilerParams(
            dimension_semantics=("arbitrary",) * 4, vmem_limit_bytes=VMEM_LIMIT_BYTES),
        name="nsa_attn",
    )(proj, gates, cmp_kv, cmp_kv, kn, vv, kn, vv, q_w.reshape(dh, 1), slopes)


def _nsa(proj, batch, q_norm_w, kc_norm_w, ks_norm_w, kw_norm_w, pos_k, pos_v, w_ck1, w_ck2, w_cv1, w_cv2):
    n = proj.shape[0]
    G, hkv = NSA_GROUP, NSA_KV_HEADS
    kn, vv, cf = _kv_prep(proj, ks_norm_w, kw_norm_w, 512, n // batch)
    half = CMP_STRIDE * NSA_HEAD_DIM
    pos = jnp.stack([pos_k.reshape(2, half), pos_v.reshape(2, half)])
    w1 = jnp.stack([w_ck1, w_cv1]).astype(jnp.bfloat16)
    w2 = jnp.stack([w_ck2, w_cv2]).astype(jnp.bfloat16)
    cmp_kv = _cmp_kv(cf, pos, w1, w2, kc_norm_w, batch)
    gates = proj[:, GATE_COL0:GATE_COL0 + N_GATES].reshape(n, hkv, 3 * G).transpose(1, 2, 0)
    slopes = jnp.asarray(2.0 ** (-8.0 * np.arange(1, NSA_Q_HEADS + 1) / NSA_Q_HEADS), jnp.float32) * LOG2E
    return _nsa_attn(proj, gates, cmp_kv, kn, vv, q_norm_w, slopes.reshape(hkv, G, 1, 1), batch)


BIG_NEG = -3.0e38
PEER_CAND = [(a, b) for a in range(PEER_TOPK) for b in range(PEER_TOPK) if (a + 1) * (b + 1) <= PEER_TOPK]
PEER_CAND_ROWS = -(-len(PEER_CAND) // 8) * 8


def _peer_route_body(h_ref, g_ref, wq_ref, keys_ref, xt_ref, rk2_ref, a2_ref, cnt_ref, r_ref,
                     qt_ref, sc_ref, cur_ref, top_ref, cand_ref):
    x = h_ref[...]
    hn = x * lax.rsqrt(jnp.mean(x * x, axis=-1, keepdims=True) + NORM_EPS) * g_ref[...]
    hnt = hn.T.astype(jnp.bfloat16)
    xt_ref[...] = hnt
    qt_ref[...] = jnp.dot(wq_ref[...], hnt, preferred_element_type=jnp.float32).astype(jnp.bfloat16)
    cand_ref[...] = jnp.full(cand_ref.shape, BIG_NEG, jnp.float32)
    kd = PEER_KEY_DIM // 2
    n_parts = 2 * PEER_HEADS
    for hp in range(n_parts):
        s = jnp.dot(keys_ref[hp], qt_ref[hp * kd:(hp + 1) * kd, :], preferred_element_type=jnp.float32)
        sc_ref[hp] = s
        cur_ref[hp] = s

    def extract(k, carry):
        cur = cur_ref[...]
        mk = jnp.max(cur, axis=1, keepdims=True)
        top_ref[k] = mk
        cur_ref[...] = jnp.where(cur == mk, BIG_NEG, cur)
        return carry

    lax.fori_loop(0, PEER_TOPK, extract, 0)

    def head(hd, carry):
        t1 = [top_ref[a, 2 * hd] for a in range(PEER_TOPK)]
        t2 = [top_ref[b, 2 * hd + 1] for b in range(PEER_TOPK)]
        cmax = t1[0] + t2[0]
        cands = [t1[a] + t2[b] for a, b in PEER_CAND]
        for i, c in enumerate(cands):
            cand_ref[i:i + 1, :] = c
        call = cand_ref[...]
        n_gt = jnp.zeros(call.shape, jnp.float32)
        for c in cands:
            n_gt = n_gt + jnp.where(c > call, 1.0, 0.0)
        tau = jnp.min(jnp.where(n_gt <= PEER_TOPK - 1, call, -BIG_NEG), axis=0, keepdims=True)
        z = jnp.sum(jnp.where(call >= tau, jnp.exp(call - cmax), 0.0), axis=0, keepdims=True)
        count = [None] * PEER_TOPK
        for (a, b), c in zip(PEER_CAND, cands):
            hit = jnp.where(c >= tau, 1.0, 0.0)
            count[a] = hit if count[a] is None else count[a] + hit
        s1 = sc_ref[2 * hd]
        s2 = sc_ref[2 * hd + 1]
        cnt = jnp.zeros(s1.shape, jnp.float32)
        rank2 = jnp.zeros(s2.shape, jnp.float32)
        for a in range(PEER_TOPK):
            cnt = jnp.where(s1 == t1[a], count[a], cnt)
            rank2 = rank2 + jnp.where(t2[a] > s2, 1.0, 0.0)
        rk2_ref[hd] = rank2.astype(jnp.bfloat16)
        a2_ref[hd] = jnp.exp(s2 - t2[0]).astype(jnp.bfloat16)
        cnt_ref[hd] = cnt
        r_ref[hd] = jnp.exp(s1 - t1[0]) / z
        return carry

    lax.fori_loop(0, PEER_HEADS, head, 0)


def _peer_route(h, gain, wq_t, keys, tt):
    n, d = h.shape
    hp, nk, kd = keys.shape
    stat = jax.ShapeDtypeStruct((PEER_HEADS, nk, n), jnp.float32)
    stat_bf16 = jax.ShapeDtypeStruct((PEER_HEADS, nk, n), jnp.bfloat16)
    stat_spec = pl.BlockSpec((PEER_HEADS, nk, tt), lambda i: (0, 0, i))
    return pl.pallas_call(
        _peer_route_body,
        grid=(n // tt,),
        in_specs=[
            pl.BlockSpec((tt, d), lambda i: (i, 0)),
            pl.BlockSpec((1, d), lambda i: (0, 0)),
            pl.BlockSpec(wq_t.shape, lambda i: (0, 0)),
            pl.BlockSpec(keys.shape, lambda i: (0, 0, 0)),
        ],
        out_specs=[pl.BlockSpec((d, tt), lambda i: (0, i)), stat_spec, stat_spec, stat_spec, stat_spec],
        out_shape=[jax.ShapeDtypeStruct((d, n), jnp.bfloat16), stat_bf16, stat_bf16, stat, stat],
        scratch_shapes=[pltpu.VMEM((wq_t.shape[0], tt), jnp.bfloat16),
                        pltpu.VMEM((hp, nk, tt), jnp.float32), pltpu.VMEM((hp, nk, tt), jnp.float32),
                        pltpu.VMEM((PEER_TOPK, hp, 1, tt), jnp.float32),
                        pltpu.VMEM((PEER_CAND_ROWS, tt), jnp.float32)],
        compiler_params=pltpu.CompilerParams(
            dimension_semantics=("arbitrary",), vmem_limit_bytes=VMEM_LIMIT_BYTES),
        name="peer_route",
    )(h, gain.reshape(1, d), wq_t, keys)


def _peer_experts_body(xt_ref, u_ref, vt_ref, rk2_ref, a2_ref, cnt_ref, r_ref, o_ref, *, n_i1):
    @pl.when(pl.program_id(1) == 0)
    def _():
        o_ref[...] = jnp.zeros(o_ref.shape, jnp.float32)

    act = _gelu(jnp.dot(u_ref[...], xt_ref[...], preferred_element_type=jnp.float32))
    parts = []
    for i1 in range(n_i1):
        w = None
        for hd in range(PEER_HEADS):
            picked = rk2_ref[hd] < cnt_ref[hd, i1:i1 + 1, :].astype(jnp.bfloat16)
            term = jnp.where(picked, a2_ref[hd], 0.0) * r_ref[hd, i1:i1 + 1, :].astype(jnp.bfloat16)
            w = term if w is None else w + term
        parts.append(w * act[i1 * N_KEYS:(i1 + 1) * N_KEYS].astype(jnp.bfloat16))
    p = jnp.concatenate(parts, axis=0)
    o_ref[...] += jnp.dot(vt_ref[...], p, preferred_element_type=jnp.float32)


def _peer_experts(xt, u, vt, rk2, a2, cnt, r, tt, te):
    d, n = xt.shape
    e = u.shape[0]
    n_i1 = te // N_KEYS
    full = pl.BlockSpec((PEER_HEADS, N_KEYS, tt), lambda i, j: (0, 0, i))
    part = pl.BlockSpec((PEER_HEADS, n_i1, tt), lambda i, j: (0, j, i))
    return pl.pallas_call(
        functools.partial(_peer_experts_body, n_i1=n_i1),
        grid=(n // tt, e // te),
        in_specs=[
            pl.BlockSpec((d, tt), lambda i, j: (0, i)),
            pl.BlockSpec((te, d), lambda i, j: (j, 0)),
            pl.BlockSpec((d, te), lambda i, j: (0, j)),
            full, full, part, part,
        ],
        out_specs=pl.BlockSpec((d, tt), lambda i, j: (0, i)),
        out_shape=jax.ShapeDtypeStruct((d, n), jnp.float32),
        compiler_params=pltpu.CompilerParams(
            dimension_semantics=("arbitrary", "arbitrary"), vmem_limit_bytes=PEER_VMEM_LIMIT_BYTES),
        name="peer_experts",
    )(xt, u, vt, rk2, a2, cnt, r)


def _transpose_cast_body(x_ref, o_ref):
    o_ref[...] = x_ref[...].T.astype(o_ref.dtype)


def _transpose_cast(x, blk, dtype):
    r, c = x.shape
    return pl.pallas_call(
        _transpose_cast_body,
        grid=(r // blk, c // blk),
        in_specs=[pl.BlockSpec((blk, blk), lambda i, j: (i, j))],
        out_specs=pl.BlockSpec((blk, blk), lambda i, j: (j, i)),
        out_shape=jax.ShapeDtypeStruct((c, r), dtype),
        compiler_params=pltpu.CompilerParams(
            dimension_semantics=("arbitrary", "arbitrary"), vmem_limit_bytes=VMEM_LIMIT_BYTES),
        name="transpose_cast",
    )(x)


def _transpose_add_body(x_ref, r_ref, o_ref):
    o_ref[...] = r_ref[...] + x_ref[...].T


def _transpose_add(x_t, res, blk):
    d, n = x_t.shape
    return pl.pallas_call(
        _transpose_add_body,
        grid=(d // blk, n // blk),
        in_specs=[pl.BlockSpec((blk, blk), lambda i, j: (i, j)),
                  pl.BlockSpec((blk, blk), lambda i, j: (j, i))],
        out_specs=pl.BlockSpec((blk, blk), lambda i, j: (j, i)),
        out_shape=jax.ShapeDtypeStruct((n, d), jnp.float32),
        compiler_params=pltpu.CompilerParams(
            dimension_semantics=("arbitrary", "arbitrary"), vmem_limit_bytes=VMEM_LIMIT_BYTES),
        name="transpose_add",
    )(x_t, res)


def _peer(h, gain, w_q, sub_keys, u_tab, v_tab):
    keys = sub_keys.reshape(PEER_HEADS * 2, N_KEYS, PEER_KEY_DIM // 2).astype(jnp.bfloat16)
    xt, rk2, a2, cnt, r = _peer_route(h, gain, _transpose_cast(w_q, 1024, jnp.bfloat16), keys, 256)
    out_t = _peer_experts(xt, u_tab.astype(jnp.bfloat16), _transpose_cast(v_tab, 1024, jnp.bfloat16),
                          rk2, a2, cnt, r, 512, 2048)
    return _transpose_add(out_t, h, min(1024, h.shape[0]))


def kernel(x, norm1_w, w_in, hg_lb_logits, hg_norm_w, q_norm_w, kc_norm_w, ks_norm_w, kw_norm_w,
           cmp_pos_k, cmp_pos_v, w_ck1, w_ck2, w_cv1, w_cv2, w_out, norm2_w,
           peer_w_q, peer_sub_keys, peer_u, peer_v):
    B, T, D = x.shape
    n = B * T
    layer = 0
    lower_bounds = jnp.cumsum(jax.nn.softmax(hg_lb_logits, axis=0), axis=0)
    xt = x.reshape(n, D)

    w_in_b = jnp.pad(w_in[layer].astype(jnp.bfloat16), ((0, 0), (0, IN_COLS_PADDED - IN_COLS)))
    proj = _norm_matmul(xt, norm1_w[layer], w_in_b, 1024, 1024)
    hg_out = _hgrn2(proj, lower_bounds[layer], hg_norm_w[layer], B)
    nsa_out = _nsa(proj, B, q_norm_w[layer], kc_norm_w[layer], ks_norm_w[layer], kw_norm_w[layer],
                   cmp_pos_k[layer], cmp_pos_v[layer], w_ck1[layer], w_ck2[layer], w_cv1[layer], w_cv2[layer])
    h = _out_proj(hg_out, nsa_out, w_out[layer].astype(jnp.bfloat16), xt, 1024, 1024)

    y = _peer(h, norm2_w[layer], peer_w_q[layer], peer_sub_keys[layer], peer_u[layer], peer_v[layer])
    return y.reshape(B, T, D)
```

```python
import functools

import jax
import jax.numpy as jnp
import numpy as np
from jax import lax
from jax.experimental import pallas as pl
from jax.experimental.pallas import tpu as pltpu

D_MODEL = 2048
HG_WIDTH = 1024
HG_HEAD_DIM = 128
HG_HEADS = 8
HG_CHUNK = 64
NSA_WIDTH = 1024
NSA_HEAD_DIM = 64
NSA_Q_HEADS = 16
NSA_KV_HEADS = 4
NSA_GROUP = 4
CMP_BLOCK = 32
CMP_STRIDE = 16
CMP_HIDDEN = 256
SEL_BLOCK = 64
N_SELECT = 16
N_LOCAL = 2
WINDOW = 512
FORCE_SCORE = 1e9
NEG_INF = -1e30
MASK_OFF = 3e30
LOG2E = float(np.log2(np.e))
PEER_HEADS = 8
N_KEYS = 128
PEER_KEY_DIM = 256
PEER_TOPK = 16
NORM_EPS = 1e-6
KV_W = NSA_KV_HEADS * NSA_HEAD_DIM
N_GATES = 3 * NSA_Q_HEADS
IN_SIZES = [HG_WIDTH] * 4 + [NSA_WIDTH] + [KV_W] * 6 + [N_GATES]
IN_COLS = sum(IN_SIZES)
IN_COLS_PADDED = 7168
Q_COL0 = 4 * HG_WIDTH
KV_COL0 = Q_COL0 + NSA_WIDTH
GATE_COL0 = KV_COL0 + 6 * KV_W

VMEM_LIMIT_BYTES = 48 * 1024 * 1024
LANES = 128
PEER_VMEM_LIMIT_BYTES = 60 * 1024 * 1024


def _norm_matmul_body(x_ref, g_ref, w_ref, o_ref, xn_ref):
    @pl.when(pl.program_id(1) == 0)
    def _():
        x = x_ref[...]
        r = lax.rsqrt(jnp.mean(x * x, axis=-1, keepdims=True) + NORM_EPS)
        xn_ref[...] = (x * r * g_ref[...]).astype(jnp.bfloat16)

    o_ref[...] = jnp.dot(xn_ref[...], w_ref[...], preferred_element_type=jnp.float32).astype(o_ref.dtype)


def _norm_matmul(x, gain, w, tm, tn, out_dtype=jnp.float32):
    m, k = x.shape
    n = w.shape[1]
    return pl.pallas_call(
        _norm_matmul_body,
        grid=(m // tm, n // tn),
        in_specs=[
            pl.BlockSpec((tm, k), lambda i, j: (i, 0)),
            pl.BlockSpec((1, k), lambda i, j: (0, 0)),
            pl.BlockSpec((k, tn), lambda i, j: (0, j)),
        ],
        out_specs=pl.BlockSpec((tm, tn), lambda i, j: (i, j)),
        out_shape=jax.ShapeDtypeStruct((m, n), out_dtype),
        scratch_shapes=[pltpu.VMEM((tm, k), jnp.bfloat16)],
        compiler_params=pltpu.CompilerParams(
            dimension_semantics=("arbitrary", "arbitrary"), vmem_limit_bytes=VMEM_LIMIT_BYTES),
        name="norm_matmul",
    )(x, gain.reshape(1, k), w)


def _out_proj_body(a1_ref, a2_ref, w1_ref, w2_ref, r_ref, o_ref):
    acc = jnp.dot(a1_ref[...].astype(jnp.bfloat16), w1_ref[...], preferred_element_type=jnp.float32)
    acc = acc + jnp.dot(a2_ref[...].astype(jnp.bfloat16), w2_ref[...], preferred_element_type=jnp.float32)
    o_ref[...] = r_ref[...] + acc


def _out_proj(a1, a2, w, res, tm, tn):
    m, k1 = a1.shape
    k2 = a2.shape[1]
    n = w.shape[1]
    return pl.pallas_call(
        _out_proj_body,
        grid=(m // tm, n // tn),
        in_specs=[
            pl.BlockSpec((tm, k1), lambda i, j: (i, 0)),
            pl.BlockSpec((tm, k2), lambda i, j: (i, 0)),
            pl.BlockSpec((k1, tn), lambda i, j: (0, j)),
            pl.BlockSpec((k2, tn), lambda i, j: (k1 // k2, j)),
            pl.BlockSpec((tm, tn), lambda i, j: (i, j)),
        ],
        out_specs=pl.BlockSpec((tm, tn), lambda i, j: (i, j)),
        out_shape=jax.ShapeDtypeStruct((m, n), jnp.float32),
        compiler_params=pltpu.CompilerParams(
            dimension_semantics=("arbitrary", "arbitrary"), vmem_limit_bytes=VMEM_LIMIT_BYTES),
        name="out_proj",
    )(a1, a2, w, w, res)


def _hgrn2_body(q_ref, f_ref, v_ref, g_ref, lb_ref, nw_ref, o_ref, qd_ref, kd_ref, ku_ref, vb_ref, dec_ref, sp_ref):
    t, dk = q_ref.shape
    C = HG_CHUNK
    nc = t // C
    lb = lb_ref[...]
    f = lb + (1.0 - lb) * jax.nn.sigmoid(f_ref[...])
    kf = 1.0 - f
    b = jnp.log(f)
    row = lax.broadcasted_iota(jnp.int32, (t, dk), 0) % C
    shift = 1
    while shift < C:
        b = b + jnp.where(row >= shift, pltpu.roll(b, shift, 0), 0.0)
        shift *= 2
    b3 = b.reshape(nc, C, dk)
    b_end = b3[:, C - 1:C, :]
    qd_ref[...] = (q_ref[...] * jnp.exp(b)).astype(jnp.bfloat16)
    kd_ref[...] = (kf * jnp.exp(-b)).astype(jnp.bfloat16)
    ku_ref[...] = (kf.reshape(nc, C, dk) * jnp.exp(b_end - b3)).reshape(t, dk).astype(jnp.bfloat16)
    vb_ref[...] = v_ref[...].astype(jnp.bfloat16)
    dec_ref[...] = jnp.exp(b_end)

    st = jnp.zeros((dk, dk), jnp.float32)
    for n in range(nc):
        rows = slice(n * C, (n + 1) * C)
        sp_ref[n] = st.astype(jnp.bfloat16)
        upd_t = lax.dot_general(vb_ref[rows, :], ku_ref[rows, :], (((0,), (0,)), ((), ())),
                                preferred_element_type=jnp.float32)
        st = dec_ref[n] * st + upd_t

    causal = lax.broadcasted_iota(jnp.int32, (C, C), 0) >= lax.broadcasted_iota(jnp.int32, (C, C), 1)
    for n in range(nc):
        rows = slice(n * C, (n + 1) * C)
        qd = qd_ref[rows, :]
        attn = lax.dot_general(qd, kd_ref[rows, :], (((1,), (1,)), ((), ())), preferred_element_type=jnp.float32)
        attn = jnp.where(causal, attn, 0.0).astype(jnp.bfloat16)
        o = jnp.dot(attn, vb_ref[rows, :], preferred_element_type=jnp.float32)
        o = o + lax.dot_general(qd, sp_ref[n], (((1,), (1,)), ((), ())), preferred_element_type=jnp.float32)
        o = o * lax.rsqrt(jnp.mean(o * o, axis=-1, keepdims=True) + NORM_EPS) * nw_ref[...]
        o_ref[rows, :] = o * jax.nn.silu(g_ref[rows, :])


def _hgrn2(proj, lb, norm_w, batch):
    n = proj.shape[0]
    t = n // batch
    dk, H = HG_HEAD_DIM, HG_HEADS
    nc = t // HG_CHUNK

    def part(p):
        return pl.BlockSpec((t, dk), lambda b, h: (b, p * H + h))

    slab = pltpu.VMEM((t, dk), jnp.bfloat16)
    return pl.pallas_call(
        _hgrn2_body,
        grid=(batch, H),
        in_specs=[part(0), part(1), part(2), part(3),
                  pl.BlockSpec((1, dk), lambda b, h: (0, h)),
                  pl.BlockSpec((1, dk), lambda b, h: (0, 0))],
        out_specs=pl.BlockSpec((t, dk), lambda b, h: (b, h)),
        out_shape=jax.ShapeDtypeStruct((n, H * dk), jnp.float32),
        scratch_shapes=[slab, slab, slab, slab, pltpu.VMEM((nc, 1, dk), jnp.float32),
                        pltpu.VMEM((nc, dk, dk), jnp.bfloat16)],
        compiler_params=pltpu.CompilerParams(
            dimension_semantics=("arbitrary", "arbitrary"), vmem_limit_bytes=VMEM_LIMIT_BYTES),
        name="hgrn2",
    )(proj, proj, proj, proj, lb.reshape(1, H * dk), norm_w.reshape(1, dk))


GELU_C0 = float(np.sqrt(2.0 / np.pi))
GELU_C1 = GELU_C0 * 0.044715


def _gelu(x):
    half = 0.5 * x
    return half + half * jnp.tanh(x * (GELU_C0 + GELU_C1 * (x * x)))


def _head_norm(x, w):
    return x * lax.rsqrt(jnp.mean(x * x, axis=-1, keepdims=True) + NORM_EPS) * w


def _value_slab_t(v):
    lane = lax.broadcasted_iota(jnp.int32, v.shape, 1)
    return jnp.concatenate([v, jnp.where(lane == 0, 1.0, 0.0)], axis=-1).T.astype(jnp.bfloat16)


POS_BASE = 64


def _key_slab(k, pos):
    lane = lax.broadcasted_iota(jnp.int32, k.shape, 1)
    hi = (pos // POS_BASE).astype(jnp.float32)
    lo = (pos % POS_BASE).astype(jnp.float32)
    tail = jnp.where(lane < 2, hi, jnp.where(lane < 4, lo, 0.0))
    return jnp.concatenate([k, tail], axis=-1).astype(jnp.bfloat16)


def _kv_prep_body(c_ref, s_ref, w_ref, ksw_ref, kww_ref, kn_ref, vv_ref, cf_ref, *, seq):
    dh = NSA_HEAD_DIM
    tm = c_ref.shape[0]
    pos = (pl.program_id(0) * tm) % seq + lax.broadcasted_iota(jnp.int32, (tm, dh), 0)
    for h in range(NSA_KV_HEADS):
        k_cols = slice(h * dh, (h + 1) * dh)
        v_cols = slice(KV_W + h * dh, KV_W + (h + 1) * dh)
        cf_ref[0, h] = c_ref[:, k_cols]
        cf_ref[1, h] = c_ref[:, v_cols]
        kn_ref[0, h] = _key_slab(_head_norm(s_ref[:, k_cols], ksw_ref[...]), pos)
        vv_ref[0, h] = _value_slab_t(s_ref[:, v_cols])
        kn_ref[1, h] = _key_slab(_head_norm(w_ref[:, k_cols], kww_ref[...]), pos)
        vv_ref[1, h] = _value_slab_t(w_ref[:, v_cols])


def _kv_prep(proj, ks_w, kw_w, tm, seq):
    n = proj.shape[0]
    dh = NSA_HEAD_DIM
    pair = 2 * KV_W
    col0 = KV_COL0 // pair
    head_block = pl.BlockSpec((2, NSA_KV_HEADS, tm, dh), lambda i: (0, 0, i, 0))
    wide_block = pl.BlockSpec((2, NSA_KV_HEADS, tm, 2 * dh), lambda i: (0, 0, i, 0))
    w_spec = pl.BlockSpec((1, dh), lambda i: (0, 0))
    return pl.pallas_call(
        functools.partial(_kv_prep_body, seq=seq),
        grid=(n // tm,),
        in_specs=[pl.BlockSpec((tm, pair), lambda i: (i, col0)),
                  pl.BlockSpec((tm, pair), lambda i: (i, col0 + 1)),
                  pl.BlockSpec((tm, pair), lambda i: (i, col0 + 2)),
                  w_spec, w_spec],
        out_specs=[wide_block, pl.BlockSpec((2, NSA_KV_HEADS, 2 * dh, tm), lambda i: (0, 0, 0, i)), head_block],
        out_shape=[jax.ShapeDtypeStruct((2, NSA_KV_HEADS, n, 2 * dh), jnp.bfloat16),
                   jax.ShapeDtypeStruct((2, NSA_KV_HEADS, 2 * dh, n), jnp.bfloat16),
                   jax.ShapeDtypeStruct((2, NSA_KV_HEADS, n, dh), jnp.float32)],
        compiler_params=pltpu.CompilerParams(
            dimension_semantics=("arbitrary",), vmem_limit_bytes=VMEM_LIMIT_BYTES),
        name="kv_prep",
    )(proj, proj, proj, ks_w.reshape(1, dh), kw_w.reshape(1, dh))


def _cmp_kv_body(r_ref, pos_ref, w1_ref, w2_ref, nw_ref, o_ref):
    half = CMP_STRIDE * NSA_HEAD_DIM
    n_strips = r_ref.shape[2]
    for kind in range(2):
        strips = r_ref[kind, 0]
        top = (strips + pos_ref[kind, 0:1, :]).astype(jnp.bfloat16)
        bot = (strips + pos_ref[kind, 1:2, :]).astype(jnp.bfloat16)
        a = jnp.dot(top, w1_ref[kind, :half, :], preferred_element_type=jnp.float32)
        b = jnp.dot(bot, w1_ref[kind, half:, :], preferred_element_type=jnp.float32)
        hid = _gelu(a + pltpu.roll(b, n_strips - 1, 0))
        out = jnp.dot(hid.astype(jnp.bfloat16), w2_ref[kind], preferred_element_type=jnp.float32)
        if kind == 0:
            out = _head_norm(out, nw_ref[...])
        o_ref[kind, 0, 0] = out.astype(jnp.bfloat16)


def _cmp_kv(cf, pos, w1, w2, kc_w, batch):
    _, hkv, n, dh = cf.shape
    t = n // batch
    n_strips = t // CMP_STRIDE
    strips = cf.reshape(2, hkv, n // CMP_STRIDE, CMP_STRIDE * dh)
    return pl.pallas_call(
        _cmp_kv_body,
        grid=(batch, hkv),
        in_specs=[pl.BlockSpec((2, 1, n_strips, CMP_STRIDE * dh), lambda b, g: (0, g, b, 0)),
                  pl.BlockSpec(pos.shape, lambda b, g: (0, 0, 0)),
                  pl.BlockSpec(w1.shape, lambda b, g: (0, 0, 0)),
                  pl.BlockSpec(w2.shape, lambda b, g: (0, 0, 0)),
                  pl.BlockSpec((1, dh), lambda b, g: (0, 0))],
        out_specs=pl.BlockSpec((2, 1, 1, n_strips, dh), lambda b, g: (0, g, b, 0, 0)),
        out_shape=jax.ShapeDtypeStruct((2, hkv, batch, n_strips, dh), jnp.bfloat16),
        compiler_params=pltpu.CompilerParams(
            dimension_semantics=("arbitrary", "arbitrary"), vmem_limit_bytes=VMEM_LIMIT_BYTES),
        name="cmp_kv",
    )(strips, pos, w1, w2, kc_w.reshape(1, dh))


def _flash_step(qt_ref, k, vt, m_ref, acc_ref, base, mask_fn, tq):
    tk = k.shape[0]
    dist = base + lax.broadcasted_iota(jnp.int32, (tk, tq), 1) - lax.broadcasted_iota(jnp.int32, (tk, tq), 0)
    off = jnp.where(mask_fn(dist), 0.0, MASK_OFF)
    for r in range(NSA_GROUP):
        cols = slice(r * tq, (r + 1) * tq)
        s = jnp.dot(k, qt_ref[:, cols], preferred_element_type=jnp.float32) - off
        m_old = m_ref[:, cols]
        m_new = jnp.maximum(m_old, jnp.max(s, axis=0, keepdims=True))
        p = jnp.exp2(s - m_new).astype(jnp.bfloat16)
        acc_ref[:, cols] = jnp.exp2(m_old - m_new) * acc_ref[:, cols] \
            + jnp.dot(vt, p, preferred_element_type=jnp.float32)
        m_ref[:, cols] = m_new


def _nsa_attn_body(q_ref, gate_ref, kc_ref, vc_ref, ks_ref, vs_ref, kw_ref, vw_ref, qw_ref, slope_ref, o_ref,
                   qt_ref, sel_ref, oc_ref, ms_ref, as_ref, mw_ref, aw_ref, *, tq, tk, n_sel):
    i = pl.program_id(2)
    j = pl.program_id(3)
    G, dh = NSA_GROUP, NSA_HEAD_DIM
    last_j = (i * tq + tq - 1) // tk

    @pl.when(j == 0)
    def _():
        xt = q_ref[...].T
        row = lax.broadcasted_iota(jnp.int32, (dh, tq), 0)
        for r in range(G):
            xr = xt[r * dh:(r + 1) * dh, :]
            inv = lax.rsqrt(jnp.mean(xr * xr, axis=0, keepdims=True) + NORM_EPS)
            qn = xr * inv * qw_ref[...] * (dh ** -0.5 * LOG2E)
            slope = slope_ref[0, r]
            s_hi = slope.astype(jnp.bfloat16).astype(jnp.float32)
            s_lo = slope - s_hi
            tail = jnp.where(row == 0, POS_BASE * s_hi, jnp.where(row == 1, POS_BASE * s_lo,
                             jnp.where(row == 2, s_hi, jnp.where(row == 3, s_lo, 0.0))))
            qt_ref[:, r * tq:(r + 1) * tq] = jnp.concatenate([qn, tail], axis=0).astype(jnp.bfloat16)
        for ref in (ms_ref, mw_ref):
            ref[...] = jnp.full(ref.shape, NEG_INF, jnp.float32)
        for ref in (as_ref, aw_ref):
            ref[...] = jnp.zeros(ref.shape, jnp.float32)

        n_cmp_pad = kc_ref.shape[-2]
        t_pos = i * tq + lax.broadcasted_iota(jnp.int32, (n_cmp_pad, tq), 1)
        blk = lax.broadcasted_iota(jnp.int32, (n_cmp_pad, tq), 0)
        dist = t_pos - (blk * CMP_STRIDE + CMP_BLOCK - 1)
        valid = (dist >= 0) & (blk < n_cmp_pad - 1)
        distf = dist.astype(jnp.float32)
        cb = lax.broadcasted_iota(jnp.int32, (n_sel, n_cmp_pad), 1) * CMP_STRIDE
        sb = lax.broadcasted_iota(jnp.int32, (n_sel, n_cmp_pad), 0) * SEL_BLOCK
        overlap_t = ((cb < sb + SEL_BLOCK) & (cb + CMP_BLOCK > sb)).astype(jnp.bfloat16)
        imp = None
        for r in range(G):
            cols = slice(r * tq, (r + 1) * tq)
            s = jnp.dot(kc_ref[0, 0, 0], qt_ref[:dh, cols], preferred_element_type=jnp.float32)
            s = jnp.where(valid, s - slope_ref[0, r] * distf, NEG_INF)
            p = jnp.where(valid, jnp.exp2(s - jnp.max(s, axis=0, keepdims=True)), 0.0)
            denom = jnp.maximum(jnp.sum(p, axis=0, keepdims=True), 1e-30)
            p = (p / denom).astype(jnp.bfloat16)
            oc_ref[:, cols] = lax.dot_general(vc_ref[0, 0, 0], p, (((0,), (0,)), ((), ())),
                                              preferred_element_type=jnp.float32)
            part = jnp.dot(overlap_t, p, preferred_element_type=jnp.float32)
            imp = part if imp is None else imp + part

        cur = (i * tq + lax.broadcasted_iota(jnp.int32, (n_sel, tq), 1)) // SEL_BLOCK
        jb = lax.broadcasted_iota(jnp.int32, (n_sel, tq), 0)
        forced = (jb == 0) | ((jb <= cur) & (jb > cur - N_LOCAL))
        imp = jnp.where(forced, FORCE_SCORE, jnp.where(jb > cur, -FORCE_SCORE, imp))
        rank = jnp.zeros((n_sel, tq), jnp.float32)
        for c in range(n_sel):
            row = imp[c:c + 1, :]
            ahead = (row > imp) | ((row == imp) & (jb > c))
            rank = rank + jnp.where(ahead, 1.0, 0.0)
        sel_t = jnp.where(rank < min(N_SELECT, n_sel), 1.0, 0.0)
        sel_t = jnp.concatenate([sel_t, jnp.zeros((sel_ref.shape[0] - n_sel, tq), jnp.float32)], axis=0)
        sel_ref[...] = sel_t

    @pl.when(j <= last_j)
    def _():
        per_tile = tk // SEL_BLOCK

        def mask(dist):
            picked = jnp.concatenate(
                [jnp.broadcast_to(sel_ref[pl.ds(j * per_tile + b, 1), :], (SEL_BLOCK, tq)) for b in range(per_tile)],
                axis=0)
            return (dist >= 0) & (picked > 0.5)

        _flash_step(qt_ref, ks_ref[0, 0], vs_ref[0, 0], ms_ref, as_ref, i * tq - j * tk, mask, tq)

    @pl.when((j <= last_j) & (j * tk + tk - 1 >= i * tq - (WINDOW - 1)))
    def _():
        _flash_step(qt_ref, kw_ref[0, 0], vw_ref[0, 0], mw_ref, aw_ref, i * tq - j * tk,
                    lambda dist: (dist >= 0) & (dist < WINDOW), tq)

    @pl.when(j == pl.num_programs(3) - 1)
    def _():
        gates = jax.nn.sigmoid(gate_ref[0])
        outs = []
        for r in range(G):
            cols = slice(r * tq, (r + 1) * tq)
            o_sel = as_ref[:dh, cols] / as_ref[dh:dh + 1, cols]
            o_win = aw_ref[:dh, cols] / aw_ref[dh:dh + 1, cols]
            outs.append(gates[3 * r:3 * r + 1] * oc_ref[:, cols] + gates[3 * r + 1:3 * r + 2] * o_sel
                        + gates[3 * r + 2:3 * r + 3] * o_win)
        o_ref[...] = jnp.concatenate(outs, axis=0).T


def _nsa_attn(proj, gates, cmp_kv, kn, vv, q_w, slopes, batch, tq=512, tk=512):
    n = proj.shape[0]
    t = n // batch
    G, dh, hkv = NSA_GROUP, NSA_HEAD_DIM, NSA_KV_HEADS
    n_cmp_pad = cmp_kv.shape[-2]
    n_sel = t // SEL_BLOCK
    q_blk0 = Q_COL0 // (G * dh)

    def kv_tile(b, i, j):
        return b * (t // tk) + jnp.minimum(j, (i * tq + tq - 1) // tk)

    def k_map(kind):
        return lambda b, g, i, j: (kind, g, kv_tile(b, i, j), 0)

    def v_map(kind):
        return lambda b, g, i, j: (kind, g, 0, kv_tile(b, i, j))

    def cmp_map(kind):
        return lambda b, g, i, j: (kind, g, b, 0, 0)

    cmp_block = (1, 1, 1, n_cmp_pad, dh)
    k_block = (1, 1, tk, 2 * dh)
    v_block = (1, 1, 2 * dh, tk)
    slab = pltpu.VMEM((dh, G * tq), jnp.float32)
    wide = pltpu.VMEM((2 * dh, G * tq), jnp.float32)
    stat = pltpu.VMEM((1, G * tq), jnp.float32)
    return pl.pallas_call(
        functools.partial(_nsa_attn_body, tq=tq, tk=tk, n_sel=n_sel),
        grid=(batch, hkv, t // tq, t // tk),
        in_specs=[
            pl.BlockSpec((tq, G * dh), lambda b, g, i, j: (b * (t // tq) + i, q_blk0 + g)),
            pl.BlockSpec((1, 3 * G, tq), lambda b, g, i, j: (g, 0, b * (t // tq) + i)),
            pl.BlockSpec(cmp_block, cmp_map(0)), pl.BlockSpec(cmp_block, cmp_map(1)),
            pl.BlockSpec(k_block, k_map(0)), pl.BlockSpec(v_block, v_map(0)),
            pl.BlockSpec(k_block, k_map(1)), pl.BlockSpec(v_block, v_map(1)),
            pl.BlockSpec((dh, 1), lambda b, g, i, j: (0, 0)),
            pl.BlockSpec((1, G, 1, 1), lambda b, g, i, j: (g, 0, 0, 0)),
        ],
        out_specs=pl.BlockSpec((tq, G * dh), lambda b, g, i, j: (b * (t // tq) + i, g)),
        out_shape=jax.ShapeDtypeStruct((n, hkv * G * dh), jnp.float32),
        scratch_shapes=[pltpu.VMEM((2 * dh, G * tq), jnp.bfloat16), pltpu.VMEM((LANES, tq), jnp.float32),
                        slab, stat, wide, stat, wide],
        compiler_params=pltpu.CompilerParams(
            dimension_semantics=("arbitrary",) * 4, vmem_limit_bytes=VMEM_LIMIT_BYTES),
        name="nsa_attn",
    )(proj, gates, cmp_kv, cmp_kv, kn, vv, kn, vv, q_w.reshape(dh, 1), slopes)


def _nsa(proj, batch, q_norm_w, kc_norm_w, ks_norm_w, kw_norm_w, pos_k, pos_v, w_ck1, w_ck2, w_cv1, w_cv2):
    n = proj.shape[0]
    G, hkv = NSA_GROUP, NSA_KV_HEADS
    kn, vv, cf = _kv_prep(proj, ks_norm_w, kw_norm_w, 512, n // batch)
    half = CMP_STRIDE * NSA_HEAD_DIM
    pos = jnp.stack([pos_k.reshape(2, half), pos_v.reshape(2, half)])
    w1 = jnp.stack([w_ck1, w_cv1]).astype(jnp.bfloat16)
    w2 = jnp.stack([w_ck2, w_cv2]).astype(jnp.bfloat16)
    cmp_kv = _cmp_kv(cf, pos, w1, w2, kc_norm_w, batch)
    gates = proj[:, GATE_COL0:GATE_COL0 + N_GATES].reshape(n, hkv, 3 * G).transpose(1, 2, 0)
    slopes = jnp.asarray(2.0 ** (-8.0 * np.arange(1, NSA_Q_HEADS + 1) / NSA_Q_HEADS), jnp.float32) * LOG2E
    return _nsa_attn(proj, gates, cmp_kv, kn, vv, q_norm_w, slopes.reshape(hkv, G, 1, 1), batch)


BIG_NEG = -3.0e38
PEER_CAND = [(a, b) for a in range(PEER_TOPK) for b in range(PEER_TOPK) if (a + 1) * (b + 1) <= PEER_TOPK]
PEER_CAND_ROWS = -(-len(PEER_CAND) // 8) * 8


def _peer_route_body(h_ref, g_ref, wq_ref, keys_ref, xt_ref, rk2_ref, a2_ref, cnt_ref, r_ref,
                     qt_ref, sc_ref, cur_ref, top_ref, cand_ref):
    x = h_ref[...]
    hn = x * lax.rsqrt(jnp.mean(x * x, axis=-1, keepdims=True) + NORM_EPS) * g_ref[...]
    hnt = hn.T.astype(jnp.bfloat16)
    xt_ref[...] = hnt
    qt_ref[...] = jnp.dot(wq_ref[...], hnt, preferred_element_type=jnp.float32).astype(jnp.bfloat16)
    cand_ref[...] = jnp.full(cand_ref.shape, BIG_NEG, jnp.float32)
    kd = PEER_KEY_DIM // 2
    n_parts = 2 * PEER_HEADS
    for hp in range(n_parts):
        s = jnp.dot(keys_ref[hp], qt_ref[hp * kd:(hp + 1) * kd, :], preferred_element_type=jnp.float32)
        sc_ref[hp] = s
        cur_ref[hp] = s

    def extract(k, carry):
        cur = cur_ref[...]
        mk = jnp.max(cur, axis=1, keepdims=True)
        top_ref[k] = mk
        cur_ref[...] = jnp.where(cur == mk, BIG_NEG, cur)
        return carry

    lax.fori_loop(0, PEER_TOPK, extract, 0)

    def head(hd, carry):
        t1 = [top_ref[a, 2 * hd] for a in range(PEER_TOPK)]
        t2 = [top_ref[b, 2 * hd + 1] for b in range(PEER_TOPK)]
        cmax = t1[0] + t2[0]
        cands = [t1[a] + t2[b] for a, b in PEER_CAND]
        for i, c in enumerate(cands):
            cand_ref[i:i + 1, :] = c
        call = cand_ref[...]
        n_gt = jnp.zeros(call.shape, jnp.float32)
        for c in cands:
            n_gt = n_gt + jnp.where(c > call, 1.0, 0.0)
        tau = jnp.min(jnp.where(n_gt <= PEER_TOPK - 1, call, -BIG_NEG), axis=0, keepdims=True)
        z = jnp.sum(jnp.where(call >= tau, jnp.exp(call - cmax), 0.0), axis=0, keepdims=True)
        count = [None] * PEER_TOPK
        for (a, b), c in zip(PEER_CAND, cands):
            hit = jnp.where(c >= tau, 1.0, 0.0)
            count[a] = hit if count[a] is None else count[a] + hit
        s1 = sc_ref[2 * hd]
        s2 = sc_ref[2 * hd + 1]
        cnt = jnp.zeros(s1.shape, jnp.float32)
        rank2 = jnp.zeros(s2.shape, jnp.float32)
        for a in range(PEER_TOPK):
            cnt = jnp.where(s1 == t1[a], count[a], cnt)
            rank2 = rank2 + jnp.where(t2[a] > s2, 1.0, 0.0)
        rk2_ref[hd] = rank2.astype(jnp.bfloat16)
        a2_ref[hd] = jnp.exp(s2 - t2[0]).astype(jnp.bfloat16)
        cnt_ref[hd] = cnt
        r_ref[hd] = jnp.exp(s1 - t1[0]) / z
        return carry

    lax.fori_loop(0, PEER_HEADS, head, 0)


def _peer_route(h, gain, wq_t, keys, tt):
    n, d = h.shape
    hp, nk, kd = keys.shape
    stat = jax.ShapeDtypeStruct((PEER_HEADS, nk, n), jnp.float32)
    stat_bf16 = jax.ShapeDtypeStruct((PEER_HEADS, nk, n), jnp.bfloat16)
    stat_spec = pl.BlockSpec((PEER_HEADS, nk, tt), lambda i: (0, 0, i))
    return pl.pallas_call(
        _peer_route_body,
        grid=(n // tt,),
        in_specs=[
            pl.BlockSpec((tt, d), lambda i: (i, 0)),
            pl.BlockSpec((1, d), lambda i: (0, 0)),
            pl.BlockSpec(wq_t.shape, lambda i: (0, 0)),
            pl.BlockSpec(keys.shape, lambda i: (0, 0, 0)),
        ],
        out_specs=[pl.BlockSpec((d, tt), lambda i: (0, i)), stat_spec, stat_spec, stat_spec, stat_spec],
        out_shape=[jax.ShapeDtypeStruct((d, n), jnp.bfloat16), stat_bf16, stat_bf16, stat, stat],
        scratch_shapes=[pltpu.VMEM((wq_t.shape[0], tt), jnp.bfloat16),
                        pltpu.VMEM((hp, nk, tt), jnp.float32), pltpu.VMEM((hp, nk, tt), jnp.float32),
                        pltpu.VMEM((PEER_TOPK, hp, 1, tt), jnp.float32),
                        pltpu.VMEM((PEER_CAND_ROWS, tt), jnp.float32)],
        compiler_params=pltpu.CompilerParams(
            dimension_semantics=("arbitrary",), vmem_limit_bytes=VMEM_LIMIT_BYTES),
        name="peer_route",
    )(h, gain.reshape(1, d), wq_t, keys)


PEER_I1_PER_PIECE = 2


def _peer_experts_body(xt_ref, u_ref, vt_ref, rk2_ref, a2_ref, cnt_ref, r_ref, o_ref, w_ref, *, n_i1):
    @pl.when(pl.program_id(1) == 0)
    def _():
        o_ref[...] = jnp.zeros(o_ref.shape, jnp.float32)

    tt = xt_ref.shape[1]
    piece = PEER_I1_PER_PIECE * N_KEYS
    n_piece = n_i1 // PEER_I1_PER_PIECE
    parts = []
    bump = None
    for k in range(n_piece + 1):
        if k < n_piece:
            for i1 in range(k * PEER_I1_PER_PIECE, (k + 1) * PEER_I1_PER_PIECE):
                w = None
                for hd in range(PEER_HEADS):
                    picked = rk2_ref[hd] < cnt_ref[hd, i1:i1 + 1, :].astype(jnp.bfloat16)
                    term = jnp.where(picked, a2_ref[hd], 0.0) * r_ref[hd, i1:i1 + 1, :].astype(jnp.bfloat16)
                    w = term if w is None else w + term
                w_ref[i1 * N_KEYS:(i1 + 1) * N_KEYS, :] = w
            last = w[N_KEYS - 16:, tt - LANES:].astype(jnp.float32)
            new_bump = jnp.where(jnp.max(last) > 1e30, 1, 0)
        if k >= 1:
            start = pl.multiple_of((k - 1 + bump) * piece, piece)
            x = jnp.dot(u_ref[pl.ds(start, piece), :], xt_ref[...], preferred_element_type=jnp.float32)
            parts.append(w_ref[(k - 1) * piece:k * piece, :] * _gelu(x).astype(jnp.bfloat16))
        bump = new_bump
    p = jnp.concatenate(parts, axis=0)
    o_ref[...] += jnp.dot(vt_ref[...], p, preferred_element_type=jnp.float32)


def _peer_experts(xt, u, vt, rk2, a2, cnt, r, tt, te):
    d, n = xt.shape
    e = u.shape[0]
    n_i1 = te // N_KEYS
    full = pl.BlockSpec((PEER_HEADS, N_KEYS, tt), lambda i, j: (0, 0, i))
    part = pl.BlockSpec((PEER_HEADS, n_i1, tt), lambda i, j: (0, j, i))
    return pl.pallas_call(
        functools.partial(_peer_experts_body, n_i1=n_i1),
        grid=(n // tt, e // te),
        in_specs=[
            pl.BlockSpec((d, tt), lambda i, j: (0, i)),
            pl.BlockSpec((te, d), lambda i, j: (j, 0)),
            pl.BlockSpec((d, te), lambda i, j: (0, j)),
            full, full, part, part,
        ],
        out_specs=pl.BlockSpec((d, tt), lambda i, j: (0, i)),
        out_shape=jax.ShapeDtypeStruct((d, n), jnp.float32),
        scratch_shapes=[pltpu.VMEM((te, tt), jnp.bfloat16)],
        compiler_params=pltpu.CompilerParams(
            dimension_semantics=("arbitrary", "arbitrary"), vmem_limit_bytes=PEER_VMEM_LIMIT_BYTES),
        name="peer_experts",
    )(xt, u, vt, rk2, a2, cnt, r)


def _transpose_cast_body(x_ref, o_ref):
    o_ref[...] = x_ref[...].T.astype(o_ref.dtype)


def _transpose_cast(x, blk, dtype):
    r, c = x.shape
    return pl.pallas_call(
        _transpose_cast_body,
        grid=(r // blk, c // blk),
        in_specs=[pl.BlockSpec((blk, blk), lambda i, j: (i, j))],
        out_specs=pl.BlockSpec((blk, blk), lambda i, j: (j, i)),
        out_shape=jax.ShapeDtypeStruct((c, r), dtype),
        compiler_params=pltpu.CompilerParams(
            dimension_semantics=("arbitrary", "arbitrary"), vmem_limit_bytes=VMEM_LIMIT_BYTES),
        name="transpose_cast",
    )(x)


def _transpose_add_body(x_ref, r_ref, o_ref):
    o_ref[...] = r_ref[...] + x_ref[...].T


def _transpose_add(x_t, res, blk):
    d, n = x_t.shape
    return pl.pallas_call(
        _transpose_add_body,
        grid=(d // blk, n // blk),
        in_specs=[pl.BlockSpec((blk, blk), lambda i, j: (i, j)),
                  pl.BlockSpec((blk, blk), lambda i, j: (j, i))],
        out_specs=pl.BlockSpec((blk, blk), lambda i, j: (j, i)),
        out_shape=jax.ShapeDtypeStruct((n, d), jnp.float32),
        compiler_params=pltpu.CompilerParams(
            dimension_semantics=("arbitrary", "arbitrary"), vmem_limit_bytes=VMEM_LIMIT_BYTES),
        name="transpose_add",
    )(x_t, res)


def _peer(h, gain, w_q, sub_keys, u_tab, v_tab):
    keys = sub_keys.reshape(PEER_HEADS * 2, N_KEYS, PEER_KEY_DIM // 2).astype(jnp.bfloat16)
    xt, rk2, a2, cnt, r = _peer_route(h, gain, _transpose_cast(w_q, 1024, jnp.bfloat16), keys, 256)
    out_t = _peer_experts(xt, u_tab.astype(jnp.bfloat16), _transpose_cast(v_tab, 1024, jnp.bfloat16),
                          rk2, a2, cnt, r, 512, 2048)
    return _transpose_add(out_t, h, min(1024, h.shape[0]))


def kernel(x, norm1_w, w_in, hg_lb_logits, hg_norm_w, q_norm_w, kc_norm_w, ks_norm_w, kw_norm_w,
           cmp_pos_k, cmp_pos_v, w_ck1, w_ck2, w_cv1, w_cv2, w_out, norm2_w,
           peer_w_q, peer_sub_keys, peer_u, peer_v):
    B, T, D = x.shape
    n = B * T
    layer = 0
    lower_bounds = jnp.cumsum(jax.nn.softmax(hg_lb_logits, axis=0), axis=0)
    xt = x.reshape(n, D)

    w_in_b = jnp.pad(w_in[layer].astype(jnp.bfloat16), ((0, 0), (0, IN_COLS_PADDED - IN_COLS)))
    proj = _norm_matmul(xt, norm1_w[layer], w_in_b, 1024, 1024)
    hg_out = _hgrn2(proj, lower_bounds[layer], hg_norm_w[layer], B)
    nsa_out = _nsa(proj, B, q_norm_w[layer], kc_norm_w[layer], ks_norm_w[layer], kw_norm_w[layer],
                   cmp_pos_k[layer], cmp_pos_v[layer], w_ck1[layer], w_ck2[layer], w_cv1[layer], w_cv2[layer])
    h = _out_proj(hg_out, nsa_out, w_out[layer].astype(jnp.bfloat16), xt, 1024, 1024)

    y = _peer(h, norm2_w[layer], peer_w_q[layer], peer_sub_keys[layer], peer_u[layer], peer_v[layer])
    return y.reshape(B, T, D)
```

```python
import functools

import jax
import jax.numpy as jnp
import numpy as np
from jax import lax
from jax.experimental import pallas as pl
from jax.experimental.pallas import tpu as pltpu

D_MODEL = 2048
HG_WIDTH = 1024
HG_HEAD_DIM = 128
HG_HEADS = 8
HG_CHUNK = 64
NSA_WIDTH = 1024
NSA_HEAD_DIM = 64
NSA_Q_HEADS = 16
NSA_KV_HEADS = 4
NSA_GROUP = 4
CMP_BLOCK = 32
CMP_STRIDE = 16
CMP_HIDDEN = 256
SEL_BLOCK = 64
N_SELECT = 16
N_LOCAL = 2
WINDOW = 512
FORCE_SCORE = 1e9
NEG_INF = -1e30
MASK_OFF = 3e30
LOG2E = float(np.log2(np.e))
PEER_HEADS = 8
N_KEYS = 128
PEER_KEY_DIM = 256
PEER_TOPK = 16
NORM_EPS = 1e-6
KV_W = NSA_KV_HEADS * NSA_HEAD_DIM
N_GATES = 3 * NSA_Q_HEADS
IN_SIZES = [HG_WIDTH] * 4 + [NSA_WIDTH] + [KV_W] * 6 + [N_GATES]
IN_COLS = sum(IN_SIZES)
IN_COLS_PADDED = 7168
Q_COL0 = 4 * HG_WIDTH
KV_COL0 = Q_COL0 + NSA_WIDTH
GATE_COL0 = KV_COL0 + 6 * KV_W

VMEM_LIMIT_BYTES = 48 * 1024 * 1024
LANES = 128
PEER_VMEM_LIMIT_BYTES = 60 * 1024 * 1024


def _norm_matmul_body(x_ref, g_ref, w_ref, o_ref, xn_ref):
    @pl.when(pl.program_id(1) == 0)
    def _():
        x = x_ref[...]
        r = lax.rsqrt(jnp.mean(x * x, axis=-1, keepdims=True) + NORM_EPS)
        xn_ref[...] = (x * r * g_ref[...]).astype(jnp.bfloat16)

    o_ref[...] = jnp.dot(xn_ref[...], w_ref[...], preferred_element_type=jnp.float32).astype(o_ref.dtype)


def _norm_matmul(x, gain, w, tm, tn, out_dtype=jnp.float32):
    m, k = x.shape
    n = w.shape[1]
    return pl.pallas_call(
        _norm_matmul_body,
        grid=(m // tm, n // tn),
        in_specs=[
            pl.BlockSpec((tm, k), lambda i, j: (i, 0)),
            pl.BlockSpec((1, k), lambda i, j: (0, 0)),
            pl.BlockSpec((k, tn), lambda i, j: (0, j)),
        ],
        out_specs=pl.BlockSpec((tm, tn), lambda i, j: (i, j)),
        out_shape=jax.ShapeDtypeStruct((m, n), out_dtype),
        scratch_shapes=[pltpu.VMEM((tm, k), jnp.bfloat16)],
        compiler_params=pltpu.CompilerParams(
            dimension_semantics=("arbitrary", "arbitrary"), vmem_limit_bytes=VMEM_LIMIT_BYTES),
        name="norm_matmul",
    )(x, gain.reshape(1, k), w)


def _out_proj_body(a1_ref, a2_ref, w1_ref, w2_ref, r_ref, o_ref):
    acc = jnp.dot(a1_ref[...].astype(jnp.bfloat16), w1_ref[...], preferred_element_type=jnp.float32)
    acc = acc + jnp.dot(a2_ref[...].astype(jnp.bfloat16), w2_ref[...], preferred_element_type=jnp.float32)
    o_ref[...] = r_ref[...] + acc


def _out_proj(a1, a2, w, res, tm, tn):
    m, k1 = a1.shape
    k2 = a2.shape[1]
    n = w.shape[1]
    return pl.pallas_call(
        _out_proj_body,
        grid=(m // tm, n // tn),
        in_specs=[
            pl.BlockSpec((tm, k1), lambda i, j: (i, 0)),
            pl.BlockSpec((tm, k2), lambda i, j: (i, 0)),
            pl.BlockSpec((k1, tn), lambda i, j: (0, j)),
            pl.BlockSpec((k2, tn), lambda i, j: (k1 // k2, j)),
            pl.BlockSpec((tm, tn), lambda i, j: (i, j)),
        ],
        out_specs=pl.BlockSpec((tm, tn), lambda i, j: (i, j)),
        out_shape=jax.ShapeDtypeStruct((m, n), jnp.float32),
        compiler_params=pltpu.CompilerParams(
            dimension_semantics=("arbitrary", "arbitrary"), vmem_limit_bytes=VMEM_LIMIT_BYTES),
        name="out_proj",
    )(a1, a2, w, w, res)


def _hgrn2_body(q_ref, f_ref, v_ref, g_ref, lb_ref, nw_ref, o_ref, qd_ref, kd_ref, ku_ref, vb_ref, dec_ref, sp_ref):
    t, dk = q_ref.shape
    C = HG_CHUNK
    nc = t // C
    lb = lb_ref[...]
    f = lb + (1.0 - lb) * jax.nn.sigmoid(f_ref[...])
    kf = 1.0 - f
    b = jnp.log(f)
    row = lax.broadcasted_iota(jnp.int32, (t, dk), 0) % C
    shift = 1
    while shift < C:
        b = b + jnp.where(row >= shift, pltpu.roll(b, shift, 0), 0.0)
        shift *= 2
    b3 = b.reshape(nc, C, dk)
    b_end = b3[:, C - 1:C, :]
    qd_ref[...] = (q_ref[...] * jnp.exp(b)).astype(jnp.bfloat16)
    kd_ref[...] = (kf * jnp.exp(-b)).astype(jnp.bfloat16)
    ku_ref[...] = (kf.reshape(nc, C, dk) * jnp.exp(b_end - b3)).reshape(t, dk).astype(jnp.bfloat16)
    vb_ref[...] = v_ref[...].astype(jnp.bfloat16)
    dec_ref[...] = jnp.exp(b_end)

    st = jnp.zeros((dk, dk), jnp.float32)
    for n in range(nc):
        rows = slice(n * C, (n + 1) * C)
        sp_ref[n] = st.astype(jnp.bfloat16)
        upd_t = lax.dot_general(vb_ref[rows, :], ku_ref[rows, :], (((0,), (0,)), ((), ())),
                                preferred_element_type=jnp.float32)
        st = dec_ref[n] * st + upd_t

    causal = lax.broadcasted_iota(jnp.int32, (C, C), 0) >= lax.broadcasted_iota(jnp.int32, (C, C), 1)
    for n in range(nc):
        rows = slice(n * C, (n + 1) * C)
        qd = qd_ref[rows, :]
        attn = lax.dot_general(qd, kd_ref[rows, :], (((1,), (1,)), ((), ())), preferred_element_type=jnp.float32)
        attn = jnp.where(causal, attn, 0.0).astype(jnp.bfloat16)
        o = jnp.dot(attn, vb_ref[rows, :], preferred_element_type=jnp.float32)
        o = o + lax.dot_general(qd, sp_ref[n], (((1,), (1,)), ((), ())), preferred_element_type=jnp.float32)
        o = o * lax.rsqrt(jnp.mean(o * o, axis=-1, keepdims=True) + NORM_EPS) * nw_ref[...]
        o_ref[rows, :] = o * jax.nn.silu(g_ref[rows, :])


def _hgrn2(proj, lb, norm_w, batch):
    n = proj.shape[0]
    t = n // batch
    dk, H = HG_HEAD_DIM, HG_HEADS
    nc = t // HG_CHUNK

    def part(p):
        return pl.BlockSpec((t, dk), lambda b, h: (b, p * H + h))

    slab = pltpu.VMEM((t, dk), jnp.bfloat16)
    return pl.pallas_call(
        _hgrn2_body,
        grid=(batch, H),
        in_specs=[part(0), part(1), part(2), part(3),
                  pl.BlockSpec((1, dk), lambda b, h: (0, h)),
                  pl.BlockSpec((1, dk), lambda b, h: (0, 0))],
        out_specs=pl.BlockSpec((t, dk), lambda b, h: (b, h)),
        out_shape=jax.ShapeDtypeStruct((n, H * dk), jnp.float32),
        scratch_shapes=[slab, slab, slab, slab, pltpu.VMEM((nc, 1, dk), jnp.float32),
                        pltpu.VMEM((nc, dk, dk), jnp.bfloat16)],
        compiler_params=pltpu.CompilerParams(
            dimension_semantics=("arbitrary", "arbitrary"), vmem_limit_bytes=VMEM_LIMIT_BYTES),
        name="hgrn2",
    )(proj, proj, proj, proj, lb.reshape(1, H * dk), norm_w.reshape(1, dk))


GELU_C0 = float(np.sqrt(2.0 / np.pi))
GELU_C1 = GELU_C0 * 0.044715


def _gelu(x):
    half = 0.5 * x
    return half + half * jnp.tanh(x * (GELU_C0 + GELU_C1 * (x * x)))


def _head_norm(x, w):
    return x * lax.rsqrt(jnp.mean(x * x, axis=-1, keepdims=True) + NORM_EPS) * w


def _value_slab_t(v):
    lane = lax.broadcasted_iota(jnp.int32, v.shape, 1)
    return jnp.concatenate([v, jnp.where(lane == 0, 1.0, 0.0)], axis=-1).T.astype(jnp.bfloat16)


POS_BASE = 64


def _key_slab(k, pos):
    lane = lax.broadcasted_iota(jnp.int32, k.shape, 1)
    hi = (pos // POS_BASE).astype(jnp.float32)
    lo = (pos % POS_BASE).astype(jnp.float32)
    tail = jnp.where(lane < 2, hi, jnp.where(lane < 4, lo, 0.0))
    return jnp.concatenate([k, tail], axis=-1).astype(jnp.bfloat16)


def _kv_prep_body(c_ref, s_ref, w_ref, ksw_ref, kww_ref, kn_ref, vv_ref, cf_ref, *, seq):
    dh = NSA_HEAD_DIM
    tm = c_ref.shape[0]
    pos = (pl.program_id(0) * tm) % seq + lax.broadcasted_iota(jnp.int32, (tm, dh), 0)
    for h in range(NSA_KV_HEADS):
        k_cols = slice(h * dh, (h + 1) * dh)
        v_cols = slice(KV_W + h * dh, KV_W + (h + 1) * dh)
        cf_ref[0, h] = c_ref[:, k_cols]
        cf_ref[1, h] = c_ref[:, v_cols]
        kn_ref[0, h] = _key_slab(_head_norm(s_ref[:, k_cols], ksw_ref[...]), pos)
        vv_ref[0, h] = _value_slab_t(s_ref[:, v_cols])
        kn_ref[1, h] = _key_slab(_head_norm(w_ref[:, k_cols], kww_ref[...]), pos)
        vv_ref[1, h] = _value_slab_t(w_ref[:, v_cols])


def _kv_prep(proj, ks_w, kw_w, tm, seq):
    n = proj.shape[0]
    dh = NSA_HEAD_DIM
    pair = 2 * KV_W
    col0 = KV_COL0 // pair
    head_block = pl.BlockSpec((2, NSA_KV_HEADS, tm, dh), lambda i: (0, 0, i, 0))
    wide_block = pl.BlockSpec((2, NSA_KV_HEADS, tm, 2 * dh), lambda i: (0, 0, i, 0))
    w_spec = pl.BlockSpec((1, dh), lambda i: (0, 0))
    return pl.pallas_call(
        functools.partial(_kv_prep_body, seq=seq),
        grid=(n // tm,),
        in_specs=[pl.BlockSpec((tm, pair), lambda i: (i, col0)),
                  pl.BlockSpec((tm, pair), lambda i: (i, col0 + 1)),
                  pl.BlockSpec((tm, pair), lambda i: (i, col0 + 2)),
                  w_spec, w_spec],
        out_specs=[wide_block, pl.BlockSpec((2, NSA_KV_HEADS, 2 * dh, tm), lambda i: (0, 0, 0, i)), head_block],
        out_shape=[jax.ShapeDtypeStruct((2, NSA_KV_HEADS, n, 2 * dh), jnp.bfloat16),
                   jax.ShapeDtypeStruct((2, NSA_KV_HEADS, 2 * dh, n), jnp.bfloat16),
                   jax.ShapeDtypeStruct((2, NSA_KV_HEADS, n, dh), jnp.float32)],
        compiler_params=pltpu.CompilerParams(
            dimension_semantics=("arbitrary",), vmem_limit_bytes=VMEM_LIMIT_BYTES),
        name="kv_prep",
    )(proj, proj, proj, ks_w.reshape(1, dh), kw_w.reshape(1, dh))


def _cmp_kv_body(r_ref, pos_ref, w1_ref, w2_ref, nw_ref, o_ref):
    half = CMP_STRIDE * NSA_HEAD_DIM
    n_strips = r_ref.shape[2]
    for kind in range(2):
        strips = r_ref[kind, 0]
        top = (strips + pos_ref[kind, 0:1, :]).astype(jnp.bfloat16)
        bot = (strips + pos_ref[kind, 1:2, :]).astype(jnp.bfloat16)
        a = jnp.dot(top, w1_ref[kind, :half, :], preferred_element_type=jnp.float32)
        b = jnp.dot(bot, w1_ref[kind, half:, :], preferred_element_type=jnp.float32)
        hid = _gelu(a + pltpu.roll(b, n_strips - 1, 0))
        out = jnp.dot(hid.astype(jnp.bfloat16), w2_ref[kind], preferred_element_type=jnp.float32)
        if kind == 0:
            out = _head_norm(out, nw_ref[...])
        o_ref[kind, 0, 0] = out.astype(jnp.bfloat16)


def _cmp_kv(cf, pos, w1, w2, kc_w, batch):
    _, hkv, n, dh = cf.shape
    t = n // batch
    n_strips = t // CMP_STRIDE
    strips = cf.reshape(2, hkv, n // CMP_STRIDE, CMP_STRIDE * dh)
    return pl.pallas_call(
        _cmp_kv_body,
        grid=(batch, hkv),
        in_specs=[pl.BlockSpec((2, 1, n_strips, CMP_STRIDE * dh), lambda b, g: (0, g, b, 0)),
                  pl.BlockSpec(pos.shape, lambda b, g: (0, 0, 0)),
                  pl.BlockSpec(w1.shape, lambda b, g: (0, 0, 0)),
                  pl.BlockSpec(w2.shape, lambda b, g: (0, 0, 0)),
                  pl.BlockSpec((1, dh), lambda b, g: (0, 0))],
        out_specs=pl.BlockSpec((2, 1, 1, n_strips, dh), lambda b, g: (0, g, b, 0, 0)),
        out_shape=jax.ShapeDtypeStruct((2, hkv, batch, n_strips, dh), jnp.bfloat16),
        compiler_params=pltpu.CompilerParams(
            dimension_semantics=("arbitrary", "arbitrary"), vmem_limit_bytes=VMEM_LIMIT_BYTES),
        name="cmp_kv",
    )(strips, pos, w1, w2, kc_w.reshape(1, dh))


def _flash_step(qt_ref, k, vt, m_ref, acc_ref, base, mask_fn, tq):
    tk = k.shape[0]
    dist = base + lax.broadcasted_iota(jnp.int32, (tk, tq), 1) - lax.broadcasted_iota(jnp.int32, (tk, tq), 0)
    off = jnp.where(mask_fn(dist), 0.0, MASK_OFF)
    def raw_scores(r):
        return jnp.dot(k, qt_ref[:, r * tq:(r + 1) * tq], preferred_element_type=jnp.float32)

    ahead = raw_scores(0)
    for r in range(NSA_GROUP):
        cols = slice(r * tq, (r + 1) * tq)
        s = ahead - off
        if r + 1 < NSA_GROUP:
            ahead = raw_scores(r + 1)
        m_old = m_ref[:, cols]
        m_new = jnp.maximum(m_old, jnp.max(s, axis=0, keepdims=True))
        p = jnp.exp2(s - m_new).astype(jnp.bfloat16)
        acc_ref[:, cols] = jnp.exp2(m_old - m_new) * acc_ref[:, cols] \
            + jnp.dot(vt, p, preferred_element_type=jnp.float32)
        m_ref[:, cols] = m_new


def _nsa_attn_body(q_ref, gate_ref, kc_ref, vc_ref, ks_ref, vs_ref, kw_ref, vw_ref, qw_ref, slope_ref, o_ref,
                   qt_ref, sel_ref, oc_ref, ms_ref, as_ref, mw_ref, aw_ref, *, tq, tk, n_sel):
    i = pl.program_id(2)
    j = pl.program_id(3)
    G, dh = NSA_GROUP, NSA_HEAD_DIM
    last_j = (i * tq + tq - 1) // tk

    @pl.when(j == 0)
    def _():
        xt = q_ref[...].T
        row = lax.broadcasted_iota(jnp.int32, (dh, tq), 0)
        for r in range(G):
            xr = xt[r * dh:(r + 1) * dh, :]
            inv = lax.rsqrt(jnp.mean(xr * xr, axis=0, keepdims=True) + NORM_EPS)
            qn = xr * inv * qw_ref[...] * (dh ** -0.5 * LOG2E)
            slope = slope_ref[0, r]
            s_hi = slope.astype(jnp.bfloat16).astype(jnp.float32)
            s_lo = slope - s_hi
            tail = jnp.where(row == 0, POS_BASE * s_hi, jnp.where(row == 1, POS_BASE * s_lo,
                             jnp.where(row == 2, s_hi, jnp.where(row == 3, s_lo, 0.0))))
            qt_ref[:, r * tq:(r + 1) * tq] = jnp.concatenate([qn, tail], axis=0).astype(jnp.bfloat16)
        for ref in (ms_ref, mw_ref):
            ref[...] = jnp.full(ref.shape, NEG_INF, jnp.float32)
        for ref in (as_ref, aw_ref):
            ref[...] = jnp.zeros(ref.shape, jnp.float32)

        n_cmp_pad = kc_ref.shape[-2]
        t_pos = i * tq + lax.broadcasted_iota(jnp.int32, (n_cmp_pad, tq), 1)
        blk = lax.broadcasted_iota(jnp.int32, (n_cmp_pad, tq), 0)
        dist = t_pos - (blk * CMP_STRIDE + CMP_BLOCK - 1)
        valid = (dist >= 0) & (blk < n_cmp_pad - 1)
        distf = dist.astype(jnp.float32)
        cb = lax.broadcasted_iota(jnp.int32, (n_sel, n_cmp_pad), 1) * CMP_STRIDE
        sb = lax.broadcasted_iota(jnp.int32, (n_sel, n_cmp_pad), 0) * SEL_BLOCK
        overlap_t = ((cb < sb + SEL_BLOCK) & (cb + CMP_BLOCK > sb)).astype(jnp.bfloat16)
        imp = None
        for r in range(G):
            cols = slice(r * tq, (r + 1) * tq)
            s = jnp.dot(kc_ref[0, 0, 0], qt_ref[:dh, cols], preferred_element_type=jnp.float32)
            s = jnp.where(valid, s - slope_ref[0, r] * distf, NEG_INF)
            p = jnp.where(valid, jnp.exp2(s - jnp.max(s, axis=0, keepdims=True)), 0.0)
            denom = jnp.maximum(jnp.sum(p, axis=0, keepdims=True), 1e-30)
            p = (p / denom).astype(jnp.bfloat16)
            oc_ref[:, cols] = lax.dot_general(vc_ref[0, 0, 0], p, (((0,), (0,)), ((), ())),
                                              preferred_element_type=jnp.float32)
            part = jnp.dot(overlap_t, p, preferred_element_type=jnp.float32)
            imp = part if imp is None else imp + part

        cur = (i * tq + lax.broadcasted_iota(jnp.int32, (n_sel, tq), 1)) // SEL_BLOCK
        jb = lax.broadcasted_iota(jnp.int32, (n_sel, tq), 0)
        forced = (jb == 0) | ((jb <= cur) & (jb > cur - N_LOCAL))
        imp = jnp.where(forced, FORCE_SCORE, jnp.where(jb > cur, -FORCE_SCORE, imp))
        rank = jnp.zeros((n_sel, tq), jnp.float32)
        for c in range(n_sel):
            row = imp[c:c + 1, :]
            ahead = (row > imp) | ((row == imp) & (jb > c))
            rank = rank + jnp.where(ahead, 1.0, 0.0)
        sel_t = jnp.where(rank < min(N_SELECT, n_sel), 1.0, 0.0)
        sel_t = jnp.concatenate([sel_t, jnp.zeros((sel_ref.shape[0] - n_sel, tq), jnp.float32)], axis=0)
        sel_ref[...] = sel_t

    @pl.when(j <= last_j)
    def _():
        per_tile = tk // SEL_BLOCK

        def mask(dist):
            picked = jnp.concatenate(
                [jnp.broadcast_to(sel_ref[pl.ds(j * per_tile + b, 1), :], (SEL_BLOCK, tq)) for b in range(per_tile)],
                axis=0)
            return (dist >= 0) & (picked > 0.5)

        _flash_step(qt_ref, ks_ref[0, 0], vs_ref[0, 0], ms_ref, as_ref, i * tq - j * tk, mask, tq)

    @pl.when((j <= last_j) & (j * tk + tk - 1 >= i * tq - (WINDOW - 1)))
    def _():
        _flash_step(qt_ref, kw_ref[0, 0], vw_ref[0, 0], mw_ref, aw_ref, i * tq - j * tk,
                    lambda dist: (dist >= 0) & (dist < WINDOW), tq)

    @pl.when(j == pl.num_programs(3) - 1)
    def _():
        gates = jax.nn.sigmoid(gate_ref[0])
        outs = []
        for r in range(G):
            cols = slice(r * tq, (r + 1) * tq)
            o_sel = as_ref[:dh, cols] / as_ref[dh:dh + 1, cols]
            o_win = aw_ref[:dh, cols] / aw_ref[dh:dh + 1, cols]
            outs.append(gates[3 * r:3 * r + 1] * oc_ref[:, cols] + gates[3 * r + 1:3 * r + 2] * o_sel
                        + gates[3 * r + 2:3 * r + 3] * o_win)
        o_ref[...] = jnp.concatenate(outs, axis=0).T


def _nsa_attn(proj, gates, cmp_kv, kn, vv, q_w, slopes, batch, tq=512, tk=512):
    n = proj.shape[0]
    t = n // batch
    G, dh, hkv = NSA_GROUP, NSA_HEAD_DIM, NSA_KV_HEADS
    n_cmp_pad = cmp_kv.shape[-2]
    n_sel = t // SEL_BLOCK
    q_blk0 = Q_COL0 // (G * dh)

    def kv_tile(b, i, j):
        return b * (t // tk) + jnp.minimum(j, (i * tq + tq - 1) // tk)

    def k_map(kind):
        return lambda b, g, i, j: (kind, g, kv_tile(b, i, j), 0)

    def v_map(kind):
        return lambda b, g, i, j: (kind, g, 0, kv_tile(b, i, j))

    def cmp_map(kind):
        return lambda b, g, i, j: (kind, g, b, 0, 0)

    cmp_block = (1, 1, 1, n_cmp_pad, dh)
    k_block = (1, 1, tk, 2 * dh)
    v_block = (1, 1, 2 * dh, tk)
    slab = pltpu.VMEM((dh, G * tq), jnp.float32)
    wide = pltpu.VMEM((2 * dh, G * tq), jnp.float32)
    stat = pltpu.VMEM((1, G * tq), jnp.float32)
    return pl.pallas_call(
        functools.partial(_nsa_attn_body, tq=tq, tk=tk, n_sel=n_sel),
        grid=(batch, hkv, t // tq, t // tk),
        in_specs=[
            pl.BlockSpec((tq, G * dh), lambda b, g, i, j: (b * (t // tq) + i, q_blk0 + g)),
            pl.BlockSpec((1, 3 * G, tq), lambda b, g, i, j: (g, 0, b * (t // tq) + i)),
            pl.BlockSpec(cmp_block, cmp_map(0)), pl.BlockSpec(cmp_block, cmp_map(1)),
            pl.BlockSpec(k_block, k_map(0)), pl.BlockSpec(v_block, v_map(0)),
            pl.BlockSpec(k_block, k_map(1)), pl.BlockSpec(v_block, v_map(1)),
            pl.BlockSpec((dh, 1), lambda b, g, i, j: (0, 0)),
            pl.BlockSpec((1, G, 1, 1), lambda b, g, i, j: (g, 0, 0, 0)),
        ],
        out_specs=pl.BlockSpec((tq, G * dh), lambda b, g, i, j: (b * (t // tq) + i, g)),
        out_shape=jax.ShapeDtypeStruct((n, hkv * G * dh), jnp.float32),
        scratch_shapes=[pltpu.VMEM((2 * dh, G * tq), jnp.bfloat16), pltpu.VMEM((LANES, tq), jnp.float32),
                        slab, stat, wide, stat, wide],
        compiler_params=pltpu.CompilerParams(
            dimension_semantics=("arbitrary",) * 4, vmem_limit_bytes=VMEM_LIMIT_BYTES),
        name="nsa_attn",
    )(proj, gates, cmp_kv, cmp_kv, kn, vv, kn, vv, q_w.reshape(dh, 1), slopes)


def _nsa(proj, batch, q_norm_w, kc_norm_w, ks_norm_w, kw_norm_w, pos_k, pos_v, w_ck1, w_ck2, w_cv1, w_cv2):
    n = proj.shape[0]
    G, hkv = NSA_GROUP, NSA_KV_HEADS
    kn, vv, cf = _kv_prep(proj, ks_norm_w, kw_norm_w, 512, n // batch)
    half = CMP_STRIDE * NSA_HEAD_DIM
    pos = jnp.stack([pos_k.reshape(2, half), pos_v.reshape(2, half)])
    w1 = jnp.stack([w_ck1, w_cv1]).astype(jnp.bfloat16)
    w2 = jnp.stack([w_ck2, w_cv2]).astype(jnp.bfloat16)
    cmp_kv = _cmp_kv(cf, pos, w1, w2, kc_norm_w, batch)
    gates = proj[:, GATE_COL0:GATE_COL0 + N_GATES].reshape(n, hkv, 3 * G).transpose(1, 2, 0)
    slopes = jnp.asarray(2.0 ** (-8.0 * np.arange(1, NSA_Q_HEADS + 1) / NSA_Q_HEADS), jnp.float32) * LOG2E
    return _nsa_attn(proj, gates, cmp_kv, kn, vv, q_norm_w, slopes.reshape(hkv, G, 1, 1), batch)


BIG_NEG = -3.0e38
PEER_CAND = [(a, b) for a in range(PEER_TOPK) for b in range(PEER_TOPK) if (a + 1) * (b + 1) <= PEER_TOPK]
PEER_CAND_ROWS = -(-len(PEER_CAND) // 8) * 8


def _peer_route_body(h_ref, g_ref, wq_ref, keys_ref, xt_ref, rk2_ref, a2_ref, cnt_ref, r_ref,
                     qt_ref, sc_ref, cur_ref, top_ref, cand_ref):
    x = h_ref[...]
    hn = x * lax.rsqrt(jnp.mean(x * x, axis=-1, keepdims=True) + NORM_EPS) * g_ref[...]
    hnt = hn.T.astype(jnp.bfloat16)
    xt_ref[...] = hnt
    qt_ref[...] = jnp.dot(wq_ref[...], hnt, preferred_element_type=jnp.float32).astype(jnp.bfloat16)
    cand_ref[...] = jnp.full(cand_ref.shape, BIG_NEG, jnp.float32)
    kd = PEER_KEY_DIM // 2
    n_parts = 2 * PEER_HEADS
    for hp in range(n_parts):
        s = jnp.dot(keys_ref[hp], qt_ref[hp * kd:(hp + 1) * kd, :], preferred_element_type=jnp.float32)
        sc_ref[hp] = s
        cur_ref[hp] = s

    def extract(k, carry):
        cur = cur_ref[...]
        mk = jnp.max(cur, axis=1, keepdims=True)
        top_ref[k] = mk
        cur_ref[...] = jnp.where(cur == mk, BIG_NEG, cur)
        return carry

    lax.fori_loop(0, PEER_TOPK, extract, 0)

    def head(hd, carry):
        t1 = [top_ref[a, 2 * hd] for a in range(PEER_TOPK)]
        t2 = [top_ref[b, 2 * hd + 1] for b in range(PEER_TOPK)]
        cmax = t1[0] + t2[0]
        cands = [t1[a] + t2[b] for a, b in PEER_CAND]
        for i, c in enumerate(cands):
            cand_ref[i:i + 1, :] = c
        call = cand_ref[...]
        n_gt = jnp.zeros(call.shape, jnp.float32)
        for c in cands:
            n_gt = n_gt + jnp.where(c > call, 1.0, 0.0)
        tau = jnp.min(jnp.where(n_gt <= PEER_TOPK - 1, call, -BIG_NEG), axis=0, keepdims=True)
        z = jnp.sum(jnp.where(call >= tau, jnp.exp(call - cmax), 0.0), axis=0, keepdims=True)
        count = [None] * PEER_TOPK
        for (a, b), c in zip(PEER_CAND, cands):
            hit = jnp.where(c >= tau, 1.0, 0.0)
            count[a] = hit if count[a] is None else count[a] + hit
        s1 = sc_ref[2 * hd]
        s2 = sc_ref[2 * hd + 1]
        cnt = jnp.zeros(s1.shape, jnp.float32)
        rank2 = jnp.zeros(s2.shape, jnp.float32)
        for a in range(PEER_TOPK):
            cnt = jnp.where(s1 == t1[a], count[a], cnt)
            rank2 = rank2 + jnp.where(t2[a] > s2, 1.0, 0.0)
        rk2_ref[hd] = rank2.astype(jnp.bfloat16)
        a2_ref[hd] = jnp.exp(s2 - t2[0]).astype(jnp.bfloat16)
        cnt_ref[hd] = cnt
        r_ref[hd] = jnp.exp(s1 - t1[0]) / z
        return carry

    lax.fori_loop(0, PEER_HEADS, head, 0)


def _peer_route(h, gain, wq_t, keys, tt):
    n, d = h.shape
    hp, nk, kd = keys.shape
    stat = jax.ShapeDtypeStruct((PEER_HEADS, nk, n), jnp.float32)
    stat_bf16 = jax.ShapeDtypeStruct((PEER_HEADS, nk, n), jnp.bfloat16)
    stat_spec = pl.BlockSpec((PEER_HEADS, nk, tt), lambda i: (0, 0, i))
    return pl.pallas_call(
        _peer_route_body,
        grid=(n // tt,),
        in_specs=[
            pl.BlockSpec((tt, d), lambda i: (i, 0)),
            pl.BlockSpec((1, d), lambda i: (0, 0)),
            pl.BlockSpec(wq_t.shape, lambda i: (0, 0)),
            pl.BlockSpec(keys.shape, lambda i: (0, 0, 0)),
        ],
        out_specs=[pl.BlockSpec((d, tt), lambda i: (0, i)), stat_spec, stat_spec, stat_spec, stat_spec],
        out_shape=[jax.ShapeDtypeStruct((d, n), jnp.bfloat16), stat_bf16, stat_bf16, stat, stat],
        scratch_shapes=[pltpu.VMEM((wq_t.shape[0], tt), jnp.bfloat16),
                        pltpu.VMEM((hp, nk, tt), jnp.float32), pltpu.VMEM((hp, nk, tt), jnp.float32),
                        pltpu.VMEM((PEER_TOPK, hp, 1, tt), jnp.float32),
                        pltpu.VMEM((PEER_CAND_ROWS, tt), jnp.float32)],
        compiler_params=pltpu.CompilerParams(
            dimension_semantics=("arbitrary",), vmem_limit_bytes=VMEM_LIMIT_BYTES),
        name="peer_route",
    )(h, gain.reshape(1, d), wq_t, keys)


PEER_I1_PER_PIECE = 2


def _peer_experts_body(xt_ref, u_ref, vt_ref, rk2_ref, a2_ref, cnt_ref, r_ref, o_ref, w_ref, *, n_i1):
    @pl.when(pl.program_id(1) == 0)
    def _():
        o_ref[...] = jnp.zeros(o_ref.shape, jnp.float32)

    tt = xt_ref.shape[1]
    piece = PEER_I1_PER_PIECE * N_KEYS
    n_piece = n_i1 // PEER_I1_PER_PIECE
    parts = []
    bump = None
    for k in range(n_piece + 1):
        if k < n_piece:
            for i1 in range(k * PEER_I1_PER_PIECE, (k + 1) * PEER_I1_PER_PIECE):
                w = None
                for hd in range(PEER_HEADS):
                    picked = rk2_ref[hd] < cnt_ref[hd, i1:i1 + 1, :].astype(jnp.bfloat16)
                    term = jnp.where(picked, a2_ref[hd], 0.0) * r_ref[hd, i1:i1 + 1, :].astype(jnp.bfloat16)
                    w = term if w is None else w + term
                w_ref[i1 * N_KEYS:(i1 + 1) * N_KEYS, :] = w
            last = w[N_KEYS - 16:, tt - LANES:].astype(jnp.float32)
            new_bump = jnp.where(jnp.max(last) > 1e30, 1, 0)
        if k >= 1:
            start = pl.multiple_of((k - 1 + bump) * piece, piece)
            x = jnp.dot(u_ref[pl.ds(start, piece), :], xt_ref[...], preferred_element_type=jnp.float32)
            parts.append(w_ref[(k - 1) * piece:k * piece, :] * _gelu(x).astype(jnp.bfloat16))
        bump = new_bump
    p = jnp.concatenate(parts, axis=0)
    o_ref[...] += jnp.dot(vt_ref[...], p, preferred_element_type=jnp.float32)


def _peer_experts(xt, u, vt, rk2, a2, cnt, r, tt, te):
    d, n = xt.shape
    e = u.shape[0]
    n_i1 = te // N_KEYS
    full = pl.BlockSpec((PEER_HEADS, N_KEYS, tt), lambda i, j: (0, 0, i))
    part = pl.BlockSpec((PEER_HEADS, n_i1, tt), lambda i, j: (0, j, i))
    return pl.pallas_call(
        functools.partial(_peer_experts_body, n_i1=n_i1),
        grid=(n // tt, e // te),
        in_specs=[
            pl.BlockSpec((d, tt), lambda i, j: (0, i)),
            pl.BlockSpec((te, d), lambda i, j: (j, 0)),
            pl.BlockSpec((d, te), lambda i, j: (0, j)),
            full, full, part, part,
        ],
        out_specs=pl.BlockSpec((d, tt), lambda i, j: (0, i)),
        out_shape=jax.ShapeDtypeStruct((d, n), jnp.float32),
        scratch_shapes=[pltpu.VMEM((te, tt), jnp.bfloat16)],
        compiler_params=pltpu.CompilerParams(
            dimension_semantics=("arbitrary", "arbitrary"), vmem_limit_bytes=PEER_VMEM_LIMIT_BYTES),
        name="peer_experts",
    )(xt, u, vt, rk2, a2, cnt, r)


def _transpose_cast_body(x_ref, o_ref):
    o_ref[...] = x_ref[...].T.astype(o_ref.dtype)


def _transpose_cast(x, blk, dtype):
    r, c = x.shape
    return pl.pallas_call(
        _transpose_cast_body,
        grid=(r // blk, c // blk),
        in_specs=[pl.BlockSpec((blk, blk), lambda i, j: (i, j))],
        out_specs=pl.BlockSpec((blk, blk), lambda i, j: (j, i)),
        out_shape=jax.ShapeDtypeStruct((c, r), dtype),
        compiler_params=pltpu.CompilerParams(
            dimension_semantics=("arbitrary", "arbitrary"), vmem_limit_bytes=VMEM_LIMIT_BYTES),
        name="transpose_cast",
    )(x)


def _transpose_add_body(x_ref, r_ref, o_ref):
    o_ref[...] = r_ref[...] + x_ref[...].T


def _transpose_add(x_t, res, blk):
    d, n = x_t.shape
    return pl.pallas_call(
        _transpose_add_body,
        grid=(d // blk, n // blk),
        in_specs=[pl.BlockSpec((blk, blk), lambda i, j: (i, j)),
                  pl.BlockSpec((blk, blk), lambda i, j: (j, i))],
        out_specs=pl.BlockSpec((blk, blk), lambda i, j: (j, i)),
        out_shape=jax.ShapeDtypeStruct((n, d), jnp.float32),
        compiler_params=pltpu.CompilerParams(
            dimension_semantics=("arbitrary", "arbitrary"), vmem_limit_bytes=VMEM_LIMIT_BYTES),
        name="transpose_add",
    )(x_t, res)


def _peer(h, gain, w_q, sub_keys, u_tab, v_tab):
    keys = sub_keys.reshape(PEER_HEADS * 2, N_KEYS, PEER_KEY_DIM // 2).astype(jnp.bfloat16)
    xt, rk2, a2, cnt, r = _peer_route(h, gain, _transpose_cast(w_q, 1024, jnp.bfloat16), keys, 256)
    out_t = _peer_experts(xt, u_tab.astype(jnp.bfloat16), _transpose_cast(v_tab, 1024, jnp.bfloat16),
                          rk2, a2, cnt, r, 512, 2048)
    return _transpose_add(out_t, h, min(1024, h.shape[0]))


def kernel(x, norm1_w, w_in, hg_lb_logits, hg_norm_w, q_norm_w, kc_norm_w, ks_norm_w, kw_norm_w,
           cmp_pos_k, cmp_pos_v, w_ck1, w_ck2, w_cv1, w_cv2, w_out, norm2_w,
           peer_w_q, peer_sub_keys, peer_u, peer_v):
    B, T, D = x.shape
    n = B * T
    layer = 0
    lower_bounds = jnp.cumsum(jax.nn.softmax(hg_lb_logits, axis=0), axis=0)
    xt = x.reshape(n, D)

    w_in_b = jnp.pad(w_in[layer].astype(jnp.bfloat16), ((0, 0), (0, IN_COLS_PADDED - IN_COLS)))
    proj = _norm_matmul(xt, norm1_w[layer], w_in_b, 1024, 1024)
    hg_out = _hgrn2(proj, lower_bounds[layer], hg_norm_w[layer], B)
    nsa_out = _nsa(proj, B, q_norm_w[layer], kc_norm_w[layer], ks_norm_w[layer], kw_norm_w[layer],
                   cmp_pos_k[layer], cmp_pos_v[layer], w_ck1[layer], w_ck2[layer], w_cv1[layer], w_cv2[layer])
    h = _out_proj(hg_out, nsa_out, w_out[layer].astype(jnp.bfloat16), xt, 1024, 1024)

    y = _peer(h, norm2_w[layer], peer_w_q[layer], peer_sub_keys[layer], peer_u[layer], peer_v[layer])
    return y.reshape(B, T, D)
```

```python
import functools

import jax
import jax.numpy as jnp
import numpy as np
from jax import lax
from jax.experimental import pallas as pl
from jax.experimental.pallas import tpu as pltpu

D_MODEL = 2048
HG_WIDTH = 1024
HG_HEAD_DIM = 128
HG_HEADS = 8
HG_CHUNK = 64
NSA_WIDTH = 1024
NSA_HEAD_DIM = 64
NSA_Q_HEADS = 16
NSA_KV_HEADS = 4
NSA_GROUP = 4
CMP_BLOCK = 32
CMP_STRIDE = 16
CMP_HIDDEN = 256
SEL_BLOCK = 64
N_SELECT = 16
N_LOCAL = 2
WINDOW = 512
FORCE_SCORE = 1e9
NEG_INF = -1e30
MASK_OFF = 3e30
LOG2E = float(np.log2(np.e))
PEER_HEADS = 8
N_KEYS = 128
PEER_KEY_DIM = 256
PEER_TOPK = 16
NORM_EPS = 1e-6
KV_W = NSA_KV_HEADS * NSA_HEAD_DIM
N_GATES = 3 * NSA_Q_HEADS
IN_SIZES = [HG_WIDTH] * 4 + [NSA_WIDTH] + [KV_W] * 6 + [N_GATES]
IN_COLS = sum(IN_SIZES)
IN_COLS_PADDED = 7168
Q_COL0 = 4 * HG_WIDTH
KV_COL0 = Q_COL0 + NSA_WIDTH
GATE_COL0 = KV_COL0 + 6 * KV_W

VMEM_LIMIT_BYTES = 48 * 1024 * 1024
LANES = 128
PEER_VMEM_LIMIT_BYTES = 60 * 1024 * 1024


def _norm_matmul_body(x_ref, g_ref, w_ref, o_ref, xn_ref):
    @pl.when(pl.program_id(1) == 0)
    def _():
        x = x_ref[...]
        r = lax.rsqrt(jnp.mean(x * x, axis=-1, keepdims=True) + NORM_EPS)
        xn_ref[...] = (x * r * g_ref[...]).astype(jnp.bfloat16)

    o_ref[...] = jnp.dot(xn_ref[...], w_ref[...], preferred_element_type=jnp.float32).astype(o_ref.dtype)


def _norm_matmul(x, gain, w, tm, tn, out_dtype=jnp.float32):
    m, k = x.shape
    n = w.shape[1]
    return pl.pallas_call(
        _norm_matmul_body,
        grid=(m // tm, n // tn),
        in_specs=[
            pl.BlockSpec((tm, k), lambda i, j: (i, 0)),
            pl.BlockSpec((1, k), lambda i, j: (0, 0)),
            pl.BlockSpec((k, tn), lambda i, j: (0, j)),
        ],
        out_specs=pl.BlockSpec((tm, tn), lambda i, j: (i, j)),
        out_shape=jax.ShapeDtypeStruct((m, n), out_dtype),
        scratch_shapes=[pltpu.VMEM((tm, k), jnp.bfloat16)],
        compiler_params=pltpu.CompilerParams(
            dimension_semantics=("arbitrary", "arbitrary"), vmem_limit_bytes=VMEM_LIMIT_BYTES),
        name="norm_matmul",
    )(x, gain.reshape(1, k), w)


def _out_proj_body(a1_ref, a2_ref, w1_ref, w2_ref, r_ref, o_ref):
    acc = jnp.dot(a1_ref[...].astype(jnp.bfloat16), w1_ref[...], preferred_element_type=jnp.float32)
    acc = acc + jnp.dot(a2_ref[...].astype(jnp.bfloat16), w2_ref[...], preferred_element_type=jnp.float32)
    o_ref[...] = r_ref[...] + acc


def _out_proj(a1, a2, w, res, tm, tn):
    m, k1 = a1.shape
    k2 = a2.shape[1]
    n = w.shape[1]
    return pl.pallas_call(
        _out_proj_body,
        grid=(m // tm, n // tn),
        in_specs=[
            pl.BlockSpec((tm, k1), lambda i, j: (i, 0)),
            pl.BlockSpec((tm, k2), lambda i, j: (i, 0)),
            pl.BlockSpec((k1, tn), lambda i, j: (0, j)),
            pl.BlockSpec((k2, tn), lambda i, j: (k1 // k2, j)),
            pl.BlockSpec((tm, tn), lambda i, j: (i, j)),
        ],
        out_specs=pl.BlockSpec((tm, tn), lambda i, j: (i, j)),
        out_shape=jax.ShapeDtypeStruct((m, n), jnp.float32),
        compiler_params=pltpu.CompilerParams(
            dimension_semantics=("arbitrary", "arbitrary"), vmem_limit_bytes=VMEM_LIMIT_BYTES),
        name="out_proj",
    )(a1, a2, w, w, res)


def _hgrn2_body(q_ref, f_ref, v_ref, g_ref, lb_ref, nw_ref, o_ref, qd_ref, kd_ref, ku_ref, vb_ref, dec_ref, sp_ref):
    t, dk = q_ref.shape
    C = HG_CHUNK
    nc = t // C
    lb = lb_ref[...]
    f = lb + (1.0 - lb) * jax.nn.sigmoid(f_ref[...])
    kf = 1.0 - f
    b = jnp.log(f)
    row = lax.broadcasted_iota(jnp.int32, (t, dk), 0) % C
    shift = 1
    while shift < C:
        b = b + jnp.where(row >= shift, pltpu.roll(b, shift, 0), 0.0)
        shift *= 2
    b3 = b.reshape(nc, C, dk)
    b_end = b3[:, C - 1:C, :]
    qd_ref[...] = (q_ref[...] * jnp.exp(b)).astype(jnp.bfloat16)
    kd_ref[...] = (kf * jnp.exp(-b)).astype(jnp.bfloat16)
    ku_ref[...] = (kf.reshape(nc, C, dk) * jnp.exp(b_end - b3)).reshape(t, dk).astype(jnp.bfloat16)
    vb_ref[...] = v_ref[...].astype(jnp.bfloat16)
    dec_ref[...] = jnp.exp(b_end)

    st = jnp.zeros((dk, dk), jnp.float32)
    for n in range(nc):
        rows = slice(n * C, (n + 1) * C)
        sp_ref[n] = st.astype(jnp.bfloat16)
        upd_t = lax.dot_general(vb_ref[rows, :], ku_ref[rows, :], (((0,), (0,)), ((), ())),
                                preferred_element_type=jnp.float32)
        st = dec_ref[n] * st + upd_t

    causal = lax.broadcasted_iota(jnp.int32, (C, C), 0) >= lax.broadcasted_iota(jnp.int32, (C, C), 1)
    for n in range(nc):
        rows = slice(n * C, (n + 1) * C)
        qd = qd_ref[rows, :]
        attn = lax.dot_general(qd, kd_ref[rows, :], (((1,), (1,)), ((), ())), preferred_element_type=jnp.float32)
        attn = jnp.where(causal, attn, 0.0).astype(jnp.bfloat16)
        o = jnp.dot(attn, vb_ref[rows, :], preferred_element_type=jnp.float32)
        o = o + lax.dot_general(qd, sp_ref[n], (((1,), (1,)), ((), ())), preferred_element_type=jnp.float32)
        o = o * lax.rsqrt(jnp.mean(o * o, axis=-1, keepdims=True) + NORM_EPS) * nw_ref[...]
        o_ref[rows, :] = o * jax.nn.silu(g_ref[rows, :])


def _hgrn2(proj, lb, norm_w, batch):
    n = proj.shape[0]
    t = n // batch
    dk, H = HG_HEAD_DIM, HG_HEADS
    nc = t // HG_CHUNK

    def part(p):
        return pl.BlockSpec((t, dk), lambda b, h: (b, p * H + h))

    slab = pltpu.VMEM((t, dk), jnp.bfloat16)
    return pl.pallas_call(
        _hgrn2_body,
        grid=(batch, H),
        in_specs=[part(0), part(1), part(2), part(3),
                  pl.BlockSpec((1, dk), lambda b, h: (0, h)),
                  pl.BlockSpec((1, dk), lambda b, h: (0, 0))],
        out_specs=pl.BlockSpec((t, dk), lambda b, h: (b, h)),
        out_shape=jax.ShapeDtypeStruct((n, H * dk), jnp.float32),
        scratch_shapes=[slab, slab, slab, slab, pltpu.VMEM((nc, 1, dk), jnp.float32),
                        pltpu.VMEM((nc, dk, dk), jnp.bfloat16)],
        compiler_params=pltpu.CompilerParams(
            dimension_semantics=("arbitrary", "arbitrary"), vmem_limit_bytes=VMEM_LIMIT_BYTES),
        name="hgrn2",
    )(proj, proj, proj, proj, lb.reshape(1, H * dk), norm_w.reshape(1, dk))


GELU_C0 = float(np.sqrt(2.0 / np.pi))
GELU_C1 = GELU_C0 * 0.044715


def _gelu(x):
    half = 0.5 * x
    return half + half * jnp.tanh(x * (GELU_C0 + GELU_C1 * (x * x)))


def _head_norm(x, w):
    return x * lax.rsqrt(jnp.mean(x * x, axis=-1, keepdims=True) + NORM_EPS) * w


def _value_slab_t(v):
    lane = lax.broadcasted_iota(jnp.int32, v.shape, 1)
    return jnp.concatenate([v, jnp.where(lane == 0, 1.0, 0.0)], axis=-1).T.astype(jnp.bfloat16)


POS_BASE = 64


def _key_slab(k, pos):
    lane = lax.broadcasted_iota(jnp.int32, k.shape, 1)
    hi = (pos // POS_BASE).astype(jnp.float32)
    lo = (pos % POS_BASE).astype(jnp.float32)
    tail = jnp.where(lane < 2, hi, jnp.where(lane < 4, lo, 0.0))
    return jnp.concatenate([k, tail], axis=-1).astype(jnp.bfloat16)


def _kv_prep_body(c_ref, s_ref, w_ref, ksw_ref, kww_ref, kn_ref, vv_ref, cf_ref, *, seq):
    dh = NSA_HEAD_DIM
    tm = c_ref.shape[0]
    pos = (pl.program_id(0) * tm) % seq + lax.broadcasted_iota(jnp.int32, (tm, dh), 0)
    for h in range(NSA_KV_HEADS):
        k_cols = slice(h * dh, (h + 1) * dh)
        v_cols = slice(KV_W + h * dh, KV_W + (h + 1) * dh)
        cf_ref[0, h] = c_ref[:, k_cols]
        cf_ref[1, h] = c_ref[:, v_cols]
        kn_ref[0, h] = _key_slab(_head_norm(s_ref[:, k_cols], ksw_ref[...]), pos)
        vv_ref[0, h] = _value_slab_t(s_ref[:, v_cols])
        kn_ref[1, h] = _key_slab(_head_norm(w_ref[:, k_cols], kww_ref[...]), pos)
        vv_ref[1, h] = _value_slab_t(w_ref[:, v_cols])


def _kv_prep(proj, ks_w, kw_w, tm, seq):
    n = proj.shape[0]
    dh = NSA_HEAD_DIM
    pair = 2 * KV_W
    col0 = KV_COL0 // pair
    head_block = pl.BlockSpec((2, NSA_KV_HEADS, tm, dh), lambda i: (0, 0, i, 0))
    wide_block = pl.BlockSpec((2, NSA_KV_HEADS, tm, 2 * dh), lambda i: (0, 0, i, 0))
    w_spec = pl.BlockSpec((1, dh), lambda i: (0, 0))
    return pl.pallas_call(
        functools.partial(_kv_prep_body, seq=seq),
        grid=(n // tm,),
        in_specs=[pl.BlockSpec((tm, pair), lambda i: (i, col0)),
                  pl.BlockSpec((tm, pair), lambda i: (i, col0 + 1)),
                  pl.BlockSpec((tm, pair), lambda i: (i, col0 + 2)),
                  w_spec, w_spec],
        out_specs=[wide_block, pl.BlockSpec((2, NSA_KV_HEADS, 2 * dh, tm), lambda i: (0, 0, 0, i)), head_block],
        out_shape=[jax.ShapeDtypeStruct((2, NSA_KV_HEADS, n, 2 * dh), jnp.bfloat16),
                   jax.ShapeDtypeStruct((2, NSA_KV_HEADS, 2 * dh, n), jnp.bfloat16),
                   jax.ShapeDtypeStruct((2, NSA_KV_HEADS, n, dh), jnp.float32)],
        compiler_params=pltpu.CompilerParams(
            dimension_semantics=("arbitrary",), vmem_limit_bytes=VMEM_LIMIT_BYTES),
        name="kv_prep",
    )(proj, proj, proj, ks_w.reshape(1, dh), kw_w.reshape(1, dh))


def _cmp_kv_body(r_ref, pos_ref, w1_ref, w2_ref, nw_ref, o_ref):
    half = CMP_STRIDE * NSA_HEAD_DIM
    n_strips = r_ref.shape[2]
    for kind in range(2):
        strips = r_ref[kind, 0]
        top = (strips + pos_ref[kind, 0:1, :]).astype(jnp.bfloat16)
        bot = (strips + pos_ref[kind, 1:2, :]).astype(jnp.bfloat16)
        a = jnp.dot(top, w1_ref[kind, :half, :], preferred_element_type=jnp.float32)
        b = jnp.dot(bot, w1_ref[kind, half:, :], preferred_element_type=jnp.float32)
        hid = _gelu(a + pltpu.roll(b, n_strips - 1, 0))
        out = jnp.dot(hid.astype(jnp.bfloat16), w2_ref[kind], preferred_element_type=jnp.float32)
        if kind == 0:
            out = _head_norm(out, nw_ref[...])
        o_ref[kind, 0, 0] = out.astype(jnp.bfloat16)


def _cmp_kv(cf, pos, w1, w2, kc_w, batch):
    _, hkv, n, dh = cf.shape
    t = n // batch
    n_strips = t // CMP_STRIDE
    strips = cf.reshape(2, hkv, n // CMP_STRIDE, CMP_STRIDE * dh)
    return pl.pallas_call(
        _cmp_kv_body,
        grid=(batch, hkv),
        in_specs=[pl.BlockSpec((2, 1, n_strips, CMP_STRIDE * dh), lambda b, g: (0, g, b, 0)),
                  pl.BlockSpec(pos.shape, lambda b, g: (0, 0, 0)),
                  pl.BlockSpec(w1.shape, lambda b, g: (0, 0, 0)),
                  pl.BlockSpec(w2.shape, lambda b, g: (0, 0, 0)),
                  pl.BlockSpec((1, dh), lambda b, g: (0, 0))],
        out_specs=pl.BlockSpec((2, 1, 1, n_strips, dh), lambda b, g: (0, g, b, 0, 0)),
        out_shape=jax.ShapeDtypeStruct((2, hkv, batch, n_strips, dh), jnp.bfloat16),
        compiler_params=pltpu.CompilerParams(
            dimension_semantics=("arbitrary", "arbitrary"), vmem_limit_bytes=VMEM_LIMIT_BYTES),
        name="cmp_kv",
    )(strips, pos, w1, w2, kc_w.reshape(1, dh))


def _flash_step(qt_ref, k, vt, m_ref, acc_ref, base, mask_fn, tq):
    tk = k.shape[0]
    dist = base + lax.broadcasted_iota(jnp.int32, (tk, tq), 1) - lax.broadcasted_iota(jnp.int32, (tk, tq), 0)
    off = jnp.where(mask_fn(dist), 0.0, MASK_OFF)
    def raw_scores(r):
        return jnp.dot(k, qt_ref[:, r * tq:(r + 1) * tq], preferred_element_type=jnp.float32)

    ahead = raw_scores(0)
    for r in range(NSA_GROUP):
        cols = slice(r * tq, (r + 1) * tq)
        s = ahead - off
        if r + 1 < NSA_GROUP:
            ahead = raw_scores(r + 1)
        m_old = m_ref[:, cols]
        m_new = jnp.maximum(m_old, jnp.max(s, axis=0, keepdims=True))
        p = jnp.exp2(s - m_new).astype(jnp.bfloat16)
        acc_ref[:, cols] = jnp.exp2(m_old - m_new) * acc_ref[:, cols] \
            + jnp.dot(vt, p, preferred_element_type=jnp.float32)
        m_ref[:, cols] = m_new


def _nsa_attn_body(qi_ref, kj_ref, q_ref, gate_ref, kc_ref, vc_ref, ks_ref, vs_ref, kw_ref, vw_ref, qw_ref, slope_ref, o_ref,
                   qt_ref, sel_ref, oc_ref, ms_ref, as_ref, mw_ref, aw_ref, *, tq, tk, n_sel):
    step = pl.program_id(2)
    i = qi_ref[step]
    j = kj_ref[step]
    G, dh = NSA_GROUP, NSA_HEAD_DIM
    last_j = (i * tq + tq - 1) // tk

    @pl.when(j == 0)
    def _():
        xt = q_ref[...].T
        row = lax.broadcasted_iota(jnp.int32, (dh, tq), 0)
        for r in range(G):
            xr = xt[r * dh:(r + 1) * dh, :]
            inv = lax.rsqrt(jnp.mean(xr * xr, axis=0, keepdims=True) + NORM_EPS)
            qn = xr * inv * qw_ref[...] * (dh ** -0.5 * LOG2E)
            slope = slope_ref[0, r]
            s_hi = slope.astype(jnp.bfloat16).astype(jnp.float32)
            s_lo = slope - s_hi
            tail = jnp.where(row == 0, POS_BASE * s_hi, jnp.where(row == 1, POS_BASE * s_lo,
                             jnp.where(row == 2, s_hi, jnp.where(row == 3, s_lo, 0.0))))
            qt_ref[:, r * tq:(r + 1) * tq] = jnp.concatenate([qn, tail], axis=0).astype(jnp.bfloat16)
        for ref in (ms_ref, mw_ref):
            ref[...] = jnp.full(ref.shape, NEG_INF, jnp.float32)
        for ref in (as_ref, aw_ref):
            ref[...] = jnp.zeros(ref.shape, jnp.float32)

        n_cmp_pad = kc_ref.shape[-2]
        t_pos = i * tq + lax.broadcasted_iota(jnp.int32, (n_cmp_pad, tq), 1)
        blk = lax.broadcasted_iota(jnp.int32, (n_cmp_pad, tq), 0)
        dist = t_pos - (blk * CMP_STRIDE + CMP_BLOCK - 1)
        valid = (dist >= 0) & (blk < n_cmp_pad - 1)
        distf = dist.astype(jnp.float32)
        cb = lax.broadcasted_iota(jnp.int32, (n_sel, n_cmp_pad), 1) * CMP_STRIDE
        sb = lax.broadcasted_iota(jnp.int32, (n_sel, n_cmp_pad), 0) * SEL_BLOCK
        overlap_t = ((cb < sb + SEL_BLOCK) & (cb + CMP_BLOCK > sb)).astype(jnp.bfloat16)
        imp = None
        for r in range(G):
            cols = slice(r * tq, (r + 1) * tq)
            s = jnp.dot(kc_ref[0, 0, 0], qt_ref[:dh, cols], preferred_element_type=jnp.float32)
            s = jnp.where(valid, s - slope_ref[0, r] * distf, NEG_INF)
            p = jnp.where(valid, jnp.exp2(s - jnp.max(s, axis=0, keepdims=True)), 0.0)
            denom = jnp.maximum(jnp.sum(p, axis=0, keepdims=True), 1e-30)
            p = (p / denom).astype(jnp.bfloat16)
            oc_ref[:, cols] = lax.dot_general(vc_ref[0, 0, 0], p, (((0,), (0,)), ((), ())),
                                              preferred_element_type=jnp.float32)
            part = jnp.dot(overlap_t, p, preferred_element_type=jnp.float32)
            imp = part if imp is None else imp + part

        cur = (i * tq + lax.broadcasted_iota(jnp.int32, (n_sel, tq), 1)) // SEL_BLOCK
        jb = lax.broadcasted_iota(jnp.int32, (n_sel, tq), 0)
        forced = (jb == 0) | ((jb <= cur) & (jb > cur - N_LOCAL))
        imp = jnp.where(forced, FORCE_SCORE, jnp.where(jb > cur, -FORCE_SCORE, imp))
        rank = jnp.zeros((n_sel, tq), jnp.float32)
        for c in range(n_sel):
            row = imp[c:c + 1, :]
            ahead = (row > imp) | ((row == imp) & (jb > c))
            rank = rank + jnp.where(ahead, 1.0, 0.0)
        sel_t = jnp.where(rank < min(N_SELECT, n_sel), 1.0, 0.0)
        sel_t = jnp.concatenate([sel_t, jnp.zeros((sel_ref.shape[0] - n_sel, tq), jnp.float32)], axis=0)
        sel_ref[...] = sel_t

    @pl.when(j <= last_j)
    def _():
        per_tile = tk // SEL_BLOCK

        def mask(dist):
            picked = jnp.concatenate(
                [jnp.broadcast_to(sel_ref[pl.ds(j * per_tile + b, 1), :], (SEL_BLOCK, tq)) for b in range(per_tile)],
                axis=0)
            return (dist >= 0) & (picked > 0.5)

        _flash_step(qt_ref, ks_ref[0, 0], vs_ref[0, 0], ms_ref, as_ref, i * tq - j * tk, mask, tq)

    @pl.when((j <= last_j) & (j * tk + tk - 1 >= i * tq - (WINDOW - 1)))
    def _():
        _flash_step(qt_ref, kw_ref[0, 0], vw_ref[0, 0], mw_ref, aw_ref, i * tq - j * tk,
                    lambda dist: (dist >= 0) & (dist < WINDOW), tq)

    @pl.when(j == last_j)
    def _():
        gates = jax.nn.sigmoid(gate_ref[0])
        outs = []
        for r in range(G):
            cols = slice(r * tq, (r + 1) * tq)
            o_sel = as_ref[:dh, cols] / as_ref[dh:dh + 1, cols]
            o_win = aw_ref[:dh, cols] / aw_ref[dh:dh + 1, cols]
            outs.append(gates[3 * r:3 * r + 1] * oc_ref[:, cols] + gates[3 * r + 1:3 * r + 2] * o_sel
                        + gates[3 * r + 2:3 * r + 3] * o_win)
        o_ref[...] = jnp.concatenate(outs, axis=0).T


def _nsa_attn(proj, gates, cmp_kv, kn, vv, q_w, slopes, batch, tq=512, tk=512):
    n = proj.shape[0]
    t = n // batch
    G, dh, hkv = NSA_GROUP, NSA_HEAD_DIM, NSA_KV_HEADS
    n_cmp_pad = cmp_kv.shape[-2]
    n_sel = t // SEL_BLOCK
    q_blk0 = Q_COL0 // (G * dh)

    pairs = [(i, j) for i in range(t // tq) for j in range((i * tq + tq - 1) // tk + 1)]
    qi = jnp.asarray([p[0] for p in pairs], jnp.int32)
    kj = jnp.asarray([p[1] for p in pairs], jnp.int32)

    def q_tile(b, s, qi_ref):
        return b * (t // tq) + qi_ref[s]

    def k_map(kind):
        return lambda b, g, s, qi_ref, kj_ref: (kind, g, b * (t // tk) + kj_ref[s], 0)

    def v_map(kind):
        return lambda b, g, s, qi_ref, kj_ref: (kind, g, 0, b * (t // tk) + kj_ref[s])

    def cmp_map(kind):
        return lambda b, g, s, qi_ref, kj_ref: (kind, g, b, 0, 0)

    cmp_block = (1, 1, 1, n_cmp_pad, dh)
    k_block = (1, 1, tk, 2 * dh)
    v_block = (1, 1, 2 * dh, tk)
    slab = pltpu.VMEM((dh, G * tq), jnp.float32)
    wide = pltpu.VMEM((2 * dh, G * tq), jnp.float32)
    stat = pltpu.VMEM((1, G * tq), jnp.float32)
    return pl.pallas_call(
        functools.partial(_nsa_attn_body, tq=tq, tk=tk, n_sel=n_sel),
        grid_spec=pltpu.PrefetchScalarGridSpec(
            num_scalar_prefetch=2,
            grid=(batch, hkv, len(pairs)),
            in_specs=[
                pl.BlockSpec((tq, G * dh), lambda b, g, s, qi_ref, kj_ref: (q_tile(b, s, qi_ref), q_blk0 + g)),
                pl.BlockSpec((1, 3 * G, tq), lambda b, g, s, qi_ref, kj_ref: (g, 0, q_tile(b, s, qi_ref))),
                pl.BlockSpec(cmp_block, cmp_map(0)), pl.BlockSpec(cmp_block, cmp_map(1)),
                pl.BlockSpec(k_block, k_map(0)), pl.BlockSpec(v_block, v_map(0)),
                pl.BlockSpec(k_block, k_map(1)), pl.BlockSpec(v_block, v_map(1)),
                pl.BlockSpec((dh, 1), lambda b, g, s, qi_ref, kj_ref: (0, 0)),
                pl.BlockSpec((1, G, 1, 1), lambda b, g, s, qi_ref, kj_ref: (g, 0, 0, 0)),
            ],
            out_specs=pl.BlockSpec((tq, G * dh), lambda b, g, s, qi_ref, kj_ref: (q_tile(b, s, qi_ref), g)),
            scratch_shapes=[pltpu.VMEM((2 * dh, G * tq), jnp.bfloat16), pltpu.VMEM((LANES, tq), jnp.float32),
                            slab, stat, wide, stat, wide],
        ),
        out_shape=jax.ShapeDtypeStruct((n, hkv * G * dh), jnp.float32),
        compiler_params=pltpu.CompilerParams(
            dimension_semantics=("arbitrary",) * 3, vmem_limit_bytes=VMEM_LIMIT_BYTES),
        name="nsa_attn",
    )(qi, kj, proj, gates, cmp_kv, cmp_kv, kn, vv, kn, vv, q_w.reshape(dh, 1), slopes)


def _nsa(proj, batch, q_norm_w, kc_norm_w, ks_norm_w, kw_norm_w, pos_k, pos_v, w_ck1, w_ck2, w_cv1, w_cv2):
    n = proj.shape[0]
    G, hkv = NSA_GROUP, NSA_KV_HEADS
    kn, vv, cf = _kv_prep(proj, ks_norm_w, kw_norm_w, 512, n // batch)
    half = CMP_STRIDE * NSA_HEAD_DIM
    pos = jnp.stack([pos_k.reshape(2, half), pos_v.reshape(2, half)])
    w1 = jnp.stack([w_ck1, w_cv1]).astype(jnp.bfloat16)
    w2 = jnp.stack([w_ck2, w_cv2]).astype(jnp.bfloat16)
    cmp_kv = _cmp_kv(cf, pos, w1, w2, kc_norm_w, batch)
    gates = proj[:, GATE_COL0:GATE_COL0 + N_GATES].reshape(n, hkv, 3 * G).transpose(1, 2, 0)
    slopes = jnp.asarray(2.0 ** (-8.0 * np.arange(1, NSA_Q_HEADS + 1) / NSA_Q_HEADS), jnp.float32) * LOG2E
    return _nsa_attn(proj, gates, cmp_kv, kn, vv, q_norm_w, slopes.reshape(hkv, G, 1, 1), batch)


BIG_NEG = -3.0e38
PEER_CAND = [(a, b) for a in range(PEER_TOPK) for b in range(PEER_TOPK) if (a + 1) * (b + 1) <= PEER_TOPK]
PEER_CAND_ROWS = -(-len(PEER_CAND) // 8) * 8


def _peer_route_body(h_ref, g_ref, wq_ref, keys_ref, xt_ref, rk2_ref, a2_ref, cnt_ref, r_ref,
                     qt_ref, sc_ref, cur_ref, top_ref, cand_ref):
    x = h_ref[...]
    hn = x * lax.rsqrt(jnp.mean(x * x, axis=-1, keepdims=True) + NORM_EPS) * g_ref[...]
    hnt = hn.T.astype(jnp.bfloat16)
    xt_ref[...] = hnt
    qt_ref[...] = jnp.dot(wq_ref[...], hnt, preferred_element_type=jnp.float32).astype(jnp.bfloat16)
    cand_ref[...] = jnp.full(cand_ref.shape, BIG_NEG, jnp.float32)
    kd = PEER_KEY_DIM // 2
    n_parts = 2 * PEER_HEADS
    for hp in range(n_parts):
        s = jnp.dot(keys_ref[hp], qt_ref[hp * kd:(hp + 1) * kd, :], preferred_element_type=jnp.float32)
        sc_ref[hp] = s
        cur_ref[hp] = s

    def extract(k, carry):
        cur = cur_ref[...]
        mk = jnp.max(cur, axis=1, keepdims=True)
        top_ref[k] = mk
        cur_ref[...] = jnp.where(cur == mk, BIG_NEG, cur)
        return carry

    lax.fori_loop(0, PEER_TOPK, extract, 0)

    def head(hd, carry):
        t1 = [top_ref[a, 2 * hd] for a in range(PEER_TOPK)]
        t2 = [top_ref[b, 2 * hd + 1] for b in range(PEER_TOPK)]
        cmax = t1[0] + t2[0]
        cands = [t1[a] + t2[b] for a, b in PEER_CAND]
        for i, c in enumerate(cands):
            cand_ref[i:i + 1, :] = c
        call = cand_ref[...]
        n_gt = jnp.zeros(call.shape, jnp.float32)
        for c in cands:
            n_gt = n_gt + jnp.where(c > call, 1.0, 0.0)
        tau = jnp.min(jnp.where(n_gt <= PEER_TOPK - 1, call, -BIG_NEG), axis=0, keepdims=True)
        z = jnp.sum(jnp.where(call >= tau, jnp.exp(call - cmax), 0.0), axis=0, keepdims=True)
        count = [None] * PEER_TOPK
        for (a, b), c in zip(PEER_CAND, cands):
            hit = jnp.where(c >= tau, 1.0, 0.0)
            count[a] = hit if count[a] is None else count[a] + hit
        s1 = sc_ref[2 * hd]
        s2 = sc_ref[2 * hd + 1]
        cnt = jnp.zeros(s1.shape, jnp.float32)
        rank2 = jnp.zeros(s2.shape, jnp.float32)
        for a in range(PEER_TOPK):
            cnt = jnp.where(s1 == t1[a], count[a], cnt)
            rank2 = rank2 + jnp.where(t2[a] > s2, 1.0, 0.0)
        rk2_ref[hd] = rank2.astype(jnp.bfloat16)
        a2_ref[hd] = jnp.exp(s2 - t2[0]).astype(jnp.bfloat16)
        cnt_ref[hd] = cnt
        r_ref[hd] = jnp.exp(s1 - t1[0]) / z
        return carry

    lax.fori_loop(0, PEER_HEADS, head, 0)


def _peer_route(h, gain, wq_t, keys, tt):
    n, d = h.shape
    hp, nk, kd = keys.shape
    stat = jax.ShapeDtypeStruct((PEER_HEADS, nk, n), jnp.float32)
    stat_bf16 = jax.ShapeDtypeStruct((PEER_HEADS, nk, n), jnp.bfloat16)
    stat_spec = pl.BlockSpec((PEER_HEADS, nk, tt), lambda i: (0, 0, i))
    return pl.pallas_call(
        _peer_route_body,
        grid=(n // tt,),
        in_specs=[
            pl.BlockSpec((tt, d), lambda i: (i, 0)),
            pl.BlockSpec((1, d), lambda i: (0, 0)),
            pl.BlockSpec(wq_t.shape, lambda i: (0, 0)),
            pl.BlockSpec(keys.shape, lambda i: (0, 0, 0)),
        ],
        out_specs=[pl.BlockSpec((d, tt), lambda i: (0, i)), stat_spec, stat_spec, stat_spec, stat_spec],
        out_shape=[jax.ShapeDtypeStruct((d, n), jnp.bfloat16), stat_bf16, stat_bf16, stat, stat],
        scratch_shapes=[pltpu.VMEM((wq_t.shape[0], tt), jnp.bfloat16),
                        pltpu.VMEM((hp, nk, tt), jnp.float32), pltpu.VMEM((hp, nk, tt), jnp.float32),
                        pltpu.VMEM((PEER_TOPK, hp, 1, tt), jnp.float32),
                        pltpu.VMEM((PEER_CAND_ROWS, tt), jnp.float32)],
        compiler_params=pltpu.CompilerParams(
            dimension_semantics=("arbitrary",), vmem_limit_bytes=VMEM_LIMIT_BYTES),
        name="peer_route",
    )(h, gain.reshape(1, d), wq_t, keys)


PEER_I1_PER_PIECE = 2


def _peer_experts_body(xt_ref, u_ref, vt_ref, rk2_ref, a2_ref, cnt_ref, r_ref, o_ref, w_ref, *, n_i1):
    @pl.when(pl.program_id(1) == 0)
    def _():
        o_ref[...] = jnp.zeros(o_ref.shape, jnp.float32)

    tt = xt_ref.shape[1]
    piece = PEER_I1_PER_PIECE * N_KEYS
    n_piece = n_i1 // PEER_I1_PER_PIECE
    parts = []
    bump = None
    for k in range(n_piece + 1):
        if k < n_piece:
            for i1 in range(k * PEER_I1_PER_PIECE, (k + 1) * PEER_I1_PER_PIECE):
                w = None
                for hd in range(PEER_HEADS):
                    picked = rk2_ref[hd] < cnt_ref[hd, i1:i1 + 1, :].astype(jnp.bfloat16)
                    term = jnp.where(picked, a2_ref[hd], 0.0) * r_ref[hd, i1:i1 + 1, :].astype(jnp.bfloat16)
                    w = term if w is None else w + term
                w_ref[i1 * N_KEYS:(i1 + 1) * N_KEYS, :] = w
            last = w[N_KEYS - 16:, tt - LANES:].astype(jnp.float32)
            new_bump = jnp.where(jnp.max(last) > 1e30, 1, 0)
        if k >= 1:
            start = pl.multiple_of((k - 1 + bump) * piece, piece)
            x = jnp.dot(u_ref[pl.ds(start, piece), :], xt_ref[...], preferred_element_type=jnp.float32)
            parts.append(w_ref[(k - 1) * piece:k * piece, :] * _gelu(x).astype(jnp.bfloat16))
        bump = new_bump
    p = jnp.concatenate(parts, axis=0)
    o_ref[...] += jnp.dot(vt_ref[...], p, preferred_element_type=jnp.float32)


def _peer_experts(xt, u, vt, rk2, a2, cnt, r, tt, te):
    d, n = xt.shape
    e = u.shape[0]
    n_i1 = te // N_KEYS
    full = pl.BlockSpec((PEER_HEADS, N_KEYS, tt), lambda i, j: (0, 0, i))
    part = pl.BlockSpec((PEER_HEADS, n_i1, tt), lambda i, j: (0, j, i))
    return pl.pallas_call(
        functools.partial(_peer_experts_body, n_i1=n_i1),
        grid=(n // tt, e // te),
        in_specs=[
            pl.BlockSpec((d, tt), lambda i, j: (0, i)),
            pl.BlockSpec((te, d), lambda i, j: (j, 0)),
            pl.BlockSpec((d, te), lambda i, j: (0, j)),
            full, full, part, part,
        ],
        out_specs=pl.BlockSpec((d, tt), lambda i, j: (0, i)),
        out_shape=jax.ShapeDtypeStruct((d, n), jnp.float32),
        scratch_shapes=[pltpu.VMEM((te, tt), jnp.bfloat16)],
        compiler_params=pltpu.CompilerParams(
            dimension_semantics=("arbitrary", "arbitrary"), vmem_limit_bytes=PEER_VMEM_LIMIT_BYTES),
        name="peer_experts",
    )(xt, u, vt, rk2, a2, cnt, r)


def _transpose_cast_body(x_ref, o_ref):
    o_ref[...] = x_ref[...].T.astype(o_ref.dtype)


def _transpose_cast(x, blk, dtype):
    r, c = x.shape
    return pl.pallas_call(
        _transpose_cast_body,
        grid=(r // blk, c // blk),
        in_specs=[pl.BlockSpec((blk, blk), lambda i, j: (i, j))],
        out_specs=pl.BlockSpec((blk, blk), lambda i, j: (j, i)),
        out_shape=jax.ShapeDtypeStruct((c, r), dtype),
        compiler_params=pltpu.CompilerParams(
            dimension_semantics=("arbitrary", "arbitrary"), vmem_limit_bytes=VMEM_LIMIT_BYTES),
        name="transpose_cast",
    )(x)


def _transpose_add_body(x_ref, r_ref, o_ref):
    o_ref[...] = r_ref[...] + x_ref[...].T


def _transpose_add(x_t, res, blk):
    d, n = x_t.shape
    return pl.pallas_call(
        _transpose_add_body,
        grid=(d // blk, n // blk),
        in_specs=[pl.BlockSpec((blk, blk), lambda i, j: (i, j)),
                  pl.BlockSpec((blk, blk), lambda i, j: (j, i))],
        out_specs=pl.BlockSpec((blk, blk), lambda i, j: (j, i)),
        out_shape=jax.ShapeDtypeStruct((n, d), jnp.float32),
        compiler_params=pltpu.CompilerParams(
            dimension_semantics=("arbitrary", "arbitrary"), vmem_limit_bytes=VMEM_LIMIT_BYTES),
        name="transpose_add",
    )(x_t, res)


def _peer(h, gain, w_q, sub_keys, u_tab, v_tab):
    keys = sub_keys.reshape(PEER_HEADS * 2, N_KEYS, PEER_KEY_DIM // 2).astype(jnp.bfloat16)
    xt, rk2, a2, cnt, r = _peer_route(h, gain, _transpose_cast(w_q, 1024, jnp.bfloat16), keys, 256)
    out_t = _peer_experts(xt, u_tab.astype(jnp.bfloat16), _transpose_cast(v_tab, 1024, jnp.bfloat16),
                          rk2, a2, cnt, r, 512, 2048)
    return _transpose_add(out_t, h, min(1024, h.shape[0]))


def kernel(x, norm1_w, w_in, hg_lb_logits, hg_norm_w, q_norm_w, kc_norm_w, ks_norm_w, kw_norm_w,
           cmp_pos_k, cmp_pos_v, w_ck1, w_ck2, w_cv1, w_cv2, w_out, norm2_w,
           peer_w_q, peer_sub_keys, peer_u, peer_v):
    B, T, D = x.shape
    n = B * T
    layer = 0
    lower_bounds = jnp.cumsum(jax.nn.softmax(hg_lb_logits, axis=0), axis=0)
    xt = x.reshape(n, D)

    w_in_b = jnp.pad(w_in[layer].astype(jnp.bfloat16), ((0, 0), (0, IN_COLS_PADDED - IN_COLS)))
    proj = _norm_matmul(xt, norm1_w[layer], w_in_b, 1024, 1024)
    hg_out = _hgrn2(proj, lower_bounds[layer], hg_norm_w[layer], B)
    nsa_out = _nsa(proj, B, q_norm_w[layer], kc_norm_w[layer], ks_norm_w[layer], kw_norm_w[layer],
                   cmp_pos_k[layer], cmp_pos_v[layer], w_ck1[layer], w_ck2[layer], w_cv1[layer], w_cv2[layer])
    h = _out_proj(hg_out, nsa_out, w_out[layer].astype(jnp.bfloat16), xt, 1024, 1024)

    y = _peer(h, norm2_w[layer], peer_w_q[layer], peer_sub_keys[layer], peer_u[layer], peer_v[layer])
    return y.reshape(B, T, D)
```
